```python
import jax, jax.numpy as jnp
from jax import lax
import numpy as np

D_MODEL = 1024
BATCH = 4
SEQ = 4096
DEPTH = 2
DEC_BATCH = 32
DEC_SEQ = 1
PAST_LEN = 16384
PAGE_SIZE = 128

HEAD_DIM = 64
MIX_WIDTH = D_MODEL
N_HEADS = MIX_WIDTH // HEAD_DIM
N_ATT_HEADS = N_HEADS // 2
N_RWKV_HEADS = N_HEADS - N_ATT_HEADS
ATT_WIDTH = N_ATT_HEADS * HEAD_DIM
RWKV_WIDTH = N_RWKV_HEADS * HEAD_DIM
IN_COLS = 3 * ATT_WIDTH + 3 * RWKV_WIDTH
DILATED_BRANCHES = ((128, 1), (512, 4), (2048, 16))
MAX_WINDOW = 2048
ATT_BLOCK = 128
DECAY_LORA = 64
ICLR_LORA = 64
VRES_LORA = 32
GATE_LORA = 128
N_EXPERTS = 32
TOP_K = 4
D_EXPERT = 1024
SWIGLU_LIMIT = 7.0
SWIGLU_ALPHA = 1.702
NORM_EPS = 1e-5
LNX_EPS = 64e-5

kernel_name = "hymba_dilated_rwkv7_moe_step"


def _rmsnorm(x):
    xf = x.astype(jnp.float32)
    return (xf * lax.rsqrt(jnp.mean(xf * xf, axis=-1, keepdims=True) + NORM_EPS)).astype(x.dtype)


def _branch_prompt(q, k, v, window, dilation):
    B, S, H, E = q.shape
    L = S // dilation
    span = window // dilation
    nb = -(-L // ATT_BLOCK)
    Lp = nb * ATT_BLOCK

    def strided(t):
        return t.reshape(B, L, dilation, H, E).transpose(0, 2, 1, 3, 4)

    qb = jnp.pad(strided(q), ((0, 0), (0, 0), (0, Lp - L), (0, 0), (0, 0)))
    qb = qb.reshape(B, dilation, nb, ATT_BLOCK, H, E)
    kv_pad = ((0, 0), (0, 0), (ATT_BLOCK, Lp - L), (0, 0), (0, 0))
    ks = jnp.pad(strided(k), kv_pad).reshape(B, dilation, nb + 1, ATT_BLOCK, H, E)
    vs = jnp.pad(strided(v), kv_pad).reshape(B, dilation, nb + 1, ATT_BLOCK, H, E)
    kb = jnp.concatenate([ks[:, :, :-1], ks[:, :, 1:]], axis=3)
    vb = jnp.concatenate([vs[:, :, :-1], vs[:, :, 1:]], axis=3)
    s = jnp.einsum("brnqhe,brnkhe->brnhqk", qb, kb) * (E ** -0.5)
    qi = jnp.arange(ATT_BLOCK)[:, None]
    kj = jnp.arange(2 * ATT_BLOCK)[None, :]
    dist = qi + ATT_BLOCK - kj
    key_l = jnp.arange(nb)[:, None, None] * ATT_BLOCK - ATT_BLOCK + kj[None]
    mask = ((dist >= 0) & (dist <= span))[None] & (key_l >= 0)
    s = jnp.where(mask[None, None, :, None], s, -jnp.inf)
    m = jnp.max(s, axis=-1)
    p = jnp.exp(s - m[..., None])
    den = jnp.sum(p, axis=-1)
    num = jnp.einsum("brnhqk,brnkhe->brnqhe", p, vb)

    def unstride(t):
        t = jnp.swapaxes(t[:, :, :L], 1, 2)
        return t.reshape((B, S) + t.shape[3:])

    num = unstride(num.reshape(B, dilation, Lp, H, E))
    den = unstride(jnp.swapaxes(den, 3, 4).reshape(B, dilation, Lp, H))
    m = unstride(jnp.swapaxes(m, 3, 4).reshape(B, dilation, Lp, H))
    return num, den, m


def _branch_decode(q, k_all, v_all, n_buf, window, dilation):
    T = q.shape[1]
    span = window // dilation
    idx = n_buf + jnp.arange(T)[:, None] - dilation * jnp.arange(span + 1)[None, :]
    valid = idx >= 0
    idx = jnp.maximum(idx, 0)
    kg = k_all[:, idx]
    vg = v_all[:, idx]
    s = jnp.einsum("bthe,btjhe->bthj", q, kg) * (q.shape[-1] ** -0.5)
    s = jnp.where(valid[None, :, None, :], s, -jnp.inf)
    m = jnp.max(s, axis=-1)
    p = jnp.exp(s - m[..., None])
    return jnp.einsum("bthj,btjhe->bthe", p, vg), jnp.sum(p, axis=-1), m


def _merge_branches(parts):
    m_all = jnp.max(jnp.stack([m for _, _, m in parts]), axis=0)
    num = sum(n * jnp.exp(m - m_all)[..., None] for n, _, m in parts)
    den = sum(d * jnp.exp(m - m_all) for _, d, m in parts)
    return num / den[..., None]


def _wkv7_scan(r, decay, k, v, a, b, S0):
    def step(S, inp):
        r_t, w_t, k_t, v_t, a_t, b_t = inp
        sa = jnp.einsum("bhvk,bhk->bhv", S, a_t)
        S = S * w_t[:, :, None, :] + sa[..., None] * b_t[:, :, None, :] + v_t[..., None] * k_t[:, :, None, :]
        return S, jnp.einsum("bhvk,bhk->bhv", S, r_t)
    xs = tuple(jnp.moveaxis(t, 1, 0) for t in (r, decay, k, v, a, b))
    S, ys = lax.scan(step, S0.astype(jnp.float32), xs)
    return jnp.moveaxis(ys, 0, 1), S


def _token_mixing(h, h_prev, wkv0, v_first, l, P, k_buf, v_buf):
    B, T, _ = h.shape
    f32 = jnp.float32
    w_in = P["w_in"][l]
    proj = h @ w_in
    heads = lambda t: t.reshape(t.shape[0], t.shape[1], -1, HEAD_DIM)
    qa = proj[..., :ATT_WIDTH]
    ka = proj[..., ATT_WIDTH:2 * ATT_WIDTH]
    va = proj[..., 2 * ATT_WIDTH:3 * ATT_WIDTH]
    qf, kf, vf = (heads(t).astype(f32) for t in (qa, ka, va))

    if k_buf is None:
        parts = [_branch_prompt(qf, kf, vf, w, d) for w, d in DILATED_BRANCHES]
        keep = min(MAX_WINDOW, T)
        k_rows, v_rows = heads(ka)[:, T - keep:], heads(va)[:, T - keep:]
    else:
        n_buf = k_buf.shape[1]
        k_all = jnp.concatenate([k_buf.astype(f32), kf], axis=1)
        v_all = jnp.concatenate([v_buf.astype(f32), vf], axis=1)
        parts = [_branch_decode(qf, k_all, v_all, n_buf, w, d) for w, d in DILATED_BRANCHES]
        k_rows, v_rows = heads(ka), heads(va)
    att = _merge_branches(parts)
    att = _rmsnorm(att).reshape(B, T, ATT_WIDTH) * P["att_out_g"][l]

    rkv = proj[..., 3 * ATT_WIDTH:]
    rkv_prev = (h_prev @ w_in[:, 3 * ATT_WIDTH:])[:, None]
    rkv_shift = jnp.concatenate([rkv_prev, rkv[:, :-1]], axis=1)
    rkv = rkv + (rkv_shift - rkv) * P["mu_rkv"][l].reshape(3 * RWKV_WIDTH)
    r = rkv[..., :RWKV_WIDTH]
    kr = rkv[..., RWKV_WIDTH:2 * RWKV_WIDTH]
    vr = rkv[..., 2 * RWKV_WIDTH:]
    h_shift = jnp.concatenate([h_prev[:, None], h[:, :-1]], axis=1)
    dh = h_shift - h
    mu_w, mu_a, mu_g = P["mu_wag"][l]
    xw, xa, xg = h + dh * mu_w, h + dh * mu_a, h + dh * mu_g
    log_w = -jax.nn.softplus(-(P["decay_w0"][l] + jnp.tanh(xw @ P["decay_w1"][l]) @ P["decay_w2"][l])) - 0.5
    decay = jnp.exp(-jnp.exp(log_w.astype(f32)))
    if l == 0:
        v_first = vr
    else:
        xv = h + dh * P["vres_mu"][l - 1]
        vr = vr + (v_first - vr) * jax.nn.sigmoid(P["vres_v0"][l - 1] + (xv @ P["vres_v1"][l - 1]) @ P["vres_v2"][l - 1])
    a = jax.nn.sigmoid(P["iclr_a0"][l] + (xa @ P["iclr_a1"][l]) @ P["iclr_a2"][l])
    g = jax.nn.sigmoid(xg @ P["gate_g1"][l]) @ P["gate_g2"][l]
    kk = heads(kr * P["k_k"][l]).astype(f32)
    kk = kk / jnp.maximum(jnp.sqrt(jnp.sum(kk * kk, axis=-1, keepdims=True)), 1e-12)
    kr = kr * (1 + (a - 1) * P["k_a"][l])
    rh, kh, vh, ah = (heads(t).astype(f32) for t in (r, kr, vr, a))
    y, wkv_new = _wkv7_scan(rh, heads(decay), kh, vh, -kk, kk * ah, wkv0)
    mean = jnp.mean(y, axis=-1, keepdims=True)
    var = jnp.mean(jnp.square(y - mean), axis=-1, keepdims=True)
    y = ((y - mean) * lax.rsqrt(var + LNX_EPS)).reshape(B, T, RWKV_WIDTH) * P["lnx_w"][l] + P["lnx_b"][l]
    bonus = jnp.sum(rh * kh * P["r_k"][l], axis=-1, keepdims=True) * vh
    y = (y + bonus.reshape(B, T, RWKV_WIDTH)) * g

    out = jnp.concatenate([att.astype(y.dtype), y], axis=-1) @ P["w_out"][l]
    return out, k_rows, v_rows, wkv_new, v_first


def _moe(h, router_w, router_b, w1, b1, w2, b2):
    lead = h.shape[:-1]
    x = h.reshape(-1, D_MODEL)
    T = x.shape[0]
    logits = (x @ router_w).astype(jnp.float32) + router_b.astype(jnp.float32)
    top_val, top_idx = lax.top_k(logits, TOP_K)
    gates = jax.nn.softmax(top_val, axis=-1)
    A = T * TOP_K
    G = max(8, min(128, A // N_EXPERTS))
    n_groups = -(-A // G) + N_EXPERTS
    flat_e = top_idx.reshape(A)
    flat_t = jnp.repeat(jnp.arange(T, dtype=jnp.int32), TOP_K)
    flat_g = gates.reshape(A)
    order = jnp.argsort(flat_e)
    se = flat_e[order]
    counts = jnp.zeros((N_EXPERTS,), jnp.int32).at[flat_e].add(1)
    padded = (counts + G - 1) // G * G
    pad_end = jnp.cumsum(padded)
    pad_start = pad_end - padded
    start = jnp.cumsum(counts) - counts
    dest = pad_start[se] + jnp.arange(A, dtype=jnp.int32) - start[se]
    n_slots = n_groups * G
    slot_tok = jnp.full((n_slots,), T, jnp.int32).at[dest].set(flat_t[order])
    slot_gate = jnp.zeros((n_slots,), jnp.float32).at[dest].set(flat_g[order])
    group_e = jnp.clip(jnp.searchsorted(pad_end, jnp.arange(n_groups, dtype=jnp.int32) * G, side="right"), 0, N_EXPERTS - 1)
    x_pad = jnp.concatenate([x, jnp.zeros((1, D_MODEL), x.dtype)], axis=0)

    def group_fn(args):
        e, toks = args
        xb = x_pad[toks]
        u = xb @ w1[e] + b1[e]
        glu = jnp.minimum(u[:, :D_EXPERT], SWIGLU_LIMIT)
        lin = jnp.clip(u[:, D_EXPERT:], -SWIGLU_LIMIT, SWIGLU_LIMIT)
        act = glu * jax.nn.sigmoid(SWIGLU_ALPHA * glu) * (lin + 1)
        return act @ w2[e] + b2[e]

    yb = lax.map(group_fn, (group_e, slot_tok.reshape(n_groups, G)))
    y = jnp.zeros((T + 1, D_MODEL), jnp.float32).at[slot_tok].add(yb.reshape(n_slots, D_MODEL) * slot_gate[:, None])
    return y[:T].reshape(lead + (D_MODEL,)).astype(h.dtype)


def _trunk(x, c, P, wkv0, h_prev, k_buf, v_buf):
    cs = jax.nn.silu(c)
    new_k, new_v, new_wkv, new_shift = [], [], [], []
    v_first = None
    for l in range(DEPTH):
        mod = (cs @ P["w_ada"][l] + P["b_ada"][l])[:, None, :]
        sh1, sc1, gt1, sh2, sc2, gt2 = jnp.split(mod, 6, axis=-1)
        h = _rmsnorm(x) * (1 + sc1) + sh1
        mix, k_rows, v_rows, wkv, v_first = _token_mixing(
            h, h_prev[l], wkv0[l], v_first, l, P,
            None if k_buf is None else k_buf[l], None if v_buf is None else v_buf[l])
        x = x + gt1 * mix
        h2 = _rmsnorm(x) * (1 + sc2) + sh2
        x = x + gt2 * _moe(h2, P["router_w"][l], P["router_b"][l], P["moe_w1"][l], P["moe_b1"][l], P["moe_w2"][l], P["moe_b2"][l])
        new_k.append(k_rows)
        new_v.append(v_rows)
        new_wkv.append(wkv)
        new_shift.append(h[:, -1])
    y = _rmsnorm(x) * P["final_g"]
    return y, jnp.stack(new_k), jnp.stack(new_v), jnp.stack(new_wkv), jnp.stack(new_shift)


def setup_inputs(seed: int = 0) -> dict:
    key = jax.random.key(seed)
    keys = iter(jax.random.split(key, 64))
    f32 = jnp.float32
    D, RW, DM1 = D_MODEL, RWKV_WIDTH, DEPTH - 1
    n_buf = min(MAX_WINDOW, PAST_LEN)

    def nrm(shape, scale):
        return jax.random.normal(next(keys), shape, f32) * scale

    def uni(shape):
        return jax.random.uniform(next(keys), shape, f32, 0.0, 1.0)

    return {
        "x_prompt": nrm((BATCH, SEQ, D), 1.0),
        "x_sample": nrm((DEC_BATCH, DEC_SEQ, D), 1.0),
        "c_prompt": nrm((BATCH, D), 1.0),
        "c_sample": nrm((DEC_BATCH, D), 1.0),
        "state_attn_k": nrm((DEPTH, DEC_BATCH, n_buf, N_ATT_HEADS, HEAD_DIM), 1.0),
        "state_attn_v": nrm((DEPTH, DEC_BATCH, n_buf, N_ATT_HEADS, HEAD_DIM), 1.0),
        "state_wkv": nrm((DEPTH, DEC_BATCH, N_RWKV_HEADS, HEAD_DIM, HEAD_DIM), 0.1),
        "state_shift": nrm((DEPTH, DEC_BATCH, D), 1.0),
        "w_ada": nrm((DEPTH, D, 6 * D), 0.02),
        "b_ada": nrm((DEPTH, 6 * D), 0.02),
        "w_in": nrm((DEPTH, D, IN_COLS), D ** -0.5),
        "att_out_g": 1.0 + nrm((DEPTH, ATT_WIDTH), 0.02),
        "mu_rkv": uni((DEPTH, 3, RW)),
        "mu_wag": uni((DEPTH, 3, D)),
        "decay_w0": nrm((DEPTH, RW), 0.5),
        "decay_w1": nrm((DEPTH, D, DECAY_LORA), D ** -0.5),
        "decay_w2": nrm((DEPTH, DECAY_LORA, RW), DECAY_LORA ** -0.5),
        "iclr_a0": nrm((DEPTH, RW), 0.5),
        "iclr_a1": nrm((DEPTH, D, ICLR_LORA), D ** -0.5),
        "iclr_a2": nrm((DEPTH, ICLR_LORA, RW), ICLR_LORA ** -0.5),
        "gate_g1": nrm((DEPTH, D, GATE_LORA), D ** -0.5),
        "gate_g2": nrm((DEPTH, GATE_LORA, RW), GATE_LORA ** -0.5),
        "vres_mu": uni((DM1, D)),
        "vres_v0": nrm((DM1, RW), 0.5),
        "vres_v1": nrm((DM1, D, VRES_LORA), D ** -0.5),
        "vres_v2": nrm((DM1, VRES_LORA, RW), VRES_LORA ** -0.5),
        "k_k": 0.85 + nrm((DEPTH, RW), 0.05),
        "k_a": 1.0 + nrm((DEPTH, RW), 0.05),
        "r_k": nrm((DEPTH, N_RWKV_HEADS, HEAD_DIM), 0.1),
        "lnx_w": 1.0 + nrm((DEPTH, RW), 0.02),
        "lnx_b": nrm((DEPTH, RW), 0.02),
        "w_out": nrm((DEPTH, MIX_WIDTH, D), MIX_WIDTH ** -0.5),
        "router_w": nrm((DEPTH, D, N_EXPERTS), D ** -0.5),
        "router_b": nrm((DEPTH, N_EXPERTS), 0.01),
        "moe_w1": nrm((DEPTH, N_EXPERTS, D, 2 * D_EXPERT), D ** -0.5),
        "moe_b1": nrm((DEPTH, N_EXPERTS, 2 * D_EXPERT), 0.01),
        "moe_w2": nrm((DEPTH, N_EXPERTS, D_EXPERT, D), D_EXPERT ** -0.5),
        "moe_b2": nrm((DEPTH, N_EXPERTS, D), 0.01),
        "final_g": 1.0 + nrm((D,), 0.02),
    }


def reference(x_prompt, x_sample, c_prompt, c_sample, state_attn_k, state_attn_v, state_wkv, state_shift,
              w_ada, b_ada, w_in, att_out_g, mu_rkv, mu_wag, decay_w0, decay_w1, decay_w2,
              iclr_a0, iclr_a1, iclr_a2, gate_g1, gate_g2, vres_mu, vres_v0, vres_v1, vres_v2,
              k_k, k_a, r_k, lnx_w, lnx_b, w_out, router_w, router_b, moe_w1, moe_b1, moe_w2, moe_b2,
              final_g):
    P = dict(w_ada=w_ada, b_ada=b_ada, w_in=w_in, att_out_g=att_out_g, mu_rkv=mu_rkv, mu_wag=mu_wag,
             decay_w0=decay_w0, decay_w1=decay_w1, decay_w2=decay_w2, iclr_a0=iclr_a0, iclr_a1=iclr_a1,
             iclr_a2=iclr_a2, gate_g1=gate_g1, gate_g2=gate_g2, vres_mu=vres_mu, vres_v0=vres_v0,
             vres_v1=vres_v1, vres_v2=vres_v2, k_k=k_k, k_a=k_a, r_k=r_k, lnx_w=lnx_w, lnx_b=lnx_b,
             w_out=w_out, router_w=router_w, router_b=router_b, moe_w1=moe_w1, moe_b1=moe_b1,
             moe_w2=moe_w2, moe_b2=moe_b2, final_g=final_g)
    B = x_prompt.shape[0]
    wkv0_prompt = jnp.zeros((DEPTH, B, N_RWKV_HEADS, HEAD_DIM, HEAD_DIM), jnp.float32)
    shift0_prompt = jnp.zeros((DEPTH, B, D_MODEL), x_prompt.dtype)
    y_prompt, k_p, v_p, wkv_p, shift_p = _trunk(x_prompt, c_prompt, P, wkv0_prompt, shift0_prompt, None, None)
    y_sample, k_s, v_s, wkv_s, shift_s = _trunk(x_sample, c_sample, P, state_wkv, state_shift, state_attn_k, state_attn_v)
    return (y_prompt, y_sample, k_p, v_p, wkv_p, shift_p, k_s, v_s, wkv_s, shift_s)
```

```python
import functools
import math

import jax
import jax.numpy as jnp
from jax import lax
from jax.experimental import pallas as pl
from jax.experimental.pallas import tpu as pltpu

F32 = jnp.float32
BF16 = jnp.bfloat16
HIGHEST = lax.Precision.HIGHEST

D_MODEL = 1024
HEAD_DIM = 64
N_ATT_HEADS = 8
N_RWKV_HEADS = 8
ATT_WIDTH = N_ATT_HEADS * HEAD_DIM
RWKV_WIDTH = N_RWKV_HEADS * HEAD_DIM
IN_COLS = 3 * ATT_WIDTH + 3 * RWKV_WIDTH
DILATIONS = (1, 4, 16)
ATT_BLOCK = 128
MAX_WINDOW = 2048
N_EXPERTS = 32
TOP_K = 4
D_EXPERT = 1024
SWIGLU_LIMIT = 7.0
SWIGLU_ALPHA = 1.702
NORM_EPS = 1e-5
LNX_EPS = 64e-5
NEG_BIG = -1e30
WKV_CHUNK = 64
LANES = 128
VMEM_LIMIT = 56 * 1024 * 1024


def _params(n_grid):
    return pltpu.CompilerParams(dimension_semantics=("arbitrary",) * n_grid,
                                vmem_limit_bytes=VMEM_LIMIT)


def _dot(a, b, precision=None):
    return jnp.dot(a, b, preferred_element_type=F32, precision=precision)


def _dot_nt(a, b, precision=None):
    return lax.dot_general(a, b, (((1,), (1,)), ((), ())), preferred_element_type=F32, precision=precision)


def _dot_tn(a, b, precision=None):
    return lax.dot_general(a, b, (((0,), (0,)), ((), ())), preferred_element_type=F32, precision=precision)


def _mm_body(x_ref, w_ref, o_ref, *, precision):
    x = x_ref[...]
    if precision is None:
        x = x.astype(BF16)
    o_ref[...] = _dot(x, w_ref[...], precision)


def _mm(x, w, *, tm=512, tn=None, precision=None):
    M, K = x.shape
    N = w.shape[1]
    tm = min(tm, M)
    tn = N if tn is None else tn
    assert M % tm == 0 and N % tn == 0
    return pl.pallas_call(
        functools.partial(_mm_body, precision=precision),
        grid=(M // tm, N // tn),
        in_specs=[pl.BlockSpec((tm, K), lambda i, j: (i, 0)),
                  pl.BlockSpec((K, tn), lambda i, j: (0, j))],
        out_specs=pl.BlockSpec((tm, tn), lambda i, j: (i, j)),
        out_shape=jax.ShapeDtypeStruct((M, N), F32),
        compiler_params=_params(2),
        name="mm",
    )(x, w)


def _mm_rows(x, w, **kw):
    M = x.shape[0]
    Mp = -(-M // 8) * 8
    if Mp != M:
        x = jnp.pad(x, ((0, Mp - M), (0, 0)))
    return _mm(x, w, **kw)[:M]


def _proj_body(x_ref, w_ref, qkv_ref, rest_ref):
    acc = _dot(x_ref[...].astype(BF16), w_ref[...])
    n_slabs = qkv_ref.shape[0]
    for i in range(n_slabs):
        qkv_ref[i] = acc[:, i * LANES:(i + 1) * LANES]
    rest_ref[...] = acc[:, n_slabs * LANES:]


def _proj(x, w, *, tm=512):
    M, K = x.shape
    N = w.shape[1]
    tm = min(tm, M)
    n_slabs = 3 * ATT_WIDTH // LANES
    n_rest = N - 3 * ATT_WIDTH
    assert M % tm == 0
    return pl.pallas_call(
        _proj_body,
        grid=(M // tm,),
        in_specs=[pl.BlockSpec((tm, K), lambda i: (i, 0)),
                  pl.BlockSpec((K, N), lambda i: (0, 0))],
        out_specs=[pl.BlockSpec((n_slabs, tm, LANES), lambda i: (0, i, 0)),
                   pl.BlockSpec((tm, n_rest), lambda i: (i, 0))],
        out_shape=[jax.ShapeDtypeStruct((n_slabs, M, LANES), F32), jax.ShapeDtypeStruct((M, n_rest), F32)],
        compiler_params=_params(1),
        name="proj",
    )(x, w)


def _att_prompt_body(qkv_ref, g_ref, o_ref, m_ref, l_ref, *, seq):
    scale = HEAD_DIM ** -0.5
    qi = lax.broadcasted_iota(jnp.int32, (ATT_BLOCK, ATT_BLOCK), 0)
    kj = lax.broadcasted_iota(jnp.int32, (ATT_BLOCK, ATT_BLOCK), 1)
    mask_cur = kj <= qi
    lane = lax.broadcasted_iota(jnp.int32, (ATT_BLOCK, LANES), 1)
    first_head = lane < HEAD_DIM
    n_pairs = ATT_WIDTH // LANES

    for branch, dil in enumerate(DILATIONS):
        nb = seq // (ATT_BLOCK * dil)

        def block(bi, carry, dil=dil, nb=nb, branch=branch):
            r = bi // nb
            j = bi % nb
            jp = jnp.maximum(j - 1, 0)
            if dil > 1:
                rows = pl.ds(r + dil * ATT_BLOCK * j, ATT_BLOCK, stride=dil)
                prows = pl.ds(r + dil * ATT_BLOCK * jp, ATT_BLOCK, stride=dil)
            else:
                rows = pl.ds(pl.multiple_of(ATT_BLOCK * j, ATT_BLOCK), ATT_BLOCK)
                prows = pl.ds(pl.multiple_of(ATT_BLOCK * jp, ATT_BLOCK), ATT_BLOCK)
            mp = kj >= qi + jnp.where(j > 0, 0, ATT_BLOCK)
            if branch > 0:
                m_old = m_ref[rows, :]
                l_old = l_ref[rows, :]
            m_new = jnp.zeros((ATT_BLOCK, LANES), F32)
            l_new = jnp.zeros((ATT_BLOCK, LANES), F32)
            for p in range(n_pairs):
                q = qkv_ref[p, rows, :] * scale
                k1 = qkv_ref[n_pairs + p, prows, :].astype(BF16)
                k2 = qkv_ref[n_pairs + p, rows, :].astype(BF16)
                v1 = qkv_ref[2 * n_pairs + p, prows, :]
                v2 = qkv_ref[2 * n_pairs + p, rows, :]
                out = jnp.zeros((ATT_BLOCK, LANES), F32)
                alpha = jnp.zeros((ATT_BLOCK, LANES), F32)
                for hh in range(2):
                    h = 2 * p + hh
                    sel = first_head if hh == 0 else jnp.logical_not(first_head)
                    qh = jnp.where(sel, q, 0.0).astype(BF16)
                    s1 = jnp.where(mp, _dot_nt(qh, k1), NEG_BIG)
                    s2 = jnp.where(mask_cur, _dot_nt(qh, k2), NEG_BIG)
                    mb = jnp.maximum(jnp.max(s1, axis=1, keepdims=True), jnp.max(s2, axis=1, keepdims=True))
                    p1 = jnp.exp(s1 - mb)
                    p2 = jnp.exp(s2 - mb)
                    lb = jnp.sum(p1, axis=1, keepdims=True) + jnp.sum(p2, axis=1, keepdims=True)
                    v1h = jnp.where(sel, v1, 0.0).astype(BF16)
                    v2h = jnp.where(sel, v2, 0.0).astype(BF16)
                    ob = _dot(p1.astype(BF16), v1h) + _dot(p2.astype(BF16), v2h)
                    if branch > 0:
                        mo = m_old[:, h:h + 1]
                        lo = l_old[:, h:h + 1]
                        mn = jnp.maximum(mo, mb)
                        a_old = jnp.exp(mo - mn)
                        a_new = jnp.exp(mb - mn)
                        lb = a_old * lo + a_new * lb
                        ob = ob * a_new
                        alpha = jnp.where(sel, a_old, alpha)
                        mb = mn
                    out = out + ob
                    m_new = jnp.where(lane == h, mb, m_new)
                    l_new = jnp.where(lane == h, lb, l_new)
                if branch > 0:
                    out = out + alpha * o_ref[p, rows, :]
                o_ref[p, rows, :] = out
            m_ref[rows, :] = m_new
            l_ref[rows, :] = l_new
            return carry

        lax.fori_loop(0, seq // ATT_BLOCK, block, 0)

    hr = lax.broadcasted_iota(jnp.int32, (LANES, LANES), 0) // HEAD_DIM
    hc = lax.broadcasted_iota(jnp.int32, (LANES, LANES), 1) // HEAD_DIM
    head_sum = (hr == hc).astype(F32)

    def finish(bi, carry):
        rows = pl.ds(pl.multiple_of(bi * ATT_BLOCK, ATT_BLOCK), ATT_BLOCK)
        l_all = l_ref[rows, :]
        for p in range(n_pairs):
            den = jnp.where(first_head, l_all[:, 2 * p:2 * p + 1], l_all[:, 2 * p + 1:2 * p + 2])
            att = o_ref[p, rows, :] / den
            ms = _dot(att * att, head_sum, HIGHEST) * (1.0 / HEAD_DIM)
            o_ref[p, rows, :] = att * lax.rsqrt(ms + NORM_EPS) * g_ref[:, p * LANES:(p + 1) * LANES]
        return carry

    lax.fori_loop(0, seq // ATT_BLOCK, finish, 0)


def _att_prompt(qkv, B, gain):
    n3, T, _ = qkv.shape
    S = T // B
    n_pairs = n3 // 3
    assert S % (ATT_BLOCK * DILATIONS[-1]) == 0
    return pl.pallas_call(
        functools.partial(_att_prompt_body, seq=S),
        grid=(B,),
        in_specs=[pl.BlockSpec((n3, S, LANES), lambda b: (0, b, 0), pipeline_mode=pl.Buffered(1)),
                  pl.BlockSpec((1, ATT_WIDTH), lambda b: (0, 0))],
        out_specs=pl.BlockSpec((n_pairs, S, LANES), lambda b: (0, b, 0)),
        out_shape=jax.ShapeDtypeStruct((n_pairs, T, LANES), F32),
        scratch_shapes=[pltpu.VMEM((S, LANES), F32), pltpu.VMEM((S, LANES), F32)],
        compiler_params=_params(1),
        name="att_prompt",
    )(qkv, gain.reshape(1, ATT_WIDTH))


def _att_decode_body(q_ref, kn_ref, vn_ref, *refs):
    nd = len(DILATIONS)
    kb_refs, vb_refs, (g_ref, o_ref) = refs[:nd], refs[nd:2 * nd], refs[2 * nd:]
    scale = HEAD_DIM ** -0.5
    seg = (lax.broadcasted_iota(jnp.int32, (ATT_WIDTH, N_ATT_HEADS), 0) // HEAD_DIM
           == lax.broadcasted_iota(jnp.int32, (ATT_WIDTH, N_ATT_HEADS), 1)).astype(F32)
    seg_t = (lax.broadcasted_iota(jnp.int32, (N_ATT_HEADS, ATT_WIDTH), 1) // HEAD_DIM
             == lax.broadcasted_iota(jnp.int32, (N_ATT_HEADS, ATT_WIDTH), 0)).astype(F32)
    rows8 = lambda t: jnp.broadcast_to(t, (8, t.shape[-1]))
    q = rows8(q_ref[...]) * scale
    kn = rows8(kn_ref[...])
    vn = rows8(vn_ref[...])
    s_new = _dot(q * kn, seg, HIGHEST)
    parts = []
    for kb_ref, vb_ref in zip(kb_refs, vb_refs):
        kb = kb_ref[...]
        vb = vb_ref[...]
        s = _dot(kb * q[0:1], seg, HIGHEST)
        m = jnp.maximum(jnp.max(s, axis=0, keepdims=True), s_new)
        p = jnp.exp(s - m[0:1])
        p_new = jnp.exp(s_new - m)
        den = jnp.sum(p, axis=0, keepdims=True) + p_new
        pe = _dot(p, seg_t, HIGHEST)
        num = jnp.sum(pe * vb, axis=0, keepdims=True) + _dot(p_new, seg_t, HIGHEST) * vn
        parts.append((num, den, m))
    m_all = jnp.maximum(jnp.maximum(parts[0][2], parts[1][2]), parts[2][2])
    num = jnp.zeros((8, ATT_WIDTH), F32)
    den = jnp.zeros((8, N_ATT_HEADS), F32)
    for n_b, d_b, m_b in parts:
        w = jnp.exp(m_b - m_all)
        num = num + n_b * _dot(w, seg_t, HIGHEST)
        den = den + d_b * w
    att = num / _dot(den, seg_t, HIGHEST)
    ms = _dot(_dot(att * att, seg, HIGHEST) * (1.0 / HEAD_DIM), seg_t, HIGHEST)
    o_ref[...] = (att * lax.rsqrt(ms + NORM_EPS) * g_ref[...])[0:1]


def _att_decode(q, k_new, v_new, k_buf, v_buf, gain):
    B, n_buf, W = k_buf.shape
    row = pl.BlockSpec((None, 1, W), lambda b: (b, 0, 0))
    views, specs = [], []
    for buf in (k_buf, v_buf):
        for dil in DILATIONS:
            assert n_buf % (ATT_BLOCK * dil) == 0
            views.append(buf.reshape(B, n_buf // dil, dil * W))
            specs.append(pl.BlockSpec((None, ATT_BLOCK, W), lambda b, dil=dil: (b, n_buf // (dil * ATT_BLOCK) - 1, 0)))
    return pl.pallas_call(
        _att_decode_body,
        grid=(B,),
        in_specs=[row, row, row] + specs + [pl.BlockSpec((1, W), lambda b: (0, 0))],
        out_specs=row,
        out_shape=jax.ShapeDtypeStruct((B, 1, W), F32),
        compiler_params=_params(1),
        name="att_decode",
    )(q, k_new, v_new, *views, gain.reshape(1, W))


def _wkv_body(r_ref, lw_ref, k_ref, v_ref, a_ref, b_ref, s0_ref, y_ref, sn_ref, s_scr, *, n_chunks):
    C = WKV_CHUNK
    C2 = 2 * C
    n_pairs = RWKV_WIDTH // LANES
    t = pl.program_id(1)

    @pl.when(t == 0)
    def _():
        s_scr[...] = s0_ref[...]

    row = lax.broadcasted_iota(jnp.int32, (C2, C2), 0)
    col = lax.broadcasted_iota(jnp.int32, (C2, C2), 1)
    same_head = (row // C) == (col // C)
    lower_strict = same_head & (col < row)
    lower_incl = same_head & (col <= row)
    eye = (row == col).astype(F32)
    tri = (lax.broadcasted_iota(jnp.int32, (C, C), 1) <= lax.broadcasted_iota(jnp.int32, (C, C), 0)).astype(F32)
    first_head = lax.broadcasted_iota(jnp.int32, (C, LANES), 1) < HEAD_DIM
    kr = lax.broadcasted_iota(jnp.int32, (LANES, LANES), 0) // HEAD_DIM
    kc = lax.broadcasted_iota(jnp.int32, (LANES, LANES), 1) // HEAD_DIM
    block_diag = kr == kc
    ones = jnp.ones((C, LANES), F32)
    bf = lambda x: x.astype(BF16)

    def split(x):
        return jnp.concatenate([jnp.where(first_head, x, 0.0), jnp.where(first_head, 0.0, x)], axis=0)

    def chunk(ci, carry):
        rows = pl.ds(pl.multiple_of(ci * C, C), C)
        for p in range(n_pairs):
            cols = slice(p * LANES, (p + 1) * LANES)
            r = r_ref[rows, cols]
            lw = lw_ref[rows, cols]
            k = k_ref[rows, cols]
            v = v_ref[rows, cols]
            a = a_ref[rows, cols]
            b = b_ref[rows, cols]
            cum = _dot(tri, lw, HIGHEST)
            g_inv = jnp.exp(-cum)
            x_all = jnp.concatenate([split(a * jnp.exp(cum - lw)), split(r * jnp.exp(cum))], axis=0)
            bt = b * g_inv
            kt = k * g_inv
            y_all = jnp.concatenate([bt, bt, kt, kt], axis=0)
            gram = _dot_nt(bf(x_all), bf(y_all))
            l_ab = jnp.where(lower_strict, gram[0:C2, 0:C2], 0.0)
            m_ak = jnp.where(lower_strict, gram[0:C2, C2:], 0.0)
            m_rb = jnp.where(lower_incl, gram[C2:, 0:C2], 0.0)
            m_rk = jnp.where(lower_incl, gram[C2:, C2:], 0.0)
            t_inv = eye + l_ab
            l_pow = l_ab
            for _ in range(int(math.log2(C)) - 1):
                l_pow = _dot(bf(l_pow), bf(l_pow))
                t_inv = t_inv + _dot(bf(t_inv), bf(l_pow))
            s_t = s_scr[p]
            v_st = split(v)
            xs = _dot(bf(x_all), bf(s_t))
            u_st = _dot(bf(t_inv), bf(xs[0:C2] + _dot(bf(m_ak), bf(v_st))))
            y_st = xs[C2:] + _dot(bf(m_rb), bf(u_st)) + _dot(bf(m_rk), bf(v_st))
            y_ref[rows, cols] = y_st[0:C] + y_st[C:]
            u = u_st[0:C] + u_st[C:]
            tail = jnp.exp(cum[C - 1:C, :] - cum)
            g_col = jnp.exp(_dot_tn(lw, ones, HIGHEST))
            upd = _dot_tn(bf(b * tail), bf(u)) + _dot_tn(bf(k * tail), bf(v))
            s_scr[p] = g_col * s_t + jnp.where(block_diag, upd, 0.0)
        return carry

    lax.fori_loop(0, n_chunks, chunk, 0)

    @pl.when(t == pl.num_programs(1) - 1)
    def _():
        sn_ref[...] = s_scr[...]


def _wkv(r, lw, k, v, a, b, s0, *, chunks_per_step=4):
    B, T, W = r.shape
    H = N_RWKV_HEADS
    n_pairs = W // LANES
    tt = min(T, WKV_CHUNK * chunks_per_step)
    assert T % tt == 0 and tt % WKV_CHUNK == 0
    s0t = jnp.swapaxes(s0, -1, -2).reshape(B, n_pairs, 2, HEAD_DIM, HEAD_DIM)
    z = jnp.zeros_like(s0t[:, :, 0])
    s0bd = jnp.concatenate([jnp.concatenate([s0t[:, :, 0], z], axis=-1),
                            jnp.concatenate([z, s0t[:, :, 1]], axis=-1)], axis=-2)
    seq = pl.BlockSpec((None, tt, W), lambda bi, ti: (bi, ti, 0))
    st = pl.BlockSpec((None, n_pairs, LANES, LANES), lambda bi, ti: (bi, 0, 0, 0))
    y, sbd = pl.pallas_call(
        functools.partial(_wkv_body, n_chunks=tt // WKV_CHUNK),
        grid=(B, T // tt),
        in_specs=[seq] * 6 + [st],
        out_specs=[seq, st],
        out_shape=[jax.ShapeDtypeStruct((B, T, W), F32), jax.ShapeDtypeStruct((B, n_pairs, LANES, LANES), F32)],
        scratch_shapes=[pltpu.VMEM((n_pairs, LANES, LANES), F32)],
        compiler_params=_params(2),
        name="wkv",
    )(r, lw, k, v, a, b, s0bd)
    sn = jnp.stack([sbd[:, :, :HEAD_DIM, :HEAD_DIM], sbd[:, :, HEAD_DIM:, HEAD_DIM:]], axis=2)
    return y, jnp.swapaxes(sn.reshape(B, H, HEAD_DIM, HEAD_DIM), -1, -2)


def _moe_body(ge_ref, gn_ref, idx_ref, idxn_ref, x_hbm, w1_ref, b1_ref, w2_ref, b2_ref, out_hbm,
              xbuf, obuf, w1b, w2b, gsem, ssem, *, G):
    g = pl.program_id(0)
    ng = pl.num_programs(0)
    slot = g % 2

    def gather_copy(tok, i, s):
        return pltpu.make_async_copy(x_hbm.at[pl.ds(tok, 1)], xbuf.at[s, pl.ds(i, 1)], gsem.at[s])

    def scatter_copy(dst, i, s):
        return pltpu.make_async_copy(obuf.at[s, pl.ds(i, 1)], out_hbm.at[pl.ds(dst, 1)], ssem.at[s])

    def start_gather(ref, n, s):
        def body(i, c):
            gather_copy(ref[0, 0, i] // TOP_K, i, s).start()
            return c
        lax.fori_loop(0, n, body, 0)

    def wait_rows(make, n, s):
        def body(i, c):
            make(0, 0, s).wait()
            return c
        lax.fori_loop(0, n, body, 0)

    @pl.when(g == 0)
    def _():
        xbuf[...] = jnp.zeros_like(xbuf)
        start_gather(idx_ref, gn_ref[0], 0)

    @pl.when(g + 1 < ng)
    def _():
        start_gather(idxn_ref, gn_ref[jnp.minimum(g + 1, ng - 1)], 1 - slot)

    n = gn_ref[g]
    wait_rows(gather_copy, n, slot)

    @pl.when((g == 0) | (ge_ref[g] != ge_ref[jnp.maximum(g - 1, 0)]))
    def _():
        w1b[...] = w1_ref[...].astype(BF16)
        w2b[...] = w2_ref[...].astype(BF16)

    @pl.when(g >= 2)
    def _():
        wait_rows(scatter_copy, gn_ref[jnp.maximum(g - 2, 0)], slot)

    @pl.when(n > 0)
    def _():
        x = xbuf[slot].astype(BF16)
        u = _dot(x, w1b[...]) + b1_ref[...]
        glu = jnp.minimum(u[:, :D_EXPERT], SWIGLU_LIMIT)
        lin = jnp.clip(u[:, D_EXPERT:], -SWIGLU_LIMIT, SWIGLU_LIMIT)
        act = glu * jax.nn.sigmoid(SWIGLU_ALPHA * glu) * (lin + 1.0)
        obuf[slot] = _dot(act.astype(BF16), w2b[...]) + b2_ref[...]

        def body(i, c):
            scatter_copy(idx_ref[0, 0, i], i, slot).start()
            return c
        lax.fori_loop(0, n, body, 0)

    @pl.when(g == ng - 1)
    def _():
        wait_rows(scatter_copy, n, slot)

        @pl.when(ng >= 2)
        def _():
            wait_rows(scatter_copy, gn_ref[jnp.maximum(g - 1, 0)], 1 - slot)


def _moe_experts(x, slot_flat, group_e, group_n, w1, b1, w2, b2, *, G):
    T = x.shape[0]
    n_groups = group_e.shape[0]
    idx3 = slot_flat.reshape(n_groups, 1, G)
    smem_blk = lambda f: pl.BlockSpec((1, 1, G), f, memory_space=pltpu.SMEM)
    grid_spec = pltpu.PrefetchScalarGridSpec(
        num_scalar_prefetch=2,
        grid=(n_groups,),
        in_specs=[
            smem_blk(lambda g, ge, gn: (g, 0, 0)),
            smem_blk(lambda g, ge, gn: (jnp.minimum(g + 1, n_groups - 1), 0, 0)),
            pl.BlockSpec(memory_space=pl.ANY),
            pl.BlockSpec((None, D_MODEL, 2 * D_EXPERT), lambda g, ge, gn: (ge[g], 0, 0)),
            pl.BlockSpec((None, 1, 2 * D_EXPERT), lambda g, ge, gn: (ge[g], 0, 0)),
            pl.BlockSpec((None, D_EXPERT, D_MODEL), lambda g, ge, gn: (ge[g], 0, 0)),
            pl.BlockSpec((None, 1, D_MODEL), lambda g, ge, gn: (ge[g], 0, 0)),
        ],
        out_specs=pl.BlockSpec(memory_space=pl.ANY),
        scratch_shapes=[
            pltpu.VMEM((2, G, D_MODEL), F32),
            pltpu.VMEM((2, G, D_MODEL), F32),
            pltpu.VMEM((D_MODEL, 2 * D_EXPERT), BF16),
            pltpu.VMEM((D_EXPERT, D_MODEL), BF16),
            pltpu.SemaphoreType.DMA((2,)),
            pltpu.SemaphoreType.DMA((2,)),
        ],
    )
    return pl.pallas_call(
        functools.partial(_moe_body, G=G),
        grid_spec=grid_spec,
        out_shape=jax.ShapeDtypeStruct((T * TOP_K, D_MODEL), F32),
        compiler_params=_params(1),
        name="moe_experts",
    )(group_e, group_n, idx3, idx3, x, w1, b1.reshape(N_EXPERTS, 1, -1), w2, b2.reshape(N_EXPERTS, 1, -1))


def _moe(h, router_w, router_b, w1, b1, w2, b2):
    lead = h.shape[:-1]
    x = h.reshape(-1, D_MODEL)
    T = x.shape[0]
    rw = jnp.pad(router_w, ((0, 0), (0, LANES - N_EXPERTS)))
    logits = _mm(x, rw, precision=HIGHEST)[:, :N_EXPERTS] + router_b
    top_val, top_idx = lax.top_k(logits, TOP_K)
    gates = jax.nn.softmax(top_val, axis=-1)
    A = T * TOP_K
    G = max(8, min(256, A // N_EXPERTS))
    n_groups = -(-A // G) + N_EXPERTS
    flat_e = top_idx.reshape(A)
    order = jnp.argsort(flat_e).astype(jnp.int32)
    se = flat_e[order]
    counts = jnp.zeros((N_EXPERTS,), jnp.int32).at[flat_e].add(1)
    padded = (counts + G - 1) // G * G
    pad_end = jnp.cumsum(padded)
    pad_start = pad_end - padded
    start = jnp.cumsum(counts) - counts
    dest = pad_start[se] + jnp.arange(A, dtype=jnp.int32) - start[se]
    slot_flat = jnp.zeros((n_groups * G,), jnp.int32).at[dest].set(order)
    g0 = jnp.arange(n_groups, dtype=jnp.int32) * G
    group_e = jnp.clip(jnp.searchsorted(pad_end, g0, side="right"), 0, N_EXPERTS - 1).astype(jnp.int32)
    group_n = jnp.clip(pad_start[group_e] + counts[group_e] - g0, 0, G).astype(jnp.int32)
    group_n = jnp.where(g0 < pad_end[-1], group_n, 0)
    out = _moe_experts(x, slot_flat, group_e, group_n, w1, b1, w2, b2, G=G)
    y = jnp.sum(out.reshape(T, TOP_K, D_MODEL) * gates[:, :, None], axis=1)
    return y.reshape(lead + (D_MODEL,))


def _rmsnorm(x):
    return x * lax.rsqrt(jnp.mean(x * x, axis=-1, keepdims=True) + NORM_EPS)


def _layer_weights(l, P):
    mu_w, mu_a, mu_g = P["mu_wag"][l]
    firsts = [(mu_w, P["decay_w1"][l]), (mu_a, P["iclr_a1"][l]), (mu_g, P["gate_g1"][l])]
    seconds = [P["decay_w2"][l], P["iclr_a2"][l], P["gate_g2"][l]]
    if l > 0:
        firsts.append((P["vres_mu"][l - 1], P["vres_v1"][l - 1]))
        seconds.append(P["vres_v2"][l - 1])
    cur = jnp.concatenate([(1.0 - mu)[:, None] * w for mu, w in firsts], axis=1)
    prev = jnp.concatenate([mu[:, None] * w for mu, w in firsts], axis=1)
    n_lora = cur.shape[1]
    n_all = IN_COLS + 2 * n_lora
    n_pad = -(-n_all // LANES) * LANES
    w_big = jnp.concatenate([P["w_in"][l], cur, prev, jnp.zeros((D_MODEL, n_pad - n_all), F32)], axis=1).astype(BF16)
    k_pad = -(-n_lora // LANES) * LANES
    w2 = jnp.zeros((k_pad, len(seconds) * RWKV_WIDTH), F32)
    r0 = 0
    for i, s in enumerate(seconds):
        w2 = w2.at[r0:r0 + s.shape[0], i * RWKV_WIDTH:(i + 1) * RWKV_WIDTH].set(s)
        r0 += s.shape[0]
    return w_big, n_lora, w2.astype(BF16), k_pad


def _shift(cur, prev):
    return jnp.concatenate([prev[:, None], cur[:, :-1]], axis=1)


def _token_mixing(h, h_prev, wkv0, v_first, l, P, k_buf, v_buf):
    B, T, _ = h.shape
    w_big, n_lora, w_lora2, k_pad = _layer_weights(l, P)
    qkv, proj = _proj(h.reshape(B * T, D_MODEL), w_big)
    proj = proj.reshape(B, T, -1)
    proj_prev = _mm_rows(h_prev, w_big[:, 3 * ATT_WIDTH:])
    n_pairs = ATT_WIDTH // LANES
    unslab = lambda t: jnp.transpose(t.reshape(n_pairs, B, -1, LANES), (1, 2, 0, 3)).reshape(B, -1, ATT_WIDTH)

    if k_buf is None:
        att = unslab(_att_prompt(qkv, B, P["att_out_g"][l]))
        keep = min(MAX_WINDOW, T)
        kv = qkv.reshape(3, n_pairs, B, T, LANES)[1:, :, :, T - keep:]
        k_rows, v_rows = (unslab(kv[i]).reshape(B, keep, N_ATT_HEADS, HEAD_DIM) for i in range(2))
    else:
        assert T == 1
        n_buf = k_buf.shape[1]
        q, kn, vn = (unslab(qkv[i * n_pairs:(i + 1) * n_pairs]) for i in range(3))
        att = _att_decode(q, kn, vn, k_buf.reshape(B, n_buf, ATT_WIDTH), v_buf.reshape(B, n_buf, ATT_WIDTH),
                          P["att_out_g"][l])
        k_rows = kn.reshape(B, T, N_ATT_HEADS, HEAD_DIM)
        v_rows = vn.reshape(B, T, N_ATT_HEADS, HEAD_DIM)

    n_rkv = 3 * RWKV_WIDTH
    rkv = proj[..., :n_rkv]
    rkv_shift = _shift(rkv, proj_prev[:, :n_rkv])
    rkv = rkv + (rkv_shift - rkv) * P["mu_rkv"][l].reshape(n_rkv)
    r, kr, vr = (rkv[..., i * RWKV_WIDTH:(i + 1) * RWKV_WIDTH] for i in range(3))
    c0 = n_rkv
    lora1 = proj[..., c0:c0 + n_lora] + _shift(proj[..., c0 + n_lora:c0 + 2 * n_lora],
                                               proj_prev[:, c0 + n_lora:c0 + 2 * n_lora])
    acts = [jnp.tanh(lora1[..., 0:64]), lora1[..., 64:128], jax.nn.sigmoid(lora1[..., 128:256])]
    if l > 0:
        acts.append(lora1[..., 256:288])
    acts.append(jnp.zeros((B, T, k_pad - n_lora), F32))
    lora2 = _mm(jnp.concatenate(acts, axis=-1).reshape(B * T, k_pad), w_lora2, tm=min(B * T, 512)).reshape(B, T, -1)
    log_w = -jax.nn.softplus(-(P["decay_w0"][l] + lora2[..., 0:RWKV_WIDTH])) - 0.5
    lw = -jnp.exp(log_w)
    a = jax.nn.sigmoid(P["iclr_a0"][l] + lora2[..., RWKV_WIDTH:2 * RWKV_WIDTH])
    g = lora2[..., 2 * RWKV_WIDTH:3 * RWKV_WIDTH]
    if l == 0:
        v_first = vr
    else:
        vr = vr + (v_first - vr) * jax.nn.sigmoid(P["vres_v0"][l - 1] + lora2[..., 3 * RWKV_WIDTH:4 * RWKV_WIDTH])
    heads = lambda t: t.reshape(B, T, N_RWKV_HEADS, HEAD_DIM)
    kk = heads(kr * P["k_k"][l])
    kk = (kk / jnp.maximum(jnp.sqrt(jnp.sum(kk * kk, axis=-1, keepdims=True)), 1e-12)).reshape(B, T, RWKV_WIDTH)
    kr = kr * (1 + (a - 1) * P["k_a"][l])
    Tp = -(-T // WKV_CHUNK) * WKV_CHUNK
    pad = lambda t: jnp.pad(t, ((0, 0), (0, Tp - T), (0, 0)))
    y, wkv_new = _wkv(pad(r), pad(lw), pad(kr), pad(vr), pad(-kk), pad(kk * a), wkv0)
    y = heads(y[:, :T])
    mean = jnp.mean(y, axis=-1, keepdims=True)
    var = jnp.mean(jnp.square(y - mean), axis=-1, keepdims=True)
    y = ((y - mean) * lax.rsqrt(var + LNX_EPS)).reshape(B, T, RWKV_WIDTH) * P["lnx_w"][l] + P["lnx_b"][l]
    bonus = jnp.sum(heads(r) * heads(kr) * P["r_k"][l], axis=-1, keepdims=True) * heads(vr)
    y = (y + bonus.reshape(B, T, RWKV_WIDTH)) * g

    mixed = jnp.concatenate([att, y], axis=-1).reshape(B * T, D_MODEL)
    out = _mm(mixed, P["w_out"][l].astype(BF16)).reshape(B, T, D_MODEL)
    return out, k_rows, v_rows, wkv_new, v_first


def _trunk(x, c, P, wkv0, h_prev, k_buf, v_buf):
    B = x.shape[0]
    cs = jax.nn.silu(c)
    new_k, new_v, new_wkv, new_shift = [], [], [], []
    v_first = None
    for l in range(P["w_ada"].shape[0]):
        mod = (_mm_rows(cs, P["w_ada"][l].astype(BF16), tn=1024) + P["b_ada"][l])[:, None, :]
        sh1, sc1, gt1, sh2, sc2, gt2 = jnp.split(mod, 6, axis=-1)
        h = _rmsnorm(x) * (1 + sc1) + sh1
        mix, k_rows, v_rows, wkv, v_first = _token_mixing(
            h, h_prev[l], wkv0[l], v_first, l, P,
            None if k_buf is None else k_buf[l], None if v_buf is None else v_buf[l])
        x = x + gt1 * mix
        h2 = _rmsnorm(x) * (1 + sc2) + sh2
        x = x + gt2 * _moe(h2, P["router_w"][l], P["router_b"][l], P["moe_w1"][l], P["moe_b1"][l],
                           P["moe_w2"][l], P["moe_b2"][l])
        new_k.append(k_rows)
        new_v.append(v_rows)
        new_wkv.append(wkv)
        new_shift.append(h[:, -1])
    y = _rmsnorm(x) * P["final_g"]
    return y, jnp.stack(new_k), jnp.stack(new_v), jnp.stack(new_wkv), jnp.stack(new_shift)


def kernel(x_prompt, x_sample, c_prompt, c_sample, state_attn_k, state_attn_v, state_wkv, state_shift, w_ada, b_ada, w_in, att_out_g, mu_rkv, mu_wag, decay_w0, decay_w1, decay_w2, iclr_a0, iclr_a1, iclr_a2, gate_g1, gate_g2, vres_mu, vres_v0, vres_v1, vres_v2, k_k, k_a, r_k, lnx_w, lnx_b, w_out, router_w, router_b, moe_w1, moe_b1, moe_w2, moe_b2, final_g):
    P = dict(w_ada=w_ada, b_ada=b_ada, w_in=w_in, att_out_g=att_out_g, mu_rkv=mu_rkv, mu_wag=mu_wag,
             decay_w0=decay_w0, decay_w1=decay_w1, decay_w2=decay_w2, iclr_a0=iclr_a0, iclr_a1=iclr_a1,
             iclr_a2=iclr_a2, gate_g1=gate_g1, gate_g2=gate_g2, vres_mu=vres_mu, vres_v0=vres_v0,
             vres_v1=vres_v1, vres_v2=vres_v2, k_k=k_k, k_a=k_a, r_k=r_k, lnx_w=lnx_w, lnx_b=lnx_b,
             w_out=w_out, router_w=router_w, router_b=router_b, moe_w1=moe_w1, moe_b1=moe_b1,
             moe_w2=moe_w2, moe_b2=moe_b2, final_g=final_g)
    depth = w_ada.shape[0]
    B = x_prompt.shape[0]
    wkv0_prompt = jnp.zeros((depth, B, N_RWKV_HEADS, HEAD_DIM, HEAD_DIM), F32)
    shift0_prompt = jnp.zeros((depth, B, D_MODEL), x_prompt.dtype)
    y_p, k_p, v_p, wkv_p, shift_p = _trunk(x_prompt, c_prompt, P, wkv0_prompt, shift0_prompt, None, None)
    y_s, k_s, v_s, wkv_s, shift_s = _trunk(x_sample, c_sample, P, state_wkv, state_shift, state_attn_k, state_attn_v)
    return (y_p, y_s, k_p, v_p, wkv_p, shift_p, k_s, v_s, wkv_s, shift_s)
```

```python
import functools
import math

import jax
import jax.numpy as jnp
from jax import lax
from jax.experimental import pallas as pl
from jax.experimental.pallas import tpu as pltpu

F32 = jnp.float32
BF16 = jnp.bfloat16
HIGHEST = lax.Precision.HIGHEST

D_MODEL = 1024
HEAD_DIM = 64
N_ATT_HEADS = 8
N_RWKV_HEADS = 8
ATT_WIDTH = N_ATT_HEADS * HEAD_DIM
RWKV_WIDTH = N_RWKV_HEADS * HEAD_DIM
IN_COLS = 3 * ATT_WIDTH + 3 * RWKV_WIDTH
DILATIONS = (1, 4, 16)
ATT_BLOCK = 128
MAX_WINDOW = 2048
N_EXPERTS = 32
TOP_K = 4
D_EXPERT = 1024
SWIGLU_LIMIT = 7.0
SWIGLU_ALPHA = 1.702
NORM_EPS = 1e-5
LNX_EPS = 64e-5
NEG_BIG = -1e30
WKV_CHUNK = 64
LANES = 128
VMEM_LIMIT = 56 * 1024 * 1024
LORA_WIDTHS = (64, 64, 128, 32)
LORA_COLS = 384
N_RKV = 3 * RWKV_WIDTH
N_REST = N_RKV + 2 * LORA_COLS


def _params(n_grid):
    return pltpu.CompilerParams(dimension_semantics=("arbitrary",) * n_grid,
                                vmem_limit_bytes=VMEM_LIMIT)


def _dot(a, b, precision=None):
    return jnp.dot(a, b, preferred_element_type=F32, precision=precision)


def _dot_nt(a, b, precision=None):
    return lax.dot_general(a, b, (((1,), (1,)), ((), ())), preferred_element_type=F32, precision=precision)


def _dot_tn(a, b, precision=None):
    return lax.dot_general(a, b, (((0,), (0,)), ((), ())), preferred_element_type=F32, precision=precision)


def _head_sum_matrix():
    hr = lax.broadcasted_iota(jnp.int32, (LANES, LANES), 0) // HEAD_DIM
    hc = lax.broadcasted_iota(jnp.int32, (LANES, LANES), 1) // HEAD_DIM
    return (hr == hc).astype(F32)


def _rms_mod(x, sc, sh):
    return x * lax.rsqrt(jnp.mean(x * x, axis=-1, keepdims=True) + NORM_EPS) * (1.0 + sc) + sh


def _row_tile(T, S, tm):
    tm = min(tm, T)
    assert T % tm == 0 and (S == 1 or S % tm == 0)
    return tm


def _mod_operand(v, S, tm):
    B, N = v.shape
    if S == 1:
        return v.reshape(1, B, N), pl.BlockSpec((None, tm, N), lambda i: (0, i, 0))
    return v.reshape(B, 1, N), pl.BlockSpec((None, 1, N), lambda i: (i * tm // S, 0, 0))


def _mm_body(x_ref, w_ref, o_ref):
    o_ref[...] = _dot(x_ref[...].astype(BF16), w_ref[...])


def _mm_rows(x, w, *, tn=None):
    M, K = x.shape
    N = w.shape[1]
    Mp = -(-M // 8) * 8
    if Mp != M:
        x = jnp.pad(x, ((0, Mp - M), (0, 0)))
    tn = N if tn is None else tn
    assert N % tn == 0
    out = pl.pallas_call(
        _mm_body,
        grid=(N // tn,),
        in_specs=[pl.BlockSpec((Mp, K), lambda j: (0, 0)),
                  pl.BlockSpec((K, tn), lambda j: (0, j))],
        out_specs=pl.BlockSpec((Mp, tn), lambda j: (0, j)),
        out_shape=jax.ShapeDtypeStruct((Mp, N), F32),
        compiler_params=_params(1),
        name="mm_rows",
    )(x, w)
    return out[:M]


def _pre_proj_body(x_ref, sc_ref, sh_ref, w_ref, qkv_ref, rest_ref):
    h = _rms_mod(x_ref[...], sc_ref[...], sh_ref[...])
    acc = _dot(h.astype(BF16), w_ref[...])
    n_slabs = qkv_ref.shape[0]
    for i in range(n_slabs):
        qkv_ref[i] = acc[:, i * LANES:(i + 1) * LANES]
    rest_ref[...] = acc[:, n_slabs * LANES:]


def _pre_proj(x, sc, sh, w, S, *, tm=512):
    T, K = x.shape
    N = w.shape[1]
    tm = _row_tile(T, S, tm)
    n_slabs = 3 * ATT_WIDTH // LANES
    n_rest = N - 3 * ATT_WIDTH
    sc_op, sc_spec = _mod_operand(sc, S, tm)
    sh_op, sh_spec = _mod_operand(sh, S, tm)
    return pl.pallas_call(
        _pre_proj_body,
        grid=(T // tm,),
        in_specs=[pl.BlockSpec((tm, K), lambda i: (i, 0)), sc_spec, sh_spec,
                  pl.BlockSpec((K, N), lambda i: (0, 0))],
        out_specs=[pl.BlockSpec((n_slabs, tm, LANES), lambda i: (0, i, 0)),
                   pl.BlockSpec((tm, n_rest), lambda i: (i, 0))],
        out_shape=[jax.ShapeDtypeStruct((n_slabs, T, LANES), F32), jax.ShapeDtypeStruct((T, n_rest), F32)],
        compiler_params=_params(1),
        name="pre_proj",
    )(x, sc_op, sh_op, w)


def _att_prompt_body(qkv_ref, g_ref, o_ref, m_ref, l_ref, *, seq):
    scale = HEAD_DIM ** -0.5
    qi = lax.broadcasted_iota(jnp.int32, (ATT_BLOCK, ATT_BLOCK), 0)
    kj = lax.broadcasted_iota(jnp.int32, (ATT_BLOCK, ATT_BLOCK), 1)
    mask_cur = kj <= qi
    lane = lax.broadcasted_iota(jnp.int32, (ATT_BLOCK, LANES), 1)
    first_head = lane < HEAD_DIM
    n_pairs = ATT_WIDTH // LANES

    for branch, dil in enumerate(DILATIONS):
        nb = seq // (ATT_BLOCK * dil)

        def block(bi, carry, dil=dil, nb=nb, branch=branch):
            r = bi // nb
            j = bi % nb
            jp = jnp.maximum(j - 1, 0)
            if dil > 1:
                rows = pl.ds(r + dil * ATT_BLOCK * j, ATT_BLOCK, stride=dil)
                prows = pl.ds(r + dil * ATT_BLOCK * jp, ATT_BLOCK, stride=dil)
            else:
                rows = pl.ds(pl.multiple_of(ATT_BLOCK * j, ATT_BLOCK), ATT_BLOCK)
                prows = pl.ds(pl.multiple_of(ATT_BLOCK * jp, ATT_BLOCK), ATT_BLOCK)
            mp = kj >= qi + jnp.where(j > 0, 0, ATT_BLOCK)
            if branch > 0:
                m_old = m_ref[rows, :]
                l_old = l_ref[rows, :]
            m_new = jnp.zeros((ATT_BLOCK, LANES), F32)
            l_new = jnp.zeros((ATT_BLOCK, LANES), F32)
            for p in range(n_pairs):
                q = qkv_ref[p, rows, :] * scale
                k1 = qkv_ref[n_pairs + p, prows, :].astype(BF16)
                k2 = qkv_ref[n_pairs + p, rows, :].astype(BF16)
                v1 = qkv_ref[2 * n_pairs + p, prows, :]
                v2 = qkv_ref[2 * n_pairs + p, rows, :]
                out = jnp.zeros((ATT_BLOCK, LANES), F32)
                alpha = jnp.zeros((ATT_BLOCK, LANES), F32)
                for hh in range(2):
                    h = 2 * p + hh
                    sel = first_head if hh == 0 else jnp.logical_not(first_head)
                    qh = jnp.where(sel, q, 0.0).astype(BF16)
                    s1 = jnp.where(mp, _dot_nt(qh, k1), NEG_BIG)
                    s2 = jnp.where(mask_cur, _dot_nt(qh, k2), NEG_BIG)
                    mb = jnp.maximum(jnp.max(s1, axis=1, keepdims=True), jnp.max(s2, axis=1, keepdims=True))
                    p1 = jnp.exp(s1 - mb)
                    p2 = jnp.exp(s2 - mb)
                    lb = jnp.sum(p1, axis=1, keepdims=True) + jnp.sum(p2, axis=1, keepdims=True)
                    v1h = jnp.where(sel, v1, 0.0).astype(BF16)
                    v2h = jnp.where(sel, v2, 0.0).astype(BF16)
                    ob = _dot(p1.astype(BF16), v1h) + _dot(p2.astype(BF16), v2h)
                    if branch > 0:
                        mo = m_old[:, h:h + 1]
                        lo = l_old[:, h:h + 1]
                        mn = jnp.maximum(mo, mb)
                        a_old = jnp.exp(mo - mn)
                        a_new = jnp.exp(mb - mn)
                        lb = a_old * lo + a_new * lb
                        ob = ob * a_new
                        alpha = jnp.where(sel, a_old, alpha)
                        mb = mn
                    out = out + ob
                    m_new = jnp.where(lane == h, mb, m_new)
                    l_new = jnp.where(lane == h, lb, l_new)
                if branch > 0:
                    out = out + alpha * o_ref[p, rows, :]
                o_ref[p, rows, :] = out
            m_ref[rows, :] = m_new
            l_ref[rows, :] = l_new
            return carry

        lax.fori_loop(0, seq // ATT_BLOCK, block, 0)

    head_sum = _head_sum_matrix()

    def finish(bi, carry):
        rows = pl.ds(pl.multiple_of(bi * ATT_BLOCK, ATT_BLOCK), ATT_BLOCK)
        l_all = l_ref[rows, :]
        for p in range(n_pairs):
            den = jnp.where(first_head, l_all[:, 2 * p:2 * p + 1], l_all[:, 2 * p + 1:2 * p + 2])
            att = o_ref[p, rows, :] / den
            ms = _dot(att * att, head_sum, HIGHEST) * (1.0 / HEAD_DIM)
            o_ref[p, rows, :] = att * lax.rsqrt(ms + NORM_EPS) * g_ref[:, p * LANES:(p + 1) * LANES]
        return carry

    lax.fori_loop(0, seq // ATT_BLOCK, finish, 0)


def _att_prompt(qkv, B, gain):
    n3, T, _ = qkv.shape
    S = T // B
    n_pairs = n3 // 3
    assert S % (ATT_BLOCK * DILATIONS[-1]) == 0
    return pl.pallas_call(
        functools.partial(_att_prompt_body, seq=S),
        grid=(B,),
        in_specs=[pl.BlockSpec((n3, S, LANES), lambda b: (0, b, 0), pipeline_mode=pl.Buffered(1)),
                  pl.BlockSpec((1, ATT_WIDTH), lambda b: (0, 0))],
        out_specs=pl.BlockSpec((n_pairs, S, LANES), lambda b: (0, b, 0)),
        out_shape=jax.ShapeDtypeStruct((n_pairs, T, LANES), F32),
        scratch_shapes=[pltpu.VMEM((S, LANES), F32), pltpu.VMEM((S, LANES), F32)],
        compiler_params=_params(1),
        name="att_prompt",
    )(qkv, gain.reshape(1, ATT_WIDTH))


def _att_decode_body(q_ref, kn_ref, vn_ref, *refs):
    nd = len(DILATIONS)
    kb_refs, vb_refs, (g_ref, o_ref) = refs[:nd], refs[nd:2 * nd], refs[2 * nd:]
    scale = HEAD_DIM ** -0.5
    seg = (lax.broadcasted_iota(jnp.int32, (ATT_WIDTH, N_ATT_HEADS), 0) // HEAD_DIM
           == lax.broadcasted_iota(jnp.int32, (ATT_WIDTH, N_ATT_HEADS), 1)).astype(F32)
    seg_t = (lax.broadcasted_iota(jnp.int32, (N_ATT_HEADS, ATT_WIDTH), 1) // HEAD_DIM
             == lax.broadcasted_iota(jnp.int32, (N_ATT_HEADS, ATT_WIDTH), 0)).astype(F32)
    rows8 = lambda t: jnp.broadcast_to(t, (8, t.shape[-1]))
    q = rows8(q_ref[...]) * scale
    kn = rows8(kn_ref[...])
    vn = rows8(vn_ref[...])
    s_new = _dot(q * kn, seg, HIGHEST)
    parts = []
    for kb_ref, vb_ref in zip(kb_refs, vb_refs):
        kb = kb_ref[...]
        vb = vb_ref[...]
        s = _dot(kb * q[0:1], seg, HIGHEST)
        m = jnp.maximum(jnp.max(s, axis=0, keepdims=True), s_new)
        p = jnp.exp(s - m[0:1])
        p_new = jnp.exp(s_new - m)
        den = jnp.sum(p, axis=0, keepdims=True) + p_new
        pe = _dot(p, seg_t, HIGHEST)
        num = jnp.sum(pe * vb, axis=0, keepdims=True) + _dot(p_new, seg_t, HIGHEST) * vn
        parts.append((num, den, m))
    m_all = jnp.maximum(jnp.maximum(parts[0][2], parts[1][2]), parts[2][2])
    num = jnp.zeros((8, ATT_WIDTH), F32)
    den = jnp.zeros((8, N_ATT_HEADS), F32)
    for n_b, d_b, m_b in parts:
        w = jnp.exp(m_b - m_all)
        num = num + n_b * _dot(w, seg_t, HIGHEST)
        den = den + d_b * w
    att = num / _dot(den, seg_t, HIGHEST)
    ms = _dot(_dot(att * att, seg, HIGHEST) * (1.0 / HEAD_DIM), seg_t, HIGHEST)
    o_ref[...] = (att * lax.rsqrt(ms + NORM_EPS) * g_ref[...])[0:1]


def _att_decode(q, k_new, v_new, k_buf, v_buf, gain):
    B, n_buf, W = k_buf.shape
    row = pl.BlockSpec((None, 1, W), lambda b: (b, 0, 0))
    views, specs = [], []
    for buf in (k_buf, v_buf):
        for dil in DILATIONS:
            assert n_buf % (ATT_BLOCK * dil) == 0
            views.append(buf.reshape(B, n_buf // dil, dil * W))
            specs.append(pl.BlockSpec((None, ATT_BLOCK, W), lambda b, dil=dil: (b, n_buf // (dil * ATT_BLOCK) - 1, 0)))
    return pl.pallas_call(
        _att_decode_body,
        grid=(B,),
        in_specs=[row, row, row] + specs + [pl.BlockSpec((1, W), lambda b: (0, 0))],
        out_specs=row,
        out_shape=jax.ShapeDtypeStruct((B, 1, W), F32),
        compiler_params=_params(1),
        name="att_decode",
    )(q, k_new, v_new, *views, gain.reshape(1, W))


def _rwkv_prep_body(*refs, S, tm, has_vfirst):
    if has_vfirst:
        cur_ref, prev_ref, mu_ref, vec_ref, w2_ref, vf_ref = refs[:6]
        rest = refs[6:]
    else:
        cur_ref, prev_ref, mu_ref, vec_ref, w2_ref = refs[:5]
        vf_ref, rest = None, refs[5:]
    r_ref, lw_ref, k_ref, v_ref, a_ref, b_ref, g_ref, carry = rest
    c_lp = N_RKV + LORA_COLS
    rkv = cur_ref[:, :N_RKV]
    lp = cur_ref[:, c_lp:]
    if S == 1:
        rkv_prev = prev_ref[:, :N_RKV]
        lp_prev = prev_ref[:, c_lp:]
    else:
        i = pl.program_id(0)

        @pl.when(i == 0)
        def _():
            carry[...] = jnp.zeros_like(carry)

        first = (i % (S // tm)) == 0
        prev_row = jnp.where(first, prev_ref[...], carry[...])
        row0 = lax.broadcasted_iota(jnp.int32, (tm, 1), 0) == 0
        rkv_prev = jnp.where(row0, prev_row[:, :N_RKV], pltpu.roll(rkv, 1, 0))
        lp_prev = jnp.where(row0, prev_row[:, c_lp:], pltpu.roll(lp, 1, 0))
        carry[...] = cur_ref[tm - 1:tm, :]
    rkv = rkv + (rkv_prev - rkv) * mu_ref[...]
    r = rkv[:, :RWKV_WIDTH]
    kr = rkv[:, RWKV_WIDTH:2 * RWKV_WIDTH]
    vr = rkv[:, 2 * RWKV_WIDTH:]
    l1 = cur_ref[:, N_RKV:c_lp] + lp_prev
    t0 = l1[:, :LANES]
    lane = lax.broadcasted_iota(jnp.int32, (tm, LANES), 1)
    act = jnp.concatenate([jnp.where(lane < LORA_WIDTHS[0], jnp.tanh(t0), t0),
                           jax.nn.sigmoid(l1[:, LANES:2 * LANES]),
                           l1[:, 2 * LANES:]], axis=1)
    l2 = _dot(act.astype(BF16), w2_ref[...])
    w0, a0, v0, k_k, k_a = (vec_ref[j:j + 1, :] for j in range(5))
    z = w0 + l2[:, :RWKV_WIDTH]
    softplus_neg = jnp.maximum(-z, 0.0) + jnp.log(1.0 + jnp.exp(-jnp.abs(z)))
    lw_ref[...] = -jnp.exp(-softplus_neg - 0.5)
    a = jax.nn.sigmoid(a0 + l2[:, RWKV_WIDTH:2 * RWKV_WIDTH])
    g_ref[...] = l2[:, 2 * RWKV_WIDTH:3 * RWKV_WIDTH]
    if has_vfirst:
        vr = vr + (vf_ref[...] - vr) * jax.nn.sigmoid(v0 + l2[:, 3 * RWKV_WIDTH:])
    kk = kr * k_k
    head_sum = _head_sum_matrix()
    ss = jnp.concatenate([_dot(kk[:, p * LANES:(p + 1) * LANES] ** 2, head_sum, HIGHEST)
                          for p in range(RWKV_WIDTH // LANES)], axis=1)
    kk = kk / jnp.maximum(jnp.sqrt(ss), 1e-12)
    r_ref[...] = r
    k_ref[...] = kr * (1.0 + (a - 1.0) * k_a)
    v_ref[...] = vr
    a_ref[...] = -kk
    b_ref[...] = kk * a


def _rwkv_prep(rest, rest_prev, mu_rkv, vecs, w2, v_first, S, *, tm=256):
    T = rest.shape[0]
    tm = _row_tile(T, S, tm)
    prev_op, prev_spec = _mod_operand(rest_prev, S, tm)
    tile = pl.BlockSpec((tm, RWKV_WIDTH), lambda i: (i, 0))
    const = lambda a: pl.BlockSpec(a.shape, lambda i: (0,) * a.ndim)
    ops = [rest, prev_op, mu_rkv, vecs, w2]
    specs = [pl.BlockSpec((tm, N_REST), lambda i: (i, 0)), prev_spec, const(mu_rkv), const(vecs), const(w2)]
    if v_first is not None:
        ops.append(v_first)
        specs.append(tile)
    return pl.pallas_call(
        functools.partial(_rwkv_prep_body, S=S, tm=tm, has_vfirst=v_first is not None),
        grid=(T // tm,),
        in_specs=specs,
        out_specs=[tile] * 7,
        out_shape=[jax.ShapeDtypeStruct((T, RWKV_WIDTH), F32)] * 7,
        scratch_shapes=[pltpu.VMEM((1, N_REST), F32)],
        compiler_params=_params(1),
        name="rwkv_prep",
    )(*ops)


def _wkv_body(r_ref, lw_ref, k_ref, v_ref, a_ref, b_ref, s0_ref, y_ref, sn_ref, s_scr, *, n_chunks):
    C = WKV_CHUNK
    C2 = 2 * C
    n_pairs = RWKV_WIDTH // LANES
    t = pl.program_id(1)

    @pl.when(t == 0)
    def _():
        s_scr[...] = s0_ref[...]

    row = lax.broadcasted_iota(jnp.int32, (C2, C2), 0)
    col = lax.broadcasted_iota(jnp.int32, (C2, C2), 1)
    same_head = (row // C) == (col // C)
    lower_strict = same_head & (col < row)
    lower_incl = same_head & (col <= row)
    eye = (row == col).astype(F32)
    tri = (lax.broadcasted_iota(jnp.int32, (C, C), 1) <= lax.broadcasted_iota(jnp.int32, (C, C), 0)).astype(F32)
    first_head = lax.broadcasted_iota(jnp.int32, (C, LANES), 1) < HEAD_DIM
    block_diag = _head_sum_matrix() > 0.5
    ones = jnp.ones((C, LANES), F32)
    bf = lambda x: x.astype(BF16)

    def split(x):
        return jnp.concatenate([jnp.where(first_head, x, 0.0), jnp.where(first_head, 0.0, x)], axis=0)

    def chunk(ci, carry):
        rows = pl.ds(pl.multiple_of(ci * C, C), C)
        for p in range(n_pairs):
            cols = slice(p * LANES, (p + 1) * LANES)
            r = r_ref[rows, cols]
            lw = lw_ref[rows, cols]
            k = k_ref[rows, cols]
            v = v_ref[rows, cols]
            a = a_ref[rows, cols]
            b = b_ref[rows, cols]
            cum = _dot(tri, lw, HIGHEST)
            g_inv = jnp.exp(-cum)
            x_all = jnp.concatenate([split(a * jnp.exp(cum - lw)), split(r * jnp.exp(cum))], axis=0)
            bt = b * g_inv
            kt = k * g_inv
            y_all = jnp.concatenate([bt, bt, kt, kt], axis=0)
            gram = _dot_nt(bf(x_all), bf(y_all))
            l_ab = jnp.where(lower_strict, gram[0:C2, 0:C2], 0.0)
            m_ak = jnp.where(lower_strict, gram[0:C2, C2:], 0.0)
            m_rb = jnp.where(lower_incl, gram[C2:, 0:C2], 0.0)
            m_rk = jnp.where(lower_incl, gram[C2:, C2:], 0.0)
            t_inv = eye + l_ab
            l_pow = l_ab
            for _ in range(int(math.log2(C)) - 1):
                l_pow = _dot(bf(l_pow), bf(l_pow))
                t_inv = t_inv + _dot(bf(t_inv), bf(l_pow))
            s_t = s_scr[p]
            v_st = split(v)
            xs = _dot(bf(x_all), bf(s_t))
            u_st = _dot(bf(t_inv), bf(xs[0:C2] + _dot(bf(m_ak), bf(v_st))))
            y_st = xs[C2:] + _dot(bf(m_rb), bf(u_st)) + _dot(bf(m_rk), bf(v_st))
            y_ref[rows, cols] = y_st[0:C] + y_st[C:]
            u = u_st[0:C] + u_st[C:]
            tail = jnp.exp(cum[C - 1:C, :] - cum)
            g_col = jnp.exp(_dot_tn(lw, ones, HIGHEST))
            upd = _dot_tn(bf(b * tail), bf(u)) + _dot_tn(bf(k * tail), bf(v))
            s_scr[p] = g_col * s_t + jnp.where(block_diag, upd, 0.0)
        return carry

    lax.fori_loop(0, n_chunks, chunk, 0)

    @pl.when(t == pl.num_programs(1) - 1)
    def _():
        sn_ref[...] = s_scr[...]


def _wkv(r, lw, k, v, a, b, s0, *, chunks_per_step=4):
    B, T, W = r.shape
    H = N_RWKV_HEADS
    n_pairs = W // LANES
    tt = min(T, WKV_CHUNK * chunks_per_step)
    assert T % tt == 0 and tt % WKV_CHUNK == 0
    s0t = jnp.swapaxes(s0, -1, -2).reshape(B, n_pairs, 2, HEAD_DIM, HEAD_DIM)
    z = jnp.zeros_like(s0t[:, :, 0])
    s0bd = jnp.concatenate([jnp.concatenate([s0t[:, :, 0], z], axis=-1),
                            jnp.concatenate([z, s0t[:, :, 1]], axis=-1)], axis=-2)
    seq = pl.BlockSpec((None, tt, W), lambda bi, ti: (bi, ti, 0))
    st = pl.BlockSpec((None, n_pairs, LANES, LANES), lambda bi, ti: (bi, 0, 0, 0))
    y, sbd = pl.pallas_call(
        functools.partial(_wkv_body, n_chunks=tt // WKV_CHUNK),
        grid=(B, T // tt),
        in_specs=[seq] * 6 + [st],
        out_specs=[seq, st],
        out_shape=[jax.ShapeDtypeStruct((B, T, W), F32), jax.ShapeDtypeStruct((B, n_pairs, LANES, LANES), F32)],
        scratch_shapes=[pltpu.VMEM((n_pairs, LANES, LANES), F32)],
        compiler_params=_params(2),
        name="wkv",
    )(r, lw, k, v, a, b, s0bd)
    sn = jnp.stack([sbd[:, :, :HEAD_DIM, :HEAD_DIM], sbd[:, :, HEAD_DIM:, HEAD_DIM:]], axis=2)
    return y, jnp.swapaxes(sn.reshape(B, H, HEAD_DIM, HEAD_DIM), -1, -2)


def _out_proj_body(att_ref, y_ref, r_ref, k_ref, v_ref, g_ref, x_ref, gt_ref, sc_ref, sh_ref,
                   vec_ref, wo_ref, rw_ref, rb_ref, x1_ref, h2_ref, logit_ref):
    head_sum = _head_sum_matrix()
    n_pairs = RWKV_WIDTH // LANES
    parts = [att_ref[p].astype(BF16) for p in range(ATT_WIDTH // LANES)]
    for p in range(n_pairs):
        cols = slice(p * LANES, (p + 1) * LANES)
        y = y_ref[:, cols]
        mean = _dot(y, head_sum, HIGHEST) * (1.0 / HEAD_DIM)
        d = y - mean
        var = _dot(d * d, head_sum, HIGHEST) * (1.0 / HEAD_DIM)
        yn = d * lax.rsqrt(var + LNX_EPS) * vec_ref[0:1, cols] + vec_ref[1:2, cols]
        rk = r_ref[:, cols] * k_ref[:, cols] * vec_ref[2:3, cols]
        bonus = _dot(rk, head_sum, HIGHEST) * v_ref[:, cols]
        parts.append(((yn + bonus) * g_ref[:, cols]).astype(BF16))
    mix = _dot(jnp.concatenate(parts, axis=1), wo_ref[...])
    x1 = x_ref[...] + gt_ref[...] * mix
    x1_ref[...] = x1
    h2 = _rms_mod(x1, sc_ref[...], sh_ref[...])
    h2_ref[...] = h2
    logit_ref[...] = _dot(h2, rw_ref[...], HIGHEST) + rb_ref[...]


def _out_proj(att, y, r, k, v, g, x, gt, sc, sh, vecs, w_out, rw, rb, S, *, tm=256):
    T = x.shape[0]
    tm = _row_tile(T, S, tm)
    n_pairs = att.shape[0]
    tile = lambda n: pl.BlockSpec((tm, n), lambda i: (i, 0))
    const = lambda a: pl.BlockSpec(a.shape, lambda i: (0,) * a.ndim)
    mods = [_mod_operand(m, S, tm) for m in (gt, sc, sh)]
    return pl.pallas_call(
        _out_proj_body,
        grid=(T // tm,),
        in_specs=[pl.BlockSpec((n_pairs, tm, LANES), lambda i: (0, i, 0))] + [tile(RWKV_WIDTH)] * 5 + [tile(D_MODEL)]
                 + [m[1] for m in mods] + [const(vecs), const(w_out), const(rw), const(rb)],
        out_specs=[tile(D_MODEL), tile(D_MODEL), tile(LANES)],
        out_shape=[jax.ShapeDtypeStruct((T, D_MODEL), F32), jax.ShapeDtypeStruct((T, D_MODEL), F32),
                   jax.ShapeDtypeStruct((T, LANES), F32)],
        compiler_params=_params(1),
        name="out_proj",
    )(att, y, r, k, v, g, x, *[m[0] for m in mods], vecs, w_out, rw, rb)


def _moe_body(ge_ref, gidx_ref, sidx_ref, x_hbm, w1_ref, b1_ref, w2_ref, b2_ref, out_hbm,
              xb0, xb1, ob0, ob1, w1b, w2b, gsem, ssem, *, G, ng):
    s = pl.program_id(0)

    def step(cur, x_cur, x_oth, o_cur, o_oth):
        oth = 1 - cur

        def start_gather():
            for i in range(G):
                pltpu.make_async_copy(x_hbm.at[pl.ds(gidx_ref[0, 0, i], 1)], x_cur.at[pl.ds(i, 1)],
                                      gsem.at[cur]).start()

        def start_scatter():
            for i in range(G):
                pltpu.make_async_copy(o_cur.at[pl.ds(i, 1)], out_hbm.at[pl.ds(sidx_ref[0, 0, i], 1)],
                                      ssem.at[cur]).start()

        def wait_scatter(buf, b):
            pltpu.make_async_copy(buf, out_hbm.at[pl.ds(0, G)], ssem.at[b]).wait()

        def evaluate():
            x = x_oth[...].astype(BF16)
            u = _dot(x, w1b[...]) + b1_ref[...]
            glu = jnp.minimum(u[:, :D_EXPERT], SWIGLU_LIMIT)
            lin = jnp.clip(u[:, D_EXPERT:], -SWIGLU_LIMIT, SWIGLU_LIMIT)
            act = glu * jax.nn.sigmoid(SWIGLU_ALPHA * glu) * (lin + 1.0)
            o_oth[...] = _dot(act.astype(BF16), w2b[...]) + b2_ref[...]

        @pl.when((s >= 1) & (s <= ng))
        def _():
            pltpu.make_async_copy(x_hbm.at[pl.ds(0, G)], x_oth, gsem.at[oth]).wait()
            e = jnp.clip(s - 1, 0, ng - 1)

            @pl.when((s == 1) | (ge_ref[e] != ge_ref[jnp.maximum(e - 1, 0)]))
            def _():
                w1b[...] = w1_ref[...].astype(BF16)
                w2b[...] = w2_ref[...].astype(BF16)

        @pl.when(s >= 3)
        def _():
            wait_scatter(o_oth, oth)

        @pl.when(s == 0)
        def _():
            start_gather()

        @pl.when(s == 1)
        def _():
            start_gather()
            evaluate()

        @pl.when((s >= 2) & (s < ng))
        def _():
            start_gather()
            start_scatter()
            evaluate()

        @pl.when(s == ng)
        def _():
            start_scatter()
            evaluate()

        @pl.when(s == ng + 1)
        def _():
            start_scatter()
            wait_scatter(o_cur, cur)

    @pl.when(s % 2 == 0)
    def _():
        step(0, xb0, xb1, ob0, ob1)

    @pl.when(s % 2 == 1)
    def _():
        step(1, xb1, xb0, ob1, ob0)


def _moe_experts(x, gather_idx, scatter_idx, group_e, w1, b1, w2, b2, *, G):
    T = x.shape[0]
    ng = group_e.shape[0]
    assert ng >= 2
    gidx = gather_idx.reshape(ng, 1, G)
    sidx = scatter_idx.reshape(ng, 1, G)
    smem_blk = lambda f: pl.BlockSpec((1, 1, G), f, memory_space=pltpu.SMEM)
    expert = lambda s, ge: (ge[jnp.clip(s - 1, 0, ng - 1)], 0, 0)
    grid_spec = pltpu.PrefetchScalarGridSpec(
        num_scalar_prefetch=1,
        grid=(ng + 2,),
        in_specs=[
            smem_blk(lambda s, ge: (jnp.minimum(s, ng - 1), 0, 0)),
            smem_blk(lambda s, ge: (jnp.clip(s - 2, 0, ng - 1), 0, 0)),
            pl.BlockSpec(memory_space=pl.ANY),
            pl.BlockSpec((None, D_MODEL, 2 * D_EXPERT), expert),
            pl.BlockSpec((None, 1, 2 * D_EXPERT), expert),
            pl.BlockSpec((None, D_EXPERT, D_MODEL), expert),
            pl.BlockSpec((None, 1, D_MODEL), expert),
        ],
        out_specs=pl.BlockSpec(memory_space=pl.ANY),
        scratch_shapes=[
            pltpu.VMEM((G, D_MODEL), F32),
            pltpu.VMEM((G, D_MODEL), F32),
            pltpu.VMEM((G, D_MODEL), F32),
            pltpu.VMEM((G, D_MODEL), F32),
            pltpu.VMEM((D_MODEL, 2 * D_EXPERT), BF16),
            pltpu.VMEM((D_EXPERT, D_MODEL), BF16),
            pltpu.SemaphoreType.DMA((2,)),
            pltpu.SemaphoreType.DMA((2,)),
        ],
    )
    return pl.pallas_call(
        functools.partial(_moe_body, G=G, ng=ng),
        grid_spec=grid_spec,
        out_shape=jax.ShapeDtypeStruct((T * TOP_K + G, D_MODEL), F32),
        compiler_params=_params(1),
        name="moe_experts",
    )(group_e, gidx, sidx, x, w1, b1.reshape(N_EXPERTS, 1, -1), w2, b2.reshape(N_EXPERTS, 1, -1))


def _combine_body(*refs, final):
    outs = refs[:TOP_K]
    w_ref, x_ref, gt_ref = refs[TOP_K:TOP_K + 3]
    w = w_ref[...]
    acc = outs[0][...] * w[:, 0:1]
    for kk in range(1, TOP_K):
        acc = acc + outs[kk][...] * w[:, kk:kk + 1]
    x2 = x_ref[...] + gt_ref[...] * acc
    if final:
        fg_ref, o_ref = refs[TOP_K + 3:]
        o_ref[...] = x2 * lax.rsqrt(jnp.mean(x2 * x2, axis=-1, keepdims=True) + NORM_EPS) * fg_ref[...]
    else:
        refs[TOP_K + 3][...] = x2


def _combine(moe_out, weights, x, gt, final_g, S, *, tm=512):
    T = x.shape[0]
    tm = _row_tile(T, S, tm)
    gt_op, gt_spec = _mod_operand(gt, S, tm)
    tile = pl.BlockSpec((tm, D_MODEL), lambda i: (i, 0))
    ops = [moe_out] * TOP_K + [weights, x, gt_op]
    specs = [pl.BlockSpec((tm, D_MODEL), lambda i, kk=kk: (kk * (T // tm) + i, 0)) for kk in range(TOP_K)]
    specs += [pl.BlockSpec((tm, TOP_K), lambda i: (i, 0)), tile, gt_spec]
    if final_g is not None:
        ops.append(final_g.reshape(1, D_MODEL))
        specs.append(pl.BlockSpec((1, D_MODEL), lambda i: (0, 0)))
    return pl.pallas_call(
        functools.partial(_combine_body, final=final_g is not None),
        grid=(T // tm,),
        in_specs=specs,
        out_specs=tile,
        out_shape=jax.ShapeDtypeStruct((T, D_MODEL), F32),
        compiler_params=_params(1),
        name="combine",
    )(*ops)


def _route(logits, T):
    top_val, top_idx = lax.top_k(logits, TOP_K)
    weights = jax.nn.softmax(top_val, axis=-1)
    A = T * TOP_K
    G = max(8, min(256, A // N_EXPERTS))
    ng = -(-A // G) + N_EXPERTS
    flat_e = top_idx.reshape(A)
    order = jnp.argsort(flat_e).astype(jnp.int32)
    se = flat_e[order]
    counts = jnp.zeros((N_EXPERTS,), jnp.int32).at[flat_e].add(1)
    padded = (counts + G - 1) // G * G
    pad_end = jnp.cumsum(padded)
    pad_start = pad_end - padded
    start = jnp.cumsum(counts) - counts
    dest = pad_start[se] + jnp.arange(A, dtype=jnp.int32) - start[se]
    slot = jnp.arange(ng * G, dtype=jnp.int32)
    gather_idx = jnp.zeros((ng * G,), jnp.int32).at[dest].set(order // TOP_K)
    scatter_idx = (A + slot % G).at[dest].set((order % TOP_K) * T + order // TOP_K)
    g0 = jnp.arange(ng, dtype=jnp.int32) * G
    group_e = jnp.clip(jnp.searchsorted(pad_end, g0, side="right"), 0, N_EXPERTS - 1).astype(jnp.int32)
    return weights, gather_idx, scatter_idx, group_e, G


def _layer_weights(l, P):
    mu_w, mu_a, mu_g = P["mu_wag"][l]
    firsts = [(mu_w, P["decay_w1"][l]), (mu_a, P["iclr_a1"][l]), (mu_g, P["gate_g1"][l])]
    seconds = [P["decay_w2"][l], P["iclr_a2"][l], P["gate_g2"][l]]
    if l > 0:
        firsts.append((P["vres_mu"][l - 1], P["vres_v1"][l - 1]))
        seconds.append(P["vres_v2"][l - 1])
    n_used = sum(w.shape[1] for _, w in firsts)
    zpad = jnp.zeros((D_MODEL, LORA_COLS - n_used), F32)
    cur = jnp.concatenate([(1.0 - mu)[:, None] * w for mu, w in firsts] + [zpad], axis=1)
    prev = jnp.concatenate([mu[:, None] * w for mu, w in firsts] + [zpad], axis=1)
    w_big = jnp.concatenate([P["w_in"][l], cur, prev], axis=1).astype(BF16)
    w2 = jnp.zeros((LORA_COLS, 4 * RWKV_WIDTH), F32)
    r0 = 0
    for i, s in enumerate(seconds):
        w2 = w2.at[r0:r0 + s.shape[0], i * RWKV_WIDTH:(i + 1) * RWKV_WIDTH].set(s)
        r0 += s.shape[0]
    return w_big, w2.astype(BF16)


def _trunk(x, c, P, wkv0, h_prev, k_buf, v_buf):
    B, S, _ = x.shape
    T = B * S
    depth = P["w_ada"].shape[0]
    n_pairs = ATT_WIDTH // LANES
    unslab = lambda t: jnp.transpose(t.reshape(n_pairs, B, -1, LANES), (1, 2, 0, 3)).reshape(B, -1, ATT_WIDTH)
    cs = jax.nn.silu(c)
    x = x.reshape(T, D_MODEL)
    new_k, new_v, new_wkv, new_shift = [], [], [], []
    v_first = None
    for l in range(depth):
        mod = _mm_rows(cs, P["w_ada"][l].astype(BF16), tn=1024) + P["b_ada"][l]
        sh1, sc1, gt1, sh2, sc2, gt2 = jnp.split(mod, 6, axis=-1)
        w_big, w_lora2 = _layer_weights(l, P)
        qkv, rest = _pre_proj(x, sc1, sh1, w_big, S)
        rest_prev = _mm_rows(h_prev[l], w_big[:, 3 * ATT_WIDTH:])
        x_last = x.reshape(B, S, D_MODEL)[:, -1]
        new_shift.append(_rms_mod(x_last, sc1, sh1))

        if k_buf is None:
            att = _att_prompt(qkv, B, P["att_out_g"][l])
            keep = min(MAX_WINDOW, S)
            kv = qkv.reshape(3, n_pairs, B, S, LANES)[1:, :, :, S - keep:]
            k_rows, v_rows = (unslab(kv[i]).reshape(B, keep, N_ATT_HEADS, HEAD_DIM) for i in range(2))
        else:
            assert S == 1
            n_buf = k_buf.shape[2]
            q, kn, vn = (unslab(qkv[i * n_pairs:(i + 1) * n_pairs]) for i in range(3))
            att = _att_decode(q, kn, vn, k_buf[l].reshape(B, n_buf, ATT_WIDTH), v_buf[l].reshape(B, n_buf, ATT_WIDTH),
                              P["att_out_g"][l])
            att = jnp.transpose(att.reshape(B, n_pairs, LANES), (1, 0, 2))
            k_rows = kn.reshape(B, S, N_ATT_HEADS, HEAD_DIM)
            v_rows = vn.reshape(B, S, N_ATT_HEADS, HEAD_DIM)

        zero = jnp.zeros((RWKV_WIDTH,), F32)
        vecs = jnp.stack([P["decay_w0"][l], P["iclr_a0"][l], P["vres_v0"][l - 1] if l > 0 else zero,
                          P["k_k"][l], P["k_a"][l], zero, zero, zero])
        r, lw, k, v, a, b, g = _rwkv_prep(rest, rest_prev, P["mu_rkv"][l].reshape(1, N_RKV), vecs, w_lora2,
                                          v_first, S)
        if l == 0:
            v_first = v
        Sp = -(-S // WKV_CHUNK) * WKV_CHUNK
        seq = lambda t: jnp.pad(t.reshape(B, S, RWKV_WIDTH), ((0, 0), (0, Sp - S), (0, 0)))
        y, wkv = _wkv(seq(r), seq(lw), seq(k), seq(v), seq(a), seq(b), wkv0[l])
        y = y[:, :S].reshape(T, RWKV_WIDTH)

        vecs_out = jnp.stack([P["lnx_w"][l], P["lnx_b"][l], P["r_k"][l].reshape(RWKV_WIDTH)] + [zero] * 5)
        rw = jnp.pad(P["router_w"][l], ((0, 0), (0, LANES - N_EXPERTS)))
        rb = jnp.pad(P["router_b"][l], (0, LANES - N_EXPERTS)).reshape(1, LANES)
        x1, h2, logits = _out_proj(att, y, r, k, v, g, x, gt1, sc2, sh2, vecs_out, P["w_out"][l].astype(BF16),
                                   rw, rb, S)

        weights, gather_idx, scatter_idx, group_e, G = _route(logits[:, :N_EXPERTS], T)
        moe_out = _moe_experts(h2, gather_idx, scatter_idx, group_e, P["moe_w1"][l], P["moe_b1"][l],
                               P["moe_w2"][l], P["moe_b2"][l], G=G)
        x = _combine(moe_out, weights, x1, gt2, P["final_g"] if l == depth - 1 else None, S)

        new_k.append(k_rows)
        new_v.append(v_rows)
        new_wkv.append(wkv)
    return x.reshape(B, S, D_MODEL), jnp.stack(new_k), jnp.stack(new_v), jnp.stack(new_wkv), jnp.stack(new_shift)


def kernel(x_prompt, x_sample, c_prompt, c_sample, state_attn_k, state_attn_v, state_wkv, state_shift, w_ada, b_ada, w_in, att_out_g, mu_rkv, mu_wag, decay_w0, decay_w1, decay_w2, iclr_a0, iclr_a1, iclr_a2, gate_g1, gate_g2, vres_mu, vres_v0, vres_v1, vres_v2, k_k, k_a, r_k, lnx_w, lnx_b, w_out, router_w, router_b, moe_w1, moe_b1, moe_w2, moe_b2, final_g):
    P = dict(w_ada=w_ada, b_ada=b_ada, w_in=w_in, att_out_g=att_out_g, mu_rkv=mu_rkv, mu_wag=mu_wag,
             decay_w0=decay_w0, decay_w1=decay_w1, decay_w2=decay_w2, iclr_a0=iclr_a0, iclr_a1=iclr_a1,
             iclr_a2=iclr_a2, gate_g1=gate_g1, gate_g2=gate_g2, vres_mu=vres_mu, vres_v0=vres_v0,
             vres_v1=vres_v1, vres_v2=vres_v2, k_k=k_k, k_a=k_a, r_k=r_k, lnx_w=lnx_w, lnx_b=lnx_b,
             w_out=w_out, router_w=router_w, router_b=router_b, moe_w1=moe_w1, moe_b1=moe_b1,
             moe_w2=moe_w2, moe_b2=moe_b2, final_g=final_g)
    depth = w_ada.shape[0]
    B = x_prompt.shape[0]
    wkv0_prompt = jnp.zeros((depth, B, N_RWKV_HEADS, HEAD_DIM, HEAD_DIM), F32)
    shift0_prompt = jnp.zeros((depth, B, D_MODEL), x_prompt.dtype)
    y_p, k_p, v_p, wkv_p, shift_p = _trunk(x_prompt, c_prompt, P, wkv0_prompt, shift0_prompt, None, None)
    y_s, k_s, v_s, wkv_s, shift_s = _trunk(x_sample, c_sample, P, state_wkv, state_shift, state_attn_k, state_attn_v)
    return (y_p, y_s, k_p, v_p, wkv_p, shift_p, k_s, v_s, wkv_s, shift_s)
```

```python
import functools
import math

import jax
import jax.numpy as jnp
from jax import lax
from jax.experimental import pallas as pl
from jax.experimental.pallas import tpu as pltpu

F32 = jnp.float32
BF16 = jnp.bfloat16
HIGHEST = lax.Precision.HIGHEST

D_MODEL = 1024
HEAD_DIM = 64
N_ATT_HEADS = 8
N_RWKV_HEADS = 8
ATT_WIDTH = N_ATT_HEADS * HEAD_DIM
RWKV_WIDTH = N_RWKV_HEADS * HEAD_DIM
IN_COLS = 3 * ATT_WIDTH + 3 * RWKV_WIDTH
DILATIONS = (1, 4, 16)
ATT_BLOCK = 128
MAX_WINDOW = 2048
N_EXPERTS = 32
TOP_K = 4
D_EXPERT = 1024
SWIGLU_LIMIT = 7.0
SWIGLU_ALPHA = 1.702
NORM_EPS = 1e-5
LNX_EPS = 64e-5
NEG_BIG = -1e30
WKV_CHUNK = 64
LANES = 128
VMEM_LIMIT = 56 * 1024 * 1024
LORA_WIDTHS = (64, 64, 128, 32)
LORA_COLS = 384
N_RKV = 3 * RWKV_WIDTH
N_REST = N_RKV + 2 * LORA_COLS
ROW_TILE_SUBLANES = D_MODEL // LANES


def _params(n_grid):
    return pltpu.CompilerParams(dimension_semantics=("arbitrary",) * n_grid,
                                vmem_limit_bytes=VMEM_LIMIT)


def _dot(a, b, precision=None):
    return jnp.dot(a, b, preferred_element_type=F32, precision=precision)


def _dot_nt(a, b, precision=None):
    return lax.dot_general(a, b, (((1,), (1,)), ((), ())), preferred_element_type=F32, precision=precision)


def _dot_tn(a, b, precision=None):
    return lax.dot_general(a, b, (((0,), (0,)), ((), ())), preferred_element_type=F32, precision=precision)


def _split3(x):
    hi = x.astype(BF16)
    r1 = x - hi.astype(F32)
    mid = r1.astype(BF16)
    lo = (r1 - mid.astype(F32)).astype(BF16)
    return hi, mid, lo


def _head_sum_matrix():
    hr = lax.broadcasted_iota(jnp.int32, (LANES, LANES), 0) // HEAD_DIM
    hc = lax.broadcasted_iota(jnp.int32, (LANES, LANES), 1) // HEAD_DIM
    return (hr == hc).astype(F32)


def _rms_mod(x, sc, sh):
    return x * lax.rsqrt(jnp.mean(x * x, axis=-1, keepdims=True) + NORM_EPS) * (1.0 + sc) + sh


def _row_tile(T, S, tm):
    tm = min(tm, T)
    assert T % tm == 0 and (S == 1 or S % tm == 0)
    return tm


def _mod_operand(v, S, tm):
    B, N = v.shape
    if S == 1:
        return v.reshape(1, B, N), pl.BlockSpec((None, tm, N), lambda i: (0, i, 0))
    return v.reshape(B, 1, N), pl.BlockSpec((None, 1, N), lambda i: (i * tm // S, 0, 0))


def _mm_body(x_ref, w_ref, o_ref):
    o_ref[...] = _dot(x_ref[...].astype(BF16), w_ref[...])


def _mm_rows(x, w, *, tn=None):
    M, K = x.shape
    N = w.shape[1]
    Mp = -(-M // 8) * 8
    if Mp != M:
        x = jnp.pad(x, ((0, Mp - M), (0, 0)))
    tn = N if tn is None else tn
    assert N % tn == 0
    out = pl.pallas_call(
        _mm_body,
        grid=(N // tn,),
        in_specs=[pl.BlockSpec((Mp, K), lambda j: (0, 0)),
                  pl.BlockSpec((K, tn), lambda j: (0, j))],
        out_specs=pl.BlockSpec((Mp, tn), lambda j: (0, j)),
        out_shape=jax.ShapeDtypeStruct((Mp, N), F32),
        compiler_params=_params(1),
        name="mm_rows",
    )(x, w)
    return out[:M]


def _pre_proj_body(x_ref, sc_ref, sh_ref, w_ref, qkv_ref, kv_ref, rest_ref):
    h = _rms_mod(x_ref[...], sc_ref[...], sh_ref[...])
    acc = _dot(h.astype(BF16), w_ref[...])
    n_slabs = qkv_ref.shape[0]
    for i in range(n_slabs):
        qkv_ref[i] = acc[:, i * LANES:(i + 1) * LANES]
    kv_ref[...] = acc[:, ATT_WIDTH:3 * ATT_WIDTH]
    rest_ref[...] = acc[:, n_slabs * LANES:]


def _pre_proj(x, sc, sh, w, S, *, tm=256):
    T, K = x.shape
    N = w.shape[1]
    tm = _row_tile(T, S, tm)
    n_slabs = 3 * ATT_WIDTH // LANES
    n_rest = N - 3 * ATT_WIDTH
    sc_op, sc_spec = _mod_operand(sc, S, tm)
    sh_op, sh_spec = _mod_operand(sh, S, tm)
    return pl.pallas_call(
        _pre_proj_body,
        grid=(T // tm,),
        in_specs=[pl.BlockSpec((tm, K), lambda i: (i, 0)), sc_spec, sh_spec,
                  pl.BlockSpec((K, N), lambda i: (0, 0))],
        out_specs=[pl.BlockSpec((n_slabs, tm, LANES), lambda i: (0, i, 0)),
                   pl.BlockSpec((tm, 2 * ATT_WIDTH), lambda i: (i, 0)),
                   pl.BlockSpec((tm, n_rest), lambda i: (i, 0))],
        out_shape=[jax.ShapeDtypeStruct((n_slabs, T, LANES), F32), jax.ShapeDtypeStruct((T, 2 * ATT_WIDTH), F32),
                   jax.ShapeDtypeStruct((T, n_rest), F32)],
        compiler_params=_params(1),
        name="pre_proj",
    )(x, sc_op, sh_op, w)


def _att_prompt_body(qkv_ref, g_ref, o_ref, m_ref, l_ref, *, seq):
    scale = HEAD_DIM ** -0.5
    qi = lax.broadcasted_iota(jnp.int32, (ATT_BLOCK, ATT_BLOCK), 0)
    kj = lax.broadcasted_iota(jnp.int32, (ATT_BLOCK, ATT_BLOCK), 1)
    mask_cur = kj <= qi
    lane = lax.broadcasted_iota(jnp.int32, (ATT_BLOCK, LANES), 1)
    first_head = lane < HEAD_DIM
    n_pairs = ATT_WIDTH // LANES

    for branch, dil in enumerate(DILATIONS):
        nb = seq // (ATT_BLOCK * dil)

        def block(bi, carry, dil=dil, nb=nb, branch=branch):
            r = bi // nb
            j = bi % nb
            jp = jnp.maximum(j - 1, 0)
            if dil > 1:
                rows = pl.ds(r + dil * ATT_BLOCK * j, ATT_BLOCK, stride=dil)
                prows = pl.ds(r + dil * ATT_BLOCK * jp, ATT_BLOCK, stride=dil)
            else:
                rows = pl.ds(pl.multiple_of(ATT_BLOCK * j, ATT_BLOCK), ATT_BLOCK)
                prows = pl.ds(pl.multiple_of(ATT_BLOCK * jp, ATT_BLOCK), ATT_BLOCK)
            mp = kj >= qi + jnp.where(j > 0, 0, ATT_BLOCK)
            if branch > 0:
                m_old = m_ref[rows, :]
                l_old = l_ref[rows, :]
            heads = [(p, hh) for p in range(n_pairs) for hh in range(2)]
            sel = lambda hh: first_head if hh == 0 else jnp.logical_not(first_head)
            k1 = [qkv_ref[n_pairs + p, prows, :].astype(BF16) for p in range(n_pairs)]
            k2 = [qkv_ref[n_pairs + p, rows, :].astype(BF16) for p in range(n_pairs)]
            qh = []
            for p in range(n_pairs):
                q = qkv_ref[p, rows, :] * scale
                qh += [jnp.where(sel(hh), q, 0.0).astype(BF16) for hh in range(2)]
            s1 = [_dot_nt(qh[i], k1[p]) for i, (p, hh) in enumerate(heads)]
            s2 = [_dot_nt(qh[i], k2[p]) for i, (p, hh) in enumerate(heads)]
            p1, p2, mbs = [], [], []
            for i in range(len(heads)):
                a1 = jnp.where(mp, s1[i], NEG_BIG)
                a2 = jnp.where(mask_cur, s2[i], NEG_BIG)
                mb = jnp.max(jnp.maximum(a1, a2), axis=1, keepdims=True)
                p1.append(jnp.exp(a1 - mb).astype(BF16))
                p2.append(jnp.exp(a2 - mb).astype(BF16))
                mbs.append(mb)
            ones = jnp.ones((ATT_BLOCK, LANES), BF16)
            obs, lbs = [], []
            for p in range(n_pairs):
                v1 = qkv_ref[2 * n_pairs + p, prows, :]
                v2 = qkv_ref[2 * n_pairs + p, rows, :]
                for hh in range(2):
                    i = 2 * p + hh
                    v1h = jnp.where(sel(hh), v1, 0.0).astype(BF16)
                    v2h = jnp.where(sel(hh), v2, 0.0).astype(BF16)
                    obs.append(_dot(p1[i], v1h) + _dot(p2[i], v2h))
                    lbs.append(_dot(p1[i], ones) + _dot(p2[i], ones))
            m_new = jnp.zeros((ATT_BLOCK, LANES), F32)
            l_new = jnp.zeros((ATT_BLOCK, LANES), F32)
            for p in range(n_pairs):
                out = jnp.zeros((ATT_BLOCK, LANES), F32)
                alpha = jnp.zeros((ATT_BLOCK, LANES), F32)
                for hh in range(2):
                    h = 2 * p + hh
                    mb, lb, ob = mbs[h], lbs[h], obs[h]
                    if branch > 0:
                        mo = m_old[:, h:h + 1]
                        lo = l_old[:, h:h + 1]
                        mn = jnp.maximum(mo, mb)
                        a_old = jnp.exp(mo - mn)
                        a_new = jnp.exp(mb - mn)
                        lb = a_old * lo + a_new * lb
                        ob = ob * a_new
                        alpha = jnp.where(sel(hh), a_old, alpha)
                        mb = mn
                    out = out + ob
                    m_new = jnp.where(lane == h, mb, m_new)
                    l_new = jnp.where(lane == h, lb, l_new)
                if branch > 0:
                    out = out + alpha * o_ref[p, rows, :]
                o_ref[p, rows, :] = out
            m_ref[rows, :] = m_new
            l_ref[rows, :] = l_new
            return carry

        lax.fori_loop(0, seq // ATT_BLOCK, block, 0)

    head_sum = _head_sum_matrix()

    def finish(bi, carry):
        rows = pl.ds(pl.multiple_of(bi * ATT_BLOCK, ATT_BLOCK), ATT_BLOCK)
        l_all = l_ref[rows, :]
        for p in range(n_pairs):
            den = jnp.where(first_head, l_all[:, 2 * p:2 * p + 1], l_all[:, 2 * p + 1:2 * p + 2])
            att = o_ref[p, rows, :] / den
            ms = _dot(att * att, head_sum, HIGHEST) * (1.0 / HEAD_DIM)
            o_ref[p, rows, :] = att * lax.rsqrt(ms + NORM_EPS) * g_ref[:, p * LANES:(p + 1) * LANES]
        return carry

    lax.fori_loop(0, seq // ATT_BLOCK, finish, 0)


def _att_prompt(qkv, B, gain):
    n3, T, _ = qkv.shape
    S = T // B
    n_pairs = n3 // 3
    assert S % (ATT_BLOCK * DILATIONS[-1]) == 0
    return pl.pallas_call(
        functools.partial(_att_prompt_body, seq=S),
        grid=(B,),
        in_specs=[pl.BlockSpec((n3, S, LANES), lambda b: (0, b, 0), pipeline_mode=pl.Buffered(1)),
                  pl.BlockSpec((1, ATT_WIDTH), lambda b: (0, 0))],
        out_specs=pl.BlockSpec((n_pairs, S, LANES), lambda b: (0, b, 0)),
        out_shape=jax.ShapeDtypeStruct((n_pairs, T, LANES), F32),
        scratch_shapes=[pltpu.VMEM((S, LANES), F32), pltpu.VMEM((S, LANES), F32)],
        compiler_params=_params(1),
        name="att_prompt",
    )(qkv, gain.reshape(1, ATT_WIDTH))


def _att_decode_body(q_ref, kn_ref, vn_ref, *refs):
    nd = len(DILATIONS)
    kb_refs, vb_refs, (g_ref, o_ref) = refs[:nd], refs[nd:2 * nd], refs[2 * nd:]
    scale = HEAD_DIM ** -0.5
    seg = (lax.broadcasted_iota(jnp.int32, (ATT_WIDTH, N_ATT_HEADS), 0) // HEAD_DIM
           == lax.broadcasted_iota(jnp.int32, (ATT_WIDTH, N_ATT_HEADS), 1)).astype(F32)
    seg_t = (lax.broadcasted_iota(jnp.int32, (N_ATT_HEADS, ATT_WIDTH), 1) // HEAD_DIM
             == lax.broadcasted_iota(jnp.int32, (N_ATT_HEADS, ATT_WIDTH), 0)).astype(F32)
    rows8 = lambda t: jnp.broadcast_to(t, (8, t.shape[-1]))
    q = rows8(q_ref[...]) * scale
    kn = rows8(kn_ref[...])
    vn = rows8(vn_ref[...])
    s_new = _dot(q * kn, seg, HIGHEST)
    parts = []
    for kb_ref, vb_ref in zip(kb_refs, vb_refs):
        kb = kb_ref[...]
        vb = vb_ref[...]
        s = _dot(kb * q[0:1], seg, HIGHEST)
        m = jnp.maximum(jnp.max(s, axis=0, keepdims=True), s_new)
        p = jnp.exp(s - m[0:1])
        p_new = jnp.exp(s_new - m)
        den = jnp.sum(p, axis=0, keepdims=True) + p_new
        pe = _dot(p, seg_t, HIGHEST)
        num = jnp.sum(pe * vb, axis=0, keepdims=True) + _dot(p_new, seg_t, HIGHEST) * vn
        parts.append((num, den, m))
    m_all = jnp.maximum(jnp.maximum(parts[0][2], parts[1][2]), parts[2][2])
    num = jnp.zeros((8, ATT_WIDTH), F32)
    den = jnp.zeros((8, N_ATT_HEADS), F32)
    for n_b, d_b, m_b in parts:
        w = jnp.exp(m_b - m_all)
        num = num + n_b * _dot(w, seg_t, HIGHEST)
        den = den + d_b * w
    att = num / _dot(den, seg_t, HIGHEST)
    ms = _dot(_dot(att * att, seg, HIGHEST) * (1.0 / HEAD_DIM), seg_t, HIGHEST)
    o_ref[...] = (att * lax.rsqrt(ms + NORM_EPS) * g_ref[...])[0:1]


def _att_decode(q, k_new, v_new, k_buf, v_buf, layer, gain):
    depth, B, n_buf, W = k_buf.shape
    row = pl.BlockSpec((None, 1, W), lambda b: (b, 0, 0))
    views, specs = [], []
    for buf in (k_buf, v_buf):
        for dil in DILATIONS:
            assert n_buf % (ATT_BLOCK * dil) == 0
            views.append(buf.reshape(depth, B, n_buf // dil, dil * W))
            specs.append(pl.BlockSpec((None, None, ATT_BLOCK, W),
                                      lambda b, dil=dil: (layer, b, n_buf // (dil * ATT_BLOCK) - 1, 0)))
    return pl.pallas_call(
        _att_decode_body,
        grid=(B,),
        in_specs=[row, row, row] + specs + [pl.BlockSpec((1, W), lambda b: (0, 0))],
        out_specs=row,
        out_shape=jax.ShapeDtypeStruct((B, 1, W), F32),
        compiler_params=_params(1),
        name="att_decode",
    )(q, k_new, v_new, *views, gain.reshape(1, W))


def _rwkv_prep_body(*refs, S, tm, has_vfirst):
    if has_vfirst:
        cur_ref, prev_ref, mu_ref, vec_ref, w2_ref, vf_ref = refs[:6]
        rest = refs[6:]
    else:
        cur_ref, prev_ref, mu_ref, vec_ref, w2_ref = refs[:5]
        vf_ref, rest = None, refs[5:]
    r_ref, lw_ref, k_ref, v_ref, a_ref, b_ref, g_ref, carry = rest
    c_lp = N_RKV + LORA_COLS
    rkv = cur_ref[:, :N_RKV]
    lp = cur_ref[:, c_lp:]
    if S == 1:
        rkv_prev = prev_ref[:, :N_RKV]
        lp_prev = prev_ref[:, c_lp:]
    else:
        i = pl.program_id(0)

        @pl.when(i == 0)
        def _():
            carry[...] = jnp.zeros_like(carry)

        first = (i % (S // tm)) == 0
        prev_row = jnp.where(first, prev_ref[...], carry[...])
        row0 = lax.broadcasted_iota(jnp.int32, (tm, 1), 0) == 0
        rkv_prev = jnp.where(row0, prev_row[:, :N_RKV], pltpu.roll(rkv, 1, 0))
        lp_prev = jnp.where(row0, prev_row[:, c_lp:], pltpu.roll(lp, 1, 0))
        carry[...] = cur_ref[tm - 1:tm, :]
    rkv = rkv + (rkv_prev - rkv) * mu_ref[...]
    r = rkv[:, :RWKV_WIDTH]
    kr = rkv[:, RWKV_WIDTH:2 * RWKV_WIDTH]
    vr = rkv[:, 2 * RWKV_WIDTH:]
    l1 = cur_ref[:, N_RKV:c_lp] + lp_prev
    t0 = l1[:, :LANES]
    lane = lax.broadcasted_iota(jnp.int32, (tm, LANES), 1)
    act = jnp.concatenate([jnp.where(lane < LORA_WIDTHS[0], jnp.tanh(t0), t0),
                           jax.nn.sigmoid(l1[:, LANES:2 * LANES]),
                           l1[:, 2 * LANES:]], axis=1)
    l2 = _dot(act.astype(BF16), w2_ref[...])
    w0, a0, v0, k_k, k_a = (vec_ref[j:j + 1, :] for j in range(5))
    z = w0 + l2[:, :RWKV_WIDTH]
    softplus_neg = jnp.maximum(-z, 0.0) + jnp.log(1.0 + jnp.exp(-jnp.abs(z)))
    lw_ref[...] = -jnp.exp(-softplus_neg - 0.5)
    a = jax.nn.sigmoid(a0 + l2[:, RWKV_WIDTH:2 * RWKV_WIDTH])
    g_ref[...] = l2[:, 2 * RWKV_WIDTH:3 * RWKV_WIDTH]
    if has_vfirst:
        vr = vr + (vf_ref[...] - vr) * jax.nn.sigmoid(v0 + l2[:, 3 * RWKV_WIDTH:])
    kk = kr * k_k
    head_sum = _head_sum_matrix()
    ss = jnp.concatenate([_dot(kk[:, p * LANES:(p + 1) * LANES] ** 2, head_sum, HIGHEST)
                          for p in range(RWKV_WIDTH // LANES)], axis=1)
    kk = kk / jnp.maximum(jnp.sqrt(ss), 1e-12)
    r_ref[...] = r
    k_ref[...] = kr * (1.0 + (a - 1.0) * k_a)
    v_ref[...] = vr
    a_ref[...] = -kk
    b_ref[...] = kk * a


def _rwkv_prep(rest, rest_prev, mu_rkv, vecs, w2, v_first, S, *, tm=256):
    T = rest.shape[0]
    tm = _row_tile(T, S, tm)
    prev_op, prev_spec = _mod_operand(rest_prev, S, tm)
    tile = pl.BlockSpec((tm, RWKV_WIDTH), lambda i: (i, 0))
    const = lambda a: pl.BlockSpec(a.shape, lambda i: (0,) * a.ndim)
    ops = [rest, prev_op, mu_rkv, vecs, w2]
    specs = [pl.BlockSpec((tm, N_REST), lambda i: (i, 0)), prev_spec, const(mu_rkv), const(vecs), const(w2)]
    if v_first is not None:
        ops.append(v_first)
        specs.append(tile)
    return pl.pallas_call(
        functools.partial(_rwkv_prep_body, S=S, tm=tm, has_vfirst=v_first is not None),
        grid=(T // tm,),
        in_specs=specs,
        out_specs=[tile] * 7,
        out_shape=[jax.ShapeDtypeStruct((T, RWKV_WIDTH), F32)] * 7,
        scratch_shapes=[pltpu.VMEM((1, N_REST), F32)],
        compiler_params=_params(1),
        name="rwkv_prep",
    )(*ops)


def _wkv_body(r_ref, lw_ref, k_ref, v_ref, a_ref, b_ref, s0_ref, y_ref, sn_ref, s_scr, *, n_chunks):
    C = WKV_CHUNK
    C2 = 2 * C
    n_pairs = RWKV_WIDTH // LANES
    t = pl.program_id(1)

    @pl.when(t == 0)
    def _():
        s_scr[...] = s0_ref[...]

    row = lax.broadcasted_iota(jnp.int32, (C2, C2), 0)
    col = lax.broadcasted_iota(jnp.int32, (C2, C2), 1)
    same_head = (row // C) == (col // C)
    lower_strict = same_head & (col < row)
    lower_incl = same_head & (col <= row)
    eye = (row == col).astype(F32)
    tri = (lax.broadcasted_iota(jnp.int32, (C, C), 1) <= lax.broadcasted_iota(jnp.int32, (C, C), 0)).astype(BF16)
    first_head = lax.broadcasted_iota(jnp.int32, (C, LANES), 1) < HEAD_DIM
    block_diag = _head_sum_matrix() > 0.5
    ones = jnp.ones((C, LANES), BF16)
    bf = lambda x: x.astype(BF16)

    def split(x):
        return jnp.concatenate([jnp.where(first_head, x, 0.0), jnp.where(first_head, 0.0, x)], axis=0)

    items = [(c, p) for c in range(n_chunks) for p in range(n_pairs)]
    ld = lambda ref, c, p: ref[c * C:(c + 1) * C, p * LANES:(p + 1) * LANES]
    each = lambda f, *ls: [f(*xs) for xs in zip(*ls)]
    r, lw, k, v, a, b = ([ld(ref, c, p) for c, p in items] for ref in (r_ref, lw_ref, k_ref, v_ref, a_ref, b_ref))

    lw3 = each(_split3, lw)
    cum = each(lambda t3: sum(_dot(tri, x) for x in t3), lw3)
    x_all = each(lambda a_, r_, lw_, c_: bf(jnp.concatenate([split(a_ * jnp.exp(c_ - lw_)), split(r_ * jnp.exp(c_))],
                                                            axis=0)), a, r, lw, cum)

    def keys(b_, k_, c_):
        g_inv = jnp.exp(-c_)
        bt, kt = b_ * g_inv, k_ * g_inv
        return bf(jnp.concatenate([bt, bt, kt, kt], axis=0))

    y_all = each(keys, b, k, cum)
    gram = each(_dot_nt, x_all, y_all)
    l_pow = each(lambda g_: jnp.where(lower_strict, g_[0:C2, 0:C2], 0.0), gram)
    m_ak = each(lambda g_: bf(jnp.where(lower_strict, g_[0:C2, C2:], 0.0)), gram)
    m_rb = each(lambda g_: bf(jnp.where(lower_incl, g_[C2:, 0:C2], 0.0)), gram)
    m_rk = each(lambda g_: bf(jnp.where(lower_incl, g_[C2:, C2:], 0.0)), gram)
    v_st = each(lambda v_: bf(split(v_)), v)
    t_inv = each(lambda l_: eye + l_, l_pow)
    for _ in range(int(math.log2(C)) - 1):
        l_pow = each(lambda l_: _dot(bf(l_), bf(l_)), l_pow)
        t_inv = each(lambda t_, l_: t_ + _dot(bf(t_), bf(l_)), t_inv, l_pow)
    t_inv = each(bf, t_inv)
    akv = each(_dot, m_ak, v_st)
    rkv = each(_dot, m_rk, v_st)
    tail = each(lambda c_: jnp.exp(c_[C - 1:C, :] - c_), cum)
    b_end = each(lambda b_, t_: bf(b_ * t_), b, tail)
    kv_end = each(lambda k_, t_, v_: _dot_tn(bf(k_ * t_), bf(v_)), k, tail, v)
    g_col = each(lambda t3: jnp.exp(sum(_dot_tn(x, ones) for x in t3)), lw3)

    state = [s_scr[p] for p in range(n_pairs)]
    for c in range(n_chunks):
        sel = lambda ls: ls[c * n_pairs:(c + 1) * n_pairs]
        xs = each(lambda x_, s_: _dot(x_, bf(s_)), sel(x_all), state)
        u_st = each(lambda t_, xs_, akv_: _dot(t_, bf(xs_[0:C2] + akv_)), sel(t_inv), xs, sel(akv))
        y_st = each(lambda xs_, m_, u_, rkv_: xs_[C2:] + _dot(m_, bf(u_)) + rkv_, xs, sel(m_rb), u_st, sel(rkv))
        upd = each(lambda b_, u_: _dot_tn(b_, bf(u_[0:C] + u_[C:])), sel(b_end), u_st)
        for p in range(n_pairs):
            y_ref[c * C:(c + 1) * C, p * LANES:(p + 1) * LANES] = y_st[p][0:C] + y_st[p][C:]
        state = each(lambda g_, s_, u_, kv_: g_ * s_ + jnp.where(block_diag, u_ + kv_, 0.0),
                     sel(g_col), state, upd, sel(kv_end))
    for p in range(n_pairs):
        s_scr[p] = state[p]

    @pl.when(t == pl.num_programs(1) - 1)
    def _():
        sn_ref[...] = s_scr[...]


def _wkv(r, lw, k, v, a, b, s0, *, chunks_per_step=4):
    B, T, W = r.shape
    H = N_RWKV_HEADS
    n_pairs = W // LANES
    tt = min(T, WKV_CHUNK * chunks_per_step)
    assert T % tt == 0 and tt % WKV_CHUNK == 0
    s0t = jnp.swapaxes(s0, -1, -2).reshape(B, n_pairs, 2, HEAD_DIM, HEAD_DIM)
    z = jnp.zeros_like(s0t[:, :, 0])
    s0bd = jnp.concatenate([jnp.concatenate([s0t[:, :, 0], z], axis=-1),
                            jnp.concatenate([z, s0t[:, :, 1]], axis=-1)], axis=-2)
    seq = pl.BlockSpec((None, tt, W), lambda bi, ti: (bi, ti, 0))
    st = pl.BlockSpec((None, n_pairs, LANES, LANES), lambda bi, ti: (bi, 0, 0, 0))
    y, sbd = pl.pallas_call(
        functools.partial(_wkv_body, n_chunks=tt // WKV_CHUNK),
        grid=(B, T // tt),
        in_specs=[seq] * 6 + [st],
        out_specs=[seq, st],
        out_shape=[jax.ShapeDtypeStruct((B, T, W), F32), jax.ShapeDtypeStruct((B, n_pairs, LANES, LANES), F32)],
        scratch_shapes=[pltpu.VMEM((n_pairs, LANES, LANES), F32)],
        compiler_params=_params(2),
        name="wkv",
    )(r, lw, k, v, a, b, s0bd)
    sn = jnp.stack([sbd[:, :, :HEAD_DIM, :HEAD_DIM], sbd[:, :, HEAD_DIM:, HEAD_DIM:]], axis=2)
    return y, jnp.swapaxes(sn.reshape(B, H, HEAD_DIM, HEAD_DIM), -1, -2)


def _out_proj_body(att_ref, y_ref, r_ref, k_ref, v_ref, g_ref, x_ref, gt_ref, sc_ref, sh_ref,
                   vec_ref, wo_ref, rw_ref, rb_ref, x1_ref, h2_ref, logit_ref):
    head_sum = _head_sum_matrix()
    n_pairs = RWKV_WIDTH // LANES
    parts = [att_ref[p].astype(BF16) for p in range(ATT_WIDTH // LANES)]
    for p in range(n_pairs):
        cols = slice(p * LANES, (p + 1) * LANES)
        y = y_ref[:, cols]
        mean = _dot(y, head_sum, HIGHEST) * (1.0 / HEAD_DIM)
        d = y - mean
        var = _dot(d * d, head_sum, HIGHEST) * (1.0 / HEAD_DIM)
        yn = d * lax.rsqrt(var + LNX_EPS) * vec_ref[0:1, cols] + vec_ref[1:2, cols]
        rk = r_ref[:, cols] * k_ref[:, cols] * vec_ref[2:3, cols]
        bonus = _dot(rk, head_sum, HIGHEST) * v_ref[:, cols]
        parts.append(((yn + bonus) * g_ref[:, cols]).astype(BF16))
    mix = _dot(jnp.concatenate(parts, axis=1), wo_ref[...])
    x1 = x_ref[...] + gt_ref[...] * mix
    x1_ref[...] = x1
    h2 = _rms_mod(x1, sc_ref[...], sh_ref[...])
    h2_ref[...] = h2
    logit_ref[...] = _dot(h2, rw_ref[...], HIGHEST) + rb_ref[...]


def _out_proj(att, y, r, k, v, g, x, gt, sc, sh, vecs, w_out, rw, rb, S, *, tm=256):
    T = x.shape[0]
    tm = _row_tile(T, S, tm)
    n_pairs = att.shape[0]
    tile = lambda n: pl.BlockSpec((tm, n), lambda i: (i, 0))
    const = lambda a: pl.BlockSpec(a.shape, lambda i: (0,) * a.ndim)
    mods = [_mod_operand(m, S, tm) for m in (gt, sc, sh)]
    return pl.pallas_call(
        _out_proj_body,
        grid=(T // tm,),
        in_specs=[pl.BlockSpec((n_pairs, tm, LANES), lambda i: (0, i, 0))] + [tile(RWKV_WIDTH)] * 5 + [tile(D_MODEL)]
                 + [m[1] for m in mods] + [const(vecs), const(w_out), const(rw), const(rb)],
        out_specs=[tile(D_MODEL), tile(D_MODEL), tile(LANES)],
        out_shape=[jax.ShapeDtypeStruct((T, D_MODEL), F32), jax.ShapeDtypeStruct((T, D_MODEL), F32),
                   jax.ShapeDtypeStruct((T, LANES), F32)],
        compiler_params=_params(1),
        name="out_proj",
    )(att, y, r, k, v, g, x, *[m[0] for m in mods], vecs, w_out, rw, rb)


def _moe_body(ge_ref, gidx_ref, sidx_ref, x_hbm, w1_ref, b1_ref, w2_ref, b2_ref, out_hbm,
              xb0, xb1, ob0, ob1, w1b, w2b, gsem, ssem, *, G, ng):
    s = pl.program_id(0)

    def step(cur, x_cur, x_oth, o_cur, o_oth):
        oth = 1 - cur

        tile_rows = lambda i: pl.ds(i * ROW_TILE_SUBLANES, ROW_TILE_SUBLANES)

        def start_gather():
            for i in range(G):
                pltpu.make_async_copy(x_hbm.at[gidx_ref[0, 0, i]], x_cur.at[tile_rows(i)],
                                      gsem.at[cur]).start(priority=i % 2)

        def start_scatter():
            for i in range(G):
                pltpu.make_async_copy(o_cur.at[tile_rows(i)], out_hbm.at[sidx_ref[0, 0, i]],
                                      ssem.at[cur]).start(priority=i % 2)

        def wait_rows(buf, sem):
            pltpu.make_async_copy(buf, buf, sem).wait()

        def evaluate():
            col = lambda j: pl.ds(j, G, stride=ROW_TILE_SUBLANES)
            x = jnp.concatenate([x_oth[col(j), :] for j in range(ROW_TILE_SUBLANES)], axis=1).astype(BF16)
            u = _dot(x, w1b[...]) + b1_ref[...]
            glu = jnp.minimum(u[:, :D_EXPERT], SWIGLU_LIMIT)
            lin = jnp.clip(u[:, D_EXPERT:], -SWIGLU_LIMIT, SWIGLU_LIMIT)
            act = glu * jax.nn.sigmoid(SWIGLU_ALPHA * glu) * (lin + 1.0)
            y = _dot(act.astype(BF16), w2b[...]) + b2_ref[...]
            for j in range(ROW_TILE_SUBLANES):
                o_oth[col(j), :] = y[:, j * LANES:(j + 1) * LANES]

        @pl.when((s >= 1) & (s <= ng))
        def _():
            wait_rows(x_oth, gsem.at[oth])
            e = jnp.clip(s - 1, 0, ng - 1)

            @pl.when((s == 1) | (ge_ref[e] != ge_ref[jnp.maximum(e - 1, 0)]))
            def _():
                w1b[...] = w1_ref[...].astype(BF16)
                w2b[...] = w2_ref[...].astype(BF16)

        @pl.when(s >= 3)
        def _():
            wait_rows(o_oth, ssem.at[oth])

        @pl.when(s == 0)
        def _():
            start_gather()

        @pl.when(s == 1)
        def _():
            start_gather()
            evaluate()

        @pl.when((s >= 2) & (s < ng))
        def _():
            start_gather()
            start_scatter()
            evaluate()

        @pl.when(s == ng)
        def _():
            start_scatter()
            evaluate()

        @pl.when(s == ng + 1)
        def _():
            start_scatter()
            wait_rows(o_cur, ssem.at[cur])

    @pl.when(s % 2 == 0)
    def _():
        step(0, xb0, xb1, ob0, ob1)

    @pl.when(s % 2 == 1)
    def _():
        step(1, xb1, xb0, ob1, ob0)


def _moe_experts(x, gather_idx, scatter_idx, group_e, w1, b1, w2, b2, layer, *, G):
    T = x.shape[0]
    ng = group_e.shape[0]
    assert ng >= 2
    depth = w1.shape[0]
    gidx = gather_idx.reshape(ng, 1, G)
    sidx = scatter_idx.reshape(ng, 1, G)
    smem_blk = lambda f: pl.BlockSpec((1, 1, G), f, memory_space=pltpu.SMEM)
    expert = lambda s, ge: (layer, ge[jnp.clip(s - 1, 0, ng - 1)], 0, 0)
    row_tile = (G * ROW_TILE_SUBLANES, LANES)
    grid_spec = pltpu.PrefetchScalarGridSpec(
        num_scalar_prefetch=1,
        grid=(ng + 2,),
        in_specs=[
            smem_blk(lambda s, ge: (jnp.minimum(s, ng - 1), 0, 0)),
            smem_blk(lambda s, ge: (jnp.clip(s - 2, 0, ng - 1), 0, 0)),
            pl.BlockSpec(memory_space=pl.ANY),
            pl.BlockSpec((None, None, D_MODEL, 2 * D_EXPERT), expert),
            pl.BlockSpec((None, None, 1, 2 * D_EXPERT), expert),
            pl.BlockSpec((None, None, D_EXPERT, D_MODEL), expert),
            pl.BlockSpec((None, None, 1, D_MODEL), expert),
        ],
        out_specs=pl.BlockSpec(memory_space=pl.ANY),
        scratch_shapes=[
            pltpu.VMEM(row_tile, F32),
            pltpu.VMEM(row_tile, F32),
            pltpu.VMEM(row_tile, F32),
            pltpu.VMEM(row_tile, F32),
            pltpu.VMEM((D_MODEL, 2 * D_EXPERT), BF16),
            pltpu.VMEM((D_EXPERT, D_MODEL), BF16),
            pltpu.SemaphoreType.DMA((2,)),
            pltpu.SemaphoreType.DMA((2,)),
        ],
    )
    out = pl.pallas_call(
        functools.partial(_moe_body, G=G, ng=ng),
        grid_spec=grid_spec,
        out_shape=jax.ShapeDtypeStruct((T * TOP_K + G, ROW_TILE_SUBLANES, LANES), F32),
        compiler_params=_params(1),
        name="moe_experts",
    )(group_e, gidx, sidx, x.reshape(T, ROW_TILE_SUBLANES, LANES), w1, b1.reshape(depth, N_EXPERTS, 1, -1),
      w2, b2.reshape(depth, N_EXPERTS, 1, -1))
    return out.reshape(T * TOP_K + G, D_MODEL)


def _combine_body(*refs, final):
    outs = refs[:TOP_K]
    w_ref, x_ref, gt_ref = refs[TOP_K:TOP_K + 3]
    w = w_ref[...]
    acc = outs[0][...] * w[:, 0:1]
    for kk in range(1, TOP_K):
        acc = acc + outs[kk][...] * w[:, kk:kk + 1]
    x2 = x_ref[...] + gt_ref[...] * acc
    if final:
        fg_ref, o_ref = refs[TOP_K + 3:]
        o_ref[...] = x2 * lax.rsqrt(jnp.mean(x2 * x2, axis=-1, keepdims=True) + NORM_EPS) * fg_ref[...]
    else:
        refs[TOP_K + 3][...] = x2


def _combine(moe_out, weights, x, gt, final_g, S, *, tm=512):
    T = x.shape[0]
    tm = _row_tile(T, S, tm)
    gt_op, gt_spec = _mod_operand(gt, S, tm)
    tile = pl.BlockSpec((tm, D_MODEL), lambda i: (i, 0))
    ops = [moe_out] * TOP_K + [weights, x, gt_op]
    specs = [pl.BlockSpec((tm, D_MODEL), lambda i, kk=kk: (kk * (T // tm) + i, 0)) for kk in range(TOP_K)]
    specs += [pl.BlockSpec((tm, TOP_K), lambda i: (i, 0)), tile, gt_spec]
    if final_g is not None:
        ops.append(final_g.reshape(1, D_MODEL))
        specs.append(pl.BlockSpec((1, D_MODEL), lambda i: (0, 0)))
    return pl.pallas_call(
        functools.partial(_combine_body, final=final_g is not None),
        grid=(T // tm,),
        in_specs=specs,
        out_specs=tile,
        out_shape=jax.ShapeDtypeStruct((T, D_MODEL), F32),
        compiler_params=_params(1),
        name="combine",
    )(*ops)


def _route(logits, T):
    top_val, top_idx = lax.top_k(logits, TOP_K)
    weights = jax.nn.softmax(top_val, axis=-1)
    A = T * TOP_K
    G = max(8, min(256, A // N_EXPERTS))
    ng = -(-A // G) + N_EXPERTS
    flat_e = top_idx.reshape(A).astype(jnp.int32)
    order = jnp.argsort(flat_e).astype(jnp.int32)
    experts = jnp.arange(N_EXPERTS, dtype=jnp.int32)
    counts = jnp.sum(flat_e[:, None] == experts[None, :], axis=0, dtype=jnp.int32)
    padded = (counts + G - 1) // G * G
    pad_end = jnp.cumsum(padded)
    pad_start = pad_end - padded
    start = jnp.cumsum(counts) - counts
    g0 = jnp.arange(ng, dtype=jnp.int32) * G
    group_e = jnp.minimum(jnp.sum(pad_end[None, :] <= g0[:, None], axis=1, dtype=jnp.int32), N_EXPERTS - 1)
    slot = jnp.arange(ng * G, dtype=jnp.int32)
    e = jnp.repeat(group_e, G)
    q = slot - pad_start[e]
    valid = q < counts[e]
    asg = order[jnp.clip(start[e] + q, 0, A - 1)]
    gather_idx = jnp.where(valid, asg // TOP_K, 0)
    scatter_idx = jnp.where(valid, (asg % TOP_K) * T + asg // TOP_K, A + slot % G)
    return weights, gather_idx, scatter_idx, group_e, G


def _layer_weights(l, P):
    mu_w, mu_a, mu_g = P["mu_wag"][l]
    firsts = [(mu_w, P["decay_w1"][l]), (mu_a, P["iclr_a1"][l]), (mu_g, P["gate_g1"][l])]
    seconds = [P["decay_w2"][l], P["iclr_a2"][l], P["gate_g2"][l]]
    if l > 0:
        firsts.append((P["vres_mu"][l - 1], P["vres_v1"][l - 1]))
        seconds.append(P["vres_v2"][l - 1])
    n_used = sum(w.shape[1] for _, w in firsts)
    zpad = jnp.zeros((D_MODEL, LORA_COLS - n_used), F32)
    cur = jnp.concatenate([(1.0 - mu)[:, None] * w for mu, w in firsts] + [zpad], axis=1)
    prev = jnp.concatenate([mu[:, None] * w for mu, w in firsts] + [zpad], axis=1)
    w_big = jnp.concatenate([P["w_in"][l], cur, prev], axis=1).astype(BF16)
    w2 = jnp.zeros((LORA_COLS, 4 * RWKV_WIDTH), F32)
    r0 = 0
    for i, s in enumerate(seconds):
        w2 = w2.at[r0:r0 + s.shape[0], i * RWKV_WIDTH:(i + 1) * RWKV_WIDTH].set(s)
        r0 += s.shape[0]
    return w_big, w2.astype(BF16)


def _trunk(x, c, P, wkv0, h_prev, k_buf, v_buf):
    B, S, _ = x.shape
    T = B * S
    depth = P["w_ada"].shape[0]
    n_pairs = ATT_WIDTH // LANES
    unslab = lambda t: jnp.transpose(t.reshape(n_pairs, B, -1, LANES), (1, 2, 0, 3)).reshape(B, -1, ATT_WIDTH)
    cs = jax.nn.silu(c)
    x = x.reshape(T, D_MODEL)
    new_k, new_v, new_wkv, new_shift = [], [], [], []
    v_first = None
    for l in range(depth):
        mod = _mm_rows(cs, P["w_ada"][l].astype(BF16), tn=1024) + P["b_ada"][l]
        sh1, sc1, gt1, sh2, sc2, gt2 = jnp.split(mod, 6, axis=-1)
        w_big, w_lora2 = _layer_weights(l, P)
        qkv, kv_rows, rest = _pre_proj(x, sc1, sh1, w_big, S)
        rest_prev = _mm_rows(h_prev[l], w_big[:, 3 * ATT_WIDTH:])
        x_last = x.reshape(B, S, D_MODEL)[:, -1]
        new_shift.append(_rms_mod(x_last, sc1, sh1))
        keep = min(MAX_WINDOW, S)
        kv_rows = kv_rows.reshape(B, S, 2, N_ATT_HEADS, HEAD_DIM)[:, S - keep:]
        k_rows, v_rows = kv_rows[:, :, 0], kv_rows[:, :, 1]

        if k_buf is None:
            att = _att_prompt(qkv, B, P["att_out_g"][l])
        else:
            assert S == 1
            depth_b, _, n_buf = k_buf.shape[:3]
            q = unslab(qkv[:n_pairs])
            att = _att_decode(q, k_rows.reshape(B, 1, ATT_WIDTH), v_rows.reshape(B, 1, ATT_WIDTH),
                              k_buf.reshape(depth_b, B, n_buf, ATT_WIDTH), v_buf.reshape(depth_b, B, n_buf, ATT_WIDTH),
                              l, P["att_out_g"][l])
            att = jnp.transpose(att.reshape(B, n_pairs, LANES), (1, 0, 2))

        zero = jnp.zeros((RWKV_WIDTH,), F32)
        vecs = jnp.stack([P["decay_w0"][l], P["iclr_a0"][l], P["vres_v0"][l - 1] if l > 0 else zero,
                          P["k_k"][l], P["k_a"][l], zero, zero, zero])
        r, lw, k, v, a, b, g = _rwkv_prep(rest, rest_prev, P["mu_rkv"][l].reshape(1, N_RKV), vecs, w_lora2,
                                          v_first, S)
        if l == 0:
            v_first = v
        Sp = -(-S // WKV_CHUNK) * WKV_CHUNK
        seq = lambda t: jnp.pad(t.reshape(B, S, RWKV_WIDTH), ((0, 0), (0, Sp - S), (0, 0)))
        y, wkv = _wkv(seq(r), seq(lw), seq(k), seq(v), seq(a), seq(b), wkv0[l])
        y = y[:, :S].reshape(T, RWKV_WIDTH)

        vecs_out = jnp.stack([P["lnx_w"][l], P["lnx_b"][l], P["r_k"][l].reshape(RWKV_WIDTH)] + [zero] * 5)
        rw = jnp.pad(P["router_w"][l], ((0, 0), (0, LANES - N_EXPERTS)))
        rb = jnp.pad(P["router_b"][l], (0, LANES - N_EXPERTS)).reshape(1, LANES)
        x1, h2, logits = _out_proj(att, y, r, k, v, g, x, gt1, sc2, sh2, vecs_out, P["w_out"][l].astype(BF16),
                                   rw, rb, S)

        weights, gather_idx, scatter_idx, group_e, G = _route(logits[:, :N_EXPERTS], T)
        moe_out = _moe_experts(h2, gather_idx, scatter_idx, group_e, P["moe_w1"], P["moe_b1"],
                               P["moe_w2"], P["moe_b2"], l, G=G)
        x = _combine(moe_out, weights, x1, gt2, P["final_g"] if l == depth - 1 else None, S)

        new_k.append(k_rows)
        new_v.append(v_rows)
        new_wkv.append(wkv)
    return x.reshape(B, S, D_MODEL), jnp.stack(new_k), jnp.stack(new_v), jnp.stack(new_wkv), jnp.stack(new_shift)


def kernel(x_prompt, x_sample, c_prompt, c_sample, state_attn_k, state_attn_v, state_wkv, state_shift, w_ada, b_ada, w_in, att_out_g, mu_rkv, mu_wag, decay_w0, decay_w1, decay_w2, iclr_a0, iclr_a1, iclr_a2, gate_g1, gate_g2, vres_mu, vres_v0, vres_v1, vres_v2, k_k, k_a, r_k, lnx_w, lnx_b, w_out, router_w, router_b, moe_w1, moe_b1, moe_w2, moe_b2, final_g):
    P = dict(w_ada=w_ada, b_ada=b_ada, w_in=w_in, att_out_g=att_out_g, mu_rkv=mu_rkv, mu_wag=mu_wag,
             decay_w0=decay_w0, decay_w1=decay_w1, decay_w2=decay_w2, iclr_a0=iclr_a0, iclr_a1=iclr_a1,
             iclr_a2=iclr_a2, gate_g1=gate_g1, gate_g2=gate_g2, vres_mu=vres_mu, vres_v0=vres_v0,
             vres_v1=vres_v1, vres_v2=vres_v2, k_k=k_k, k_a=k_a, r_k=r_k, lnx_w=lnx_w, lnx_b=lnx_b,
             w_out=w_out, router_w=router_w, router_b=router_b, moe_w1=moe_w1, moe_b1=moe_b1,
             moe_w2=moe_w2, moe_b2=moe_b2, final_g=final_g)
    depth = w_ada.shape[0]
    B = x_prompt.shape[0]
    wkv0_prompt = jnp.zeros((depth, B, N_RWKV_HEADS, HEAD_DIM, HEAD_DIM), F32)
    shift0_prompt = jnp.zeros((depth, B, D_MODEL), x_prompt.dtype)
    y_p, k_p, v_p, wkv_p, shift_p = _trunk(x_prompt, c_prompt, P, wkv0_prompt, shift0_prompt, None, None)
    y_s, k_s, v_s, wkv_s, shift_s = _trunk(x_sample, c_sample, P, state_wkv, state_shift, state_attn_k, state_attn_v)
    return (y_p, y_s, k_p, v_p, wkv_p, shift_p, k_s, v_s, wkv_s, shift_s)
```

```python
import functools
import math

import jax
import jax.numpy as jnp
from jax import lax
from jax.experimental import pallas as pl
from jax.experimental.pallas import tpu as pltpu

F32 = jnp.float32
BF16 = jnp.bfloat16
HIGHEST = lax.Precision.HIGHEST

D_MODEL = 1024
HEAD_DIM = 64
N_ATT_HEADS = 8
N_RWKV_HEADS = 8
ATT_WIDTH = N_ATT_HEADS * HEAD_DIM
RWKV_WIDTH = N_RWKV_HEADS * HEAD_DIM
IN_COLS = 3 * ATT_WIDTH + 3 * RWKV_WIDTH
DILATIONS = (1, 4, 16)
ATT_BLOCK = 128
MAX_WINDOW = 2048
N_EXPERTS = 32
TOP_K = 4
D_EXPERT = 1024
SWIGLU_LIMIT = 7.0
SWIGLU_ALPHA = 1.702
NORM_EPS = 1e-5
LNX_EPS = 64e-5
NEG_BIG = -1e30
WKV_CHUNK = 64
LANES = 128
VMEM_LIMIT = 56 * 1024 * 1024
LORA_WIDTHS = (64, 64, 128, 32)
LORA_COLS = 384
N_RKV = 3 * RWKV_WIDTH
N_REST = N_RKV + 2 * LORA_COLS
ROW_TILE_SUBLANES = D_MODEL // LANES


def _params(n_grid):
    return pltpu.CompilerParams(dimension_semantics=("arbitrary",) * n_grid,
                                vmem_limit_bytes=VMEM_LIMIT)


def _dot(a, b, precision=None):
    return jnp.dot(a, b, preferred_element_type=F32, precision=precision)


def _dot_nt(a, b, precision=None):
    return lax.dot_general(a, b, (((1,), (1,)), ((), ())), preferred_element_type=F32, precision=precision)


def _dot_tn(a, b, precision=None):
    return lax.dot_general(a, b, (((0,), (0,)), ((), ())), preferred_element_type=F32, precision=precision)


def _split3(x):
    hi = x.astype(BF16)
    r1 = x - hi.astype(F32)
    mid = r1.astype(BF16)
    lo = (r1 - mid.astype(F32)).astype(BF16)
    return hi, mid, lo


def _head_sum_matrix():
    hr = lax.broadcasted_iota(jnp.int32, (LANES, LANES), 0) // HEAD_DIM
    hc = lax.broadcasted_iota(jnp.int32, (LANES, LANES), 1) // HEAD_DIM
    return (hr == hc).astype(F32)


def _rms_mod(x, sc, sh):
    return x * lax.rsqrt(jnp.mean(x * x, axis=-1, keepdims=True) + NORM_EPS) * (1.0 + sc) + sh


def _row_tile(T, S, tm):
    tm = min(tm, T)
    assert T % tm == 0 and (S == 1 or S % tm == 0)
    return tm


def _mod_operand(v, S, tm):
    B, N = v.shape
    if S == 1:
        return v.reshape(1, B, N), pl.BlockSpec((None, tm, N), lambda i: (0, i, 0))
    return v.reshape(B, 1, N), pl.BlockSpec((None, 1, N), lambda i: (i * tm // S, 0, 0))


def _mm_body(x_ref, w_ref, o_ref):
    o_ref[...] = _dot(x_ref[...].astype(BF16), w_ref[...])


def _mm_rows(x, w, *, tn=None):
    M, K = x.shape
    N = w.shape[1]
    Mp = -(-M // 8) * 8
    if Mp != M:
        x = jnp.pad(x, ((0, Mp - M), (0, 0)))
    tn = N if tn is None else tn
    assert N % tn == 0
    out = pl.pallas_call(
        _mm_body,
        grid=(N // tn,),
        in_specs=[pl.BlockSpec((Mp, K), lambda j: (0, 0)),
                  pl.BlockSpec((K, tn), lambda j: (0, j))],
        out_specs=pl.BlockSpec((Mp, tn), lambda j: (0, j)),
        out_shape=jax.ShapeDtypeStruct((Mp, N), F32),
        compiler_params=_params(1),
        name="mm_rows",
    )(x, w)
    return out[:M]


def _pre_proj_body(x_ref, sc_ref, sh_ref, w_ref, qkv_ref, kv_ref, rest_ref):
    h = _rms_mod(x_ref[...], sc_ref[...], sh_ref[...])
    acc = _dot(h.astype(BF16), w_ref[...])
    n_slabs = qkv_ref.shape[0]
    for i in range(n_slabs):
        qkv_ref[i] = acc[:, i * LANES:(i + 1) * LANES]
    tm = acc.shape[0]
    for h in range(N_ATT_HEADS):
        k_h = acc[:, ATT_WIDTH + h * HEAD_DIM:ATT_WIDTH + (h + 1) * HEAD_DIM]
        v_h = acc[:, 2 * ATT_WIDTH + h * HEAD_DIM:2 * ATT_WIDTH + (h + 1) * HEAD_DIM]
        kv_ref[pl.ds(h, tm, stride=N_ATT_HEADS), :] = jnp.concatenate([k_h, v_h], axis=1)
    rest_ref[...] = acc[:, n_slabs * LANES:]


def _pre_proj(x, sc, sh, w, S, *, tm=256):
    T, K = x.shape
    N = w.shape[1]
    tm = _row_tile(T, S, tm)
    n_slabs = 3 * ATT_WIDTH // LANES
    n_rest = N - 3 * ATT_WIDTH
    sc_op, sc_spec = _mod_operand(sc, S, tm)
    sh_op, sh_spec = _mod_operand(sh, S, tm)
    return pl.pallas_call(
        _pre_proj_body,
        grid=(T // tm,),
        in_specs=[pl.BlockSpec((tm, K), lambda i: (i, 0)), sc_spec, sh_spec,
                  pl.BlockSpec((K, N), lambda i: (0, 0))],
        out_specs=[pl.BlockSpec((n_slabs, tm, LANES), lambda i: (0, i, 0)),
                   pl.BlockSpec((tm * N_ATT_HEADS, LANES), lambda i: (i, 0)),
                   pl.BlockSpec((tm, n_rest), lambda i: (i, 0))],
        out_shape=[jax.ShapeDtypeStruct((n_slabs, T, LANES), F32), jax.ShapeDtypeStruct((T * N_ATT_HEADS, LANES), F32),
                   jax.ShapeDtypeStruct((T, n_rest), F32)],
        compiler_params=_params(1),
        name="pre_proj",
    )(x, sc_op, sh_op, w)


def _att_prompt_body(qkv_ref, g_ref, o_ref, m_ref, l_ref, *, seq):
    scale = HEAD_DIM ** -0.5
    qi = lax.broadcasted_iota(jnp.int32, (ATT_BLOCK, ATT_BLOCK), 0)
    kj = lax.broadcasted_iota(jnp.int32, (ATT_BLOCK, ATT_BLOCK), 1)
    mask_cur = kj <= qi
    lane = lax.broadcasted_iota(jnp.int32, (ATT_BLOCK, LANES), 1)
    first_head = lane < HEAD_DIM
    n_pairs = ATT_WIDTH // LANES

    for branch, dil in enumerate(DILATIONS):
        nb = seq // (ATT_BLOCK * dil)

        def block(bi, carry, dil=dil, nb=nb, branch=branch):
            r = bi // nb
            j = bi % nb
            jp = jnp.maximum(j - 1, 0)
            if dil > 1:
                rows = pl.ds(r + dil * ATT_BLOCK * j, ATT_BLOCK, stride=dil)
                prows = pl.ds(r + dil * ATT_BLOCK * jp, ATT_BLOCK, stride=dil)
            else:
                rows = pl.ds(pl.multiple_of(ATT_BLOCK * j, ATT_BLOCK), ATT_BLOCK)
                prows = pl.ds(pl.multiple_of(ATT_BLOCK * jp, ATT_BLOCK), ATT_BLOCK)
            mp = kj >= qi + jnp.where(j > 0, 0, ATT_BLOCK)
            if branch > 0:
                m_old = m_ref[rows, :]
                l_old = l_ref[rows, :]
            heads = [(p, hh) for p in range(n_pairs) for hh in range(2)]
            sel = lambda hh: first_head if hh == 0 else jnp.logical_not(first_head)
            k1 = [qkv_ref[n_pairs + p, prows, :].astype(BF16) for p in range(n_pairs)]
            k2 = [qkv_ref[n_pairs + p, rows, :].astype(BF16) for p in range(n_pairs)]
            qh = []
            for p in range(n_pairs):
                q = qkv_ref[p, rows, :] * scale
                qh += [jnp.where(sel(hh), q, 0.0).astype(BF16) for hh in range(2)]
            s1 = [_dot_nt(qh[i], k1[p]) for i, (p, hh) in enumerate(heads)]
            s2 = [_dot_nt(qh[i], k2[p]) for i, (p, hh) in enumerate(heads)]
            p1, p2, mbs = [], [], []
            for i in range(len(heads)):
                a1 = jnp.where(mp, s1[i], NEG_BIG)
                a2 = jnp.where(mask_cur, s2[i], NEG_BIG)
                mb = jnp.max(jnp.maximum(a1, a2), axis=1, keepdims=True)
                p1.append(jnp.exp(a1 - mb).astype(BF16))
                p2.append(jnp.exp(a2 - mb).astype(BF16))
                mbs.append(mb)
            ones = jnp.ones((ATT_BLOCK, LANES), BF16)
            obs, lbs = [], []
            for p in range(n_pairs):
                v1 = qkv_ref[2 * n_pairs + p, prows, :]
                v2 = qkv_ref[2 * n_pairs + p, rows, :]
                for hh in range(2):
                    i = 2 * p + hh
                    v1h = jnp.where(sel(hh), v1, 0.0).astype(BF16)
                    v2h = jnp.where(sel(hh), v2, 0.0).astype(BF16)
                    obs.append(_dot(p1[i], v1h) + _dot(p2[i], v2h))
                    lbs.append(_dot(p1[i], ones) + _dot(p2[i], ones))
            m_new = jnp.zeros((ATT_BLOCK, LANES), F32)
            l_new = jnp.zeros((ATT_BLOCK, LANES), F32)
            for p in range(n_pairs):
                out = jnp.zeros((ATT_BLOCK, LANES), F32)
                alpha = jnp.zeros((ATT_BLOCK, LANES), F32)
                for hh in range(2):
                    h = 2 * p + hh
                    mb, lb, ob = mbs[h], lbs[h], obs[h]
                    if branch > 0:
                        mo = m_old[:, h:h + 1]
                        lo = l_old[:, h:h + 1]
                        mn = jnp.maximum(mo, mb)
                        a_old = jnp.exp(mo - mn)
                        a_new = jnp.exp(mb - mn)
                        lb = a_old * lo + a_new * lb
                        ob = ob * a_new
                        alpha = jnp.where(sel(hh), a_old, alpha)
                        mb = mn
                    out = out + ob
                    m_new = jnp.where(lane == h, mb, m_new)
                    l_new = jnp.where(lane == h, lb, l_new)
                if branch > 0:
                    out = out + alpha * o_ref[p, rows, :]
                o_ref[p, rows, :] = out
            m_ref[rows, :] = m_new
            l_ref[rows, :] = l_new
            return carry

        lax.fori_loop(0, seq // ATT_BLOCK, block, 0)

    head_sum = _head_sum_matrix()

    def finish(bi, carry):
        rows = pl.ds(pl.multiple_of(bi * ATT_BLOCK, ATT_BLOCK), ATT_BLOCK)
        l_all = l_ref[rows, :]
        for p in range(n_pairs):
            den = jnp.where(first_head, l_all[:, 2 * p:2 * p + 1], l_all[:, 2 * p + 1:2 * p + 2])
            att = o_ref[p, rows, :] / den
            ms = _dot(att * att, head_sum, HIGHEST) * (1.0 / HEAD_DIM)
            o_ref[p, rows, :] = att * lax.rsqrt(ms + NORM_EPS) * g_ref[:, p * LANES:(p + 1) * LANES]
        return carry

    lax.fori_loop(0, seq // ATT_BLOCK, finish, 0)


def _att_prompt(qkv, B, gain):
    n3, T, _ = qkv.shape
    S = T // B
    n_pairs = n3 // 3
    assert S % (ATT_BLOCK * DILATIONS[-1]) == 0
    return pl.pallas_call(
        functools.partial(_att_prompt_body, seq=S),
        grid=(B,),
        in_specs=[pl.BlockSpec((n3, S, LANES), lambda b: (0, b, 0), pipeline_mode=pl.Buffered(1)),
                  pl.BlockSpec((1, ATT_WIDTH), lambda b: (0, 0))],
        out_specs=pl.BlockSpec((n_pairs, S, LANES), lambda b: (0, b, 0)),
        out_shape=jax.ShapeDtypeStruct((n_pairs, T, LANES), F32),
        scratch_shapes=[pltpu.VMEM((S, LANES), F32), pltpu.VMEM((S, LANES), F32)],
        compiler_params=_params(1),
        name="att_prompt",
    )(qkv, gain.reshape(1, ATT_WIDTH))


def _att_decode_body(q_ref, kn_ref, vn_ref, *refs):
    nd = len(DILATIONS)
    kb_refs, vb_refs, (g_ref, o_ref) = refs[:nd], refs[nd:2 * nd], refs[2 * nd:]
    scale = HEAD_DIM ** -0.5
    seg = (lax.broadcasted_iota(jnp.int32, (ATT_WIDTH, N_ATT_HEADS), 0) // HEAD_DIM
           == lax.broadcasted_iota(jnp.int32, (ATT_WIDTH, N_ATT_HEADS), 1)).astype(F32)
    seg_t = (lax.broadcasted_iota(jnp.int32, (N_ATT_HEADS, ATT_WIDTH), 1) // HEAD_DIM
             == lax.broadcasted_iota(jnp.int32, (N_ATT_HEADS, ATT_WIDTH), 0)).astype(F32)
    rows8 = lambda t: jnp.broadcast_to(t, (8, t.shape[-1]))
    q = rows8(q_ref[...]) * scale
    kn = rows8(kn_ref[...])
    vn = rows8(vn_ref[...])
    s_new = _dot(q * kn, seg, HIGHEST)
    parts = []
    for kb_ref, vb_ref in zip(kb_refs, vb_refs):
        kb = kb_ref[...]
        vb = vb_ref[...]
        s = _dot(kb * q[0:1], seg, HIGHEST)
        m = jnp.maximum(jnp.max(s, axis=0, keepdims=True), s_new)
        p = jnp.exp(s - m[0:1])
        p_new = jnp.exp(s_new - m)
        den = jnp.sum(p, axis=0, keepdims=True) + p_new
        pe = _dot(p, seg_t, HIGHEST)
        num = jnp.sum(pe * vb, axis=0, keepdims=True) + _dot(p_new, seg_t, HIGHEST) * vn
        parts.append((num, den, m))
    m_all = jnp.maximum(jnp.maximum(parts[0][2], parts[1][2]), parts[2][2])
    num = jnp.zeros((8, ATT_WIDTH), F32)
    den = jnp.zeros((8, N_ATT_HEADS), F32)
    for n_b, d_b, m_b in parts:
        w = jnp.exp(m_b - m_all)
        num = num + n_b * _dot(w, seg_t, HIGHEST)
        den = den + d_b * w
    att = num / _dot(den, seg_t, HIGHEST)
    ms = _dot(_dot(att * att, seg, HIGHEST) * (1.0 / HEAD_DIM), seg_t, HIGHEST)
    o_ref[...] = (att * lax.rsqrt(ms + NORM_EPS) * g_ref[...])[0:1]


def _att_decode(q, k_new, v_new, k_buf, v_buf, layer, gain):
    depth, B, n_buf, W = k_buf.shape
    row = pl.BlockSpec((None, 1, W), lambda b: (b, 0, 0))
    views, specs = [], []
    for buf in (k_buf, v_buf):
        for dil in DILATIONS:
            assert n_buf % (ATT_BLOCK * dil) == 0
            views.append(buf.reshape(depth, B, n_buf // dil, dil * W))
            specs.append(pl.BlockSpec((None, None, ATT_BLOCK, W),
                                      lambda b, dil=dil: (layer, b, n_buf // (dil * ATT_BLOCK) - 1, 0)))
    return pl.pallas_call(
        _att_decode_body,
        grid=(B,),
        in_specs=[row, row, row] + specs + [pl.BlockSpec((1, W), lambda b: (0, 0))],
        out_specs=row,
        out_shape=jax.ShapeDtypeStruct((B, 1, W), F32),
        compiler_params=_params(1),
        name="att_decode",
    )(q, k_new, v_new, *views, gain.reshape(1, W))


def _rwkv_prep_body(*refs, S, tm, has_vfirst):
    if has_vfirst:
        cur_ref, prev_ref, mu_ref, vec_ref, w2_ref, vf_ref = refs[:6]
        rest = refs[6:]
    else:
        cur_ref, prev_ref, mu_ref, vec_ref, w2_ref = refs[:5]
        vf_ref, rest = None, refs[5:]
    r_ref, lw_ref, k_ref, v_ref, a_ref, b_ref, g_ref, carry = rest
    c_lp = N_RKV + LORA_COLS
    rkv = cur_ref[:, :N_RKV]
    lp = cur_ref[:, c_lp:]
    if S == 1:
        rkv_prev = prev_ref[:, :N_RKV]
        lp_prev = prev_ref[:, c_lp:]
    else:
        i = pl.program_id(0)

        @pl.when(i == 0)
        def _():
            carry[...] = jnp.zeros_like(carry)

        first = (i % (S // tm)) == 0
        prev_row = jnp.where(first, prev_ref[...], carry[...])
        row0 = lax.broadcasted_iota(jnp.int32, (tm, 1), 0) == 0
        rkv_prev = jnp.where(row0, prev_row[:, :N_RKV], pltpu.roll(rkv, 1, 0))
        lp_prev = jnp.where(row0, prev_row[:, c_lp:], pltpu.roll(lp, 1, 0))
        carry[...] = cur_ref[tm - 1:tm, :]
    rkv = rkv + (rkv_prev - rkv) * mu_ref[...]
    r = rkv[:, :RWKV_WIDTH]
    kr = rkv[:, RWKV_WIDTH:2 * RWKV_WIDTH]
    vr = rkv[:, 2 * RWKV_WIDTH:]
    l1 = cur_ref[:, N_RKV:c_lp] + lp_prev
    t0 = l1[:, :LANES]
    lane = lax.broadcasted_iota(jnp.int32, (tm, LANES), 1)
    act = jnp.concatenate([jnp.where(lane < LORA_WIDTHS[0], jnp.tanh(t0), t0),
                           jax.nn.sigmoid(l1[:, LANES:2 * LANES]),
                           l1[:, 2 * LANES:]], axis=1)
    l2 = _dot(act.astype(BF16), w2_ref[...])
    w0, a0, v0, k_k, k_a = (vec_ref[j:j + 1, :] for j in range(5))
    z = w0 + l2[:, :RWKV_WIDTH]
    softplus_neg = jnp.maximum(-z, 0.0) + jnp.log(1.0 + jnp.exp(-jnp.abs(z)))
    lw_ref[...] = -jnp.exp(-softplus_neg - 0.5)
    a = jax.nn.sigmoid(a0 + l2[:, RWKV_WIDTH:2 * RWKV_WIDTH])
    g_ref[...] = l2[:, 2 * RWKV_WIDTH:3 * RWKV_WIDTH]
    if has_vfirst:
        vr = vr + (vf_ref[...] - vr) * jax.nn.sigmoid(v0 + l2[:, 3 * RWKV_WIDTH:])
    kk = kr * k_k
    head_sum = _head_sum_matrix()
    ss = jnp.concatenate([_dot(kk[:, p * LANES:(p + 1) * LANES] ** 2, head_sum, HIGHEST)
                          for p in range(RWKV_WIDTH // LANES)], axis=1)
    kk = kk / jnp.maximum(jnp.sqrt(ss), 1e-12)
    r_ref[...] = r
    k_ref[...] = kr * (1.0 + (a - 1.0) * k_a)
    v_ref[...] = vr
    a_ref[...] = -kk
    b_ref[...] = kk * a


def _rwkv_prep(rest, rest_prev, mu_rkv, vecs, w2, v_first, S, *, tm=256):
    T = rest.shape[0]
    tm = _row_tile(T, S, tm)
    prev_op, prev_spec = _mod_operand(rest_prev, S, tm)
    tile = pl.BlockSpec((tm, RWKV_WIDTH), lambda i: (i, 0))
    const = lambda a: pl.BlockSpec(a.shape, lambda i: (0,) * a.ndim)
    ops = [rest, prev_op, mu_rkv, vecs, w2]
    specs = [pl.BlockSpec((tm, N_REST), lambda i: (i, 0)), prev_spec, const(mu_rkv), const(vecs), const(w2)]
    if v_first is not None:
        ops.append(v_first)
        specs.append(tile)
    return pl.pallas_call(
        functools.partial(_rwkv_prep_body, S=S, tm=tm, has_vfirst=v_first is not None),
        grid=(T // tm,),
        in_specs=specs,
        out_specs=[tile] * 7,
        out_shape=[jax.ShapeDtypeStruct((T, RWKV_WIDTH), F32)] * 7,
        scratch_shapes=[pltpu.VMEM((1, N_REST), F32)],
        compiler_params=_params(1),
        name="rwkv_prep",
    )(*ops)


def _wkv_body(r_ref, lw_ref, k_ref, v_ref, a_ref, b_ref, s0_ref, y_ref, sn_ref, s_scr, *, n_chunks):
    C = WKV_CHUNK
    C2 = 2 * C
    n_pairs = RWKV_WIDTH // LANES
    t = pl.program_id(1)

    @pl.when(t == 0)
    def _():
        s_scr[...] = s0_ref[...]

    row = lax.broadcasted_iota(jnp.int32, (C2, C2), 0)
    col = lax.broadcasted_iota(jnp.int32, (C2, C2), 1)
    same_head = (row // C) == (col // C)
    lower_strict = same_head & (col < row)
    lower_incl = same_head & (col <= row)
    eye = (row == col).astype(F32)
    tri = (lax.broadcasted_iota(jnp.int32, (C, C), 1) <= lax.broadcasted_iota(jnp.int32, (C, C), 0)).astype(BF16)
    first_head = lax.broadcasted_iota(jnp.int32, (C, LANES), 1) < HEAD_DIM
    block_diag = _head_sum_matrix() > 0.5
    ones = jnp.ones((C, LANES), BF16)
    bf = lambda x: x.astype(BF16)

    def split(x):
        return jnp.concatenate([jnp.where(first_head, x, 0.0), jnp.where(first_head, 0.0, x)], axis=0)

    items = [(c, p) for c in range(n_chunks) for p in range(n_pairs)]
    ld = lambda ref, c, p: ref[c * C:(c + 1) * C, p * LANES:(p + 1) * LANES]
    each = lambda f, *ls: [f(*xs) for xs in zip(*ls)]
    r, lw, k, v, a, b = ([ld(ref, c, p) for c, p in items] for ref in (r_ref, lw_ref, k_ref, v_ref, a_ref, b_ref))

    lw3 = each(_split3, lw)
    cum = each(lambda t3: sum(_dot(tri, x) for x in t3), lw3)
    x_all = each(lambda a_, r_, lw_, c_: bf(jnp.concatenate([split(a_ * jnp.exp(c_ - lw_)), split(r_ * jnp.exp(c_))],
                                                            axis=0)), a, r, lw, cum)

    def keys(b_, k_, c_):
        g_inv = jnp.exp(-c_)
        bt, kt = b_ * g_inv, k_ * g_inv
        return bf(jnp.concatenate([bt, bt, kt, kt], axis=0))

    y_all = each(keys, b, k, cum)
    gram = each(_dot_nt, x_all, y_all)
    l_pow = each(lambda g_: jnp.where(lower_strict, g_[0:C2, 0:C2], 0.0), gram)
    m_ak = each(lambda g_: bf(jnp.where(lower_strict, g_[0:C2, C2:], 0.0)), gram)
    m_rb = each(lambda g_: bf(jnp.where(lower_incl, g_[C2:, 0:C2], 0.0)), gram)
    m_rk = each(lambda g_: bf(jnp.where(lower_incl, g_[C2:, C2:], 0.0)), gram)
    v_st = each(lambda v_: bf(split(v_)), v)
    t_inv = each(lambda l_: eye + l_, l_pow)
    for _ in range(int(math.log2(C)) - 1):
        l_pow = each(lambda l_: _dot(bf(l_), bf(l_)), l_pow)
        t_inv = each(lambda t_, l_: t_ + _dot(bf(t_), bf(l_)), t_inv, l_pow)
    t_inv = each(bf, t_inv)
    akv = each(_dot, m_ak, v_st)
    rkv = each(_dot, m_rk, v_st)
    tail = each(lambda c_: jnp.exp(c_[C - 1:C, :] - c_), cum)
    b_end = each(lambda b_, t_: bf(b_ * t_), b, tail)
    kv_end = each(lambda k_, t_, v_: _dot_tn(bf(k_ * t_), bf(v_)), k, tail, v)
    g_col = each(lambda t3: jnp.exp(sum(_dot_tn(x, ones) for x in t3)), lw3)

    state = [s_scr[p] for p in range(n_pairs)]
    for c in range(n_chunks):
        sel = lambda ls: ls[c * n_pairs:(c + 1) * n_pairs]
        xs = each(lambda x_, s_: _dot(x_, bf(s_)), sel(x_all), state)
        u_st = each(lambda t_, xs_, akv_: _dot(t_, bf(xs_[0:C2] + akv_)), sel(t_inv), xs, sel(akv))
        y_st = each(lambda xs_, m_, u_, rkv_: xs_[C2:] + _dot(m_, bf(u_)) + rkv_, xs, sel(m_rb), u_st, sel(rkv))
        upd = each(lambda b_, u_: _dot_tn(b_, bf(u_[0:C] + u_[C:])), sel(b_end), u_st)
        for p in range(n_pairs):
            y_ref[c * C:(c + 1) * C, p * LANES:(p + 1) * LANES] = y_st[p][0:C] + y_st[p][C:]
        state = each(lambda g_, s_, u_, kv_: g_ * s_ + jnp.where(block_diag, u_ + kv_, 0.0),
                     sel(g_col), state, upd, sel(kv_end))
    for p in range(n_pairs):
        s_scr[p] = state[p]

    @pl.when(t == pl.num_programs(1) - 1)
    def _():
        sn_ref[...] = s_scr[...]


def _wkv(r, lw, k, v, a, b, s0, *, chunks_per_step=4):
    B, T, W = r.shape
    H = N_RWKV_HEADS
    n_pairs = W // LANES
    tt = min(T, WKV_CHUNK * chunks_per_step)
    assert T % tt == 0 and tt % WKV_CHUNK == 0
    s0t = jnp.swapaxes(s0, -1, -2).reshape(B, n_pairs, 2, HEAD_DIM, HEAD_DIM)
    z = jnp.zeros_like(s0t[:, :, 0])
    s0bd = jnp.concatenate([jnp.concatenate([s0t[:, :, 0], z], axis=-1),
                            jnp.concatenate([z, s0t[:, :, 1]], axis=-1)], axis=-2)
    seq = pl.BlockSpec((None, tt, W), lambda bi, ti: (bi, ti, 0))
    st = pl.BlockSpec((None, n_pairs, LANES, LANES), lambda bi, ti: (bi, 0, 0, 0))
    y, sbd = pl.pallas_call(
        functools.partial(_wkv_body, n_chunks=tt // WKV_CHUNK),
        grid=(B, T // tt),
        in_specs=[seq] * 6 + [st],
        out_specs=[seq, st],
        out_shape=[jax.ShapeDtypeStruct((B, T, W), F32), jax.ShapeDtypeStruct((B, n_pairs, LANES, LANES), F32)],
        scratch_shapes=[pltpu.VMEM((n_pairs, LANES, LANES), F32)],
        compiler_params=_params(2),
        name="wkv",
    )(r, lw, k, v, a, b, s0bd)
    sn = jnp.stack([sbd[:, :, :HEAD_DIM, :HEAD_DIM], sbd[:, :, HEAD_DIM:, HEAD_DIM:]], axis=2)
    return y, jnp.swapaxes(sn.reshape(B, H, HEAD_DIM, HEAD_DIM), -1, -2)


def _out_proj_body(att_ref, y_ref, r_ref, k_ref, v_ref, g_ref, x_ref, gt_ref, sc_ref, sh_ref,
                   vec_ref, wo_ref, rw_ref, rb_ref, x1_ref, h2_ref, logit_ref):
    head_sum = _head_sum_matrix().astype(BF16)
    hsum = lambda t: sum(_dot(part, head_sum) for part in _split3(t))
    n_pairs = RWKV_WIDTH // LANES
    parts = [att_ref[p].astype(BF16) for p in range(ATT_WIDTH // LANES)]
    for p in range(n_pairs):
        cols = slice(p * LANES, (p + 1) * LANES)
        y = y_ref[:, cols]
        d = y - hsum(y) * (1.0 / HEAD_DIM)
        var = hsum(d * d) * (1.0 / HEAD_DIM)
        yn = d * lax.rsqrt(var + LNX_EPS) * vec_ref[0:1, cols] + vec_ref[1:2, cols]
        rk = r_ref[:, cols] * k_ref[:, cols] * vec_ref[2:3, cols]
        bonus = hsum(rk) * v_ref[:, cols]
        parts.append(((yn + bonus) * g_ref[:, cols]).astype(BF16))
    mix = _dot(jnp.concatenate(parts, axis=1), wo_ref[...])
    x1 = x_ref[...] + gt_ref[...] * mix
    x1_ref[...] = x1
    h2 = _rms_mod(x1, sc_ref[...], sh_ref[...])
    tm = h2.shape[0]
    for j in range(ROW_TILE_SUBLANES):
        h2_ref[pl.ds(j, tm, stride=ROW_TILE_SUBLANES), :] = h2[:, j * LANES:(j + 1) * LANES]
    logit_ref[...] = _dot(h2, rw_ref[...], HIGHEST) + rb_ref[...]


def _out_proj(att, y, r, k, v, g, x, gt, sc, sh, vecs, w_out, rw, rb, S, *, tm=256):
    T = x.shape[0]
    tm = _row_tile(T, S, tm)
    n_pairs = att.shape[0]
    tile = lambda n: pl.BlockSpec((tm, n), lambda i: (i, 0))
    const = lambda a: pl.BlockSpec(a.shape, lambda i: (0,) * a.ndim)
    mods = [_mod_operand(m, S, tm) for m in (gt, sc, sh)]
    return pl.pallas_call(
        _out_proj_body,
        grid=(T // tm,),
        in_specs=[pl.BlockSpec((n_pairs, tm, LANES), lambda i: (0, i, 0))] + [tile(RWKV_WIDTH)] * 5 + [tile(D_MODEL)]
                 + [m[1] for m in mods] + [const(vecs), const(w_out), const(rw), const(rb)],
        out_specs=[tile(D_MODEL), pl.BlockSpec((tm * ROW_TILE_SUBLANES, LANES), lambda i: (i, 0)), tile(LANES)],
        out_shape=[jax.ShapeDtypeStruct((T, D_MODEL), F32), jax.ShapeDtypeStruct((T * ROW_TILE_SUBLANES, LANES), F32),
                   jax.ShapeDtypeStruct((T, LANES), F32)],
        compiler_params=_params(1),
        name="out_proj",
    )(att, y, r, k, v, g, x, *[m[0] for m in mods], vecs, w_out, rw, rb)


def _moe_body(ge_ref, gidx_ref, sidx_ref, x_hbm, w1_ref, b1_ref, w2_ref, b2_ref, out_hbm,
              xb0, xb1, ob0, ob1, w1b, w2b, gsem, ssem, *, G, ng):
    s = pl.program_id(0)

    def step(cur, x_cur, x_oth, o_cur, o_oth):
        oth = 1 - cur

        tile_rows = lambda i: pl.ds(i * ROW_TILE_SUBLANES, ROW_TILE_SUBLANES)

        hbm_rows = lambda first: pl.ds(pl.multiple_of(first, ROW_TILE_SUBLANES), ROW_TILE_SUBLANES)

        def start_gather():
            for i in range(G):
                pltpu.make_async_copy(x_hbm.at[hbm_rows(gidx_ref[0, 0, i])], x_cur.at[tile_rows(i)],
                                      gsem.at[cur]).start(priority=i % 2)

        def start_scatter():
            for i in range(G):
                pltpu.make_async_copy(o_cur.at[tile_rows(i)], out_hbm.at[hbm_rows(sidx_ref[0, 0, i])],
                                      ssem.at[cur]).start(priority=i % 2)

        def wait_rows(buf, sem):
            pltpu.make_async_copy(buf, buf, sem).wait()

        def evaluate():
            col = lambda j: pl.ds(j, G, stride=ROW_TILE_SUBLANES)
            x = jnp.concatenate([x_oth[col(j), :] for j in range(ROW_TILE_SUBLANES)], axis=1).astype(BF16)
            u = _dot(x, w1b[...]) + b1_ref[...]
            glu = jnp.minimum(u[:, :D_EXPERT], SWIGLU_LIMIT)
            lin = jnp.clip(u[:, D_EXPERT:], -SWIGLU_LIMIT, SWIGLU_LIMIT)
            act = glu * jax.nn.sigmoid(SWIGLU_ALPHA * glu) * (lin + 1.0)
            y = _dot(act.astype(BF16), w2b[...]) + b2_ref[...]
            for j in range(ROW_TILE_SUBLANES):
                o_oth[col(j), :] = y[:, j * LANES:(j + 1) * LANES]

        @pl.when((s >= 1) & (s <= ng))
        def _():
            wait_rows(x_oth, gsem.at[oth])
            e = jnp.clip(s - 1, 0, ng - 1)

            @pl.when((s == 1) | (ge_ref[e] != ge_ref[jnp.maximum(e - 1, 0)]))
            def _():
                w1b[...] = w1_ref[...].astype(BF16)
                w2b[...] = w2_ref[...].astype(BF16)

        @pl.when(s >= 3)
        def _():
            wait_rows(o_oth, ssem.at[oth])

        @pl.when(s == 0)
        def _():
            start_gather()

        @pl.when(s == 1)
        def _():
            start_gather()
            evaluate()

        @pl.when((s >= 2) & (s < ng))
        def _():
            start_gather()
            start_scatter()
            evaluate()

        @pl.when(s == ng)
        def _():
            start_scatter()
            evaluate()

        @pl.when(s == ng + 1)
        def _():
            start_scatter()
            wait_rows(o_cur, ssem.at[cur])

    @pl.when(s % 2 == 0)
    def _():
        step(0, xb0, xb1, ob0, ob1)

    @pl.when(s % 2 == 1)
    def _():
        step(1, xb1, xb0, ob1, ob0)


def _moe_experts(x, gather_idx, scatter_idx, group_e, w1, b1, w2, b2, layer, *, G):
    T = x.shape[0] // ROW_TILE_SUBLANES
    ng = group_e.shape[0]
    assert ng >= 2
    depth = w1.shape[0]
    gidx = (gather_idx * ROW_TILE_SUBLANES).reshape(ng, 1, G)
    sidx = (scatter_idx * ROW_TILE_SUBLANES).reshape(ng, 1, G)
    smem_blk = lambda f: pl.BlockSpec((1, 1, G), f, memory_space=pltpu.SMEM)
    expert = lambda s, ge: (layer, ge[jnp.clip(s - 1, 0, ng - 1)], 0, 0)
    row_tile = (G * ROW_TILE_SUBLANES, LANES)
    grid_spec = pltpu.PrefetchScalarGridSpec(
        num_scalar_prefetch=1,
        grid=(ng + 2,),
        in_specs=[
            smem_blk(lambda s, ge: (jnp.minimum(s, ng - 1), 0, 0)),
            smem_blk(lambda s, ge: (jnp.clip(s - 2, 0, ng - 1), 0, 0)),
            pl.BlockSpec(memory_space=pl.ANY),
            pl.BlockSpec((None, None, D_MODEL, 2 * D_EXPERT), expert),
            pl.BlockSpec((None, None, 1, 2 * D_EXPERT), expert),
            pl.BlockSpec((None, None, D_EXPERT, D_MODEL), expert),
            pl.BlockSpec((None, None, 1, D_MODEL), expert),
        ],
        out_specs=pl.BlockSpec(memory_space=pl.ANY),
        scratch_shapes=[
            pltpu.VMEM(row_tile, F32),
            pltpu.VMEM(row_tile, F32),
            pltpu.VMEM(row_tile, F32),
            pltpu.VMEM(row_tile, F32),
            pltpu.VMEM((D_MODEL, 2 * D_EXPERT), BF16),
            pltpu.VMEM((D_EXPERT, D_MODEL), BF16),
            pltpu.SemaphoreType.DMA((2,)),
            pltpu.SemaphoreType.DMA((2,)),
        ],
    )
    return pl.pallas_call(
        functools.partial(_moe_body, G=G, ng=ng),
        grid_spec=grid_spec,
        out_shape=jax.ShapeDtypeStruct(((T * TOP_K + G) * ROW_TILE_SUBLANES, LANES), F32),
        compiler_params=_params(1),
        name="moe_experts",
    )(group_e, gidx, sidx, x, w1, b1.reshape(depth, N_EXPERTS, 1, -1), w2, b2.reshape(depth, N_EXPERTS, 1, -1))


def _combine_body(*refs, final):
    outs = refs[:TOP_K]
    w_ref, x_ref, gt_ref = refs[TOP_K:TOP_K + 3]
    w = w_ref[...]
    tm = w.shape[0]
    rows = lambda ref: jnp.concatenate([ref[pl.ds(j, tm, stride=ROW_TILE_SUBLANES), :]
                                        for j in range(ROW_TILE_SUBLANES)], axis=1)
    acc = rows(outs[0]) * w[:, 0:1]
    for kk in range(1, TOP_K):
        acc = acc + rows(outs[kk]) * w[:, kk:kk + 1]
    x2 = x_ref[...] + gt_ref[...] * acc
    if final:
        fg_ref, o_ref = refs[TOP_K + 3:]
        o_ref[...] = x2 * lax.rsqrt(jnp.mean(x2 * x2, axis=-1, keepdims=True) + NORM_EPS) * fg_ref[...]
    else:
        refs[TOP_K + 3][...] = x2


def _combine(moe_out, weights, x, gt, final_g, S, *, tm=512):
    T = x.shape[0]
    tm = _row_tile(T, S, tm)
    gt_op, gt_spec = _mod_operand(gt, S, tm)
    tile = pl.BlockSpec((tm, D_MODEL), lambda i: (i, 0))
    ops = [moe_out] * TOP_K + [weights, x, gt_op]
    specs = [pl.BlockSpec((tm * ROW_TILE_SUBLANES, LANES), lambda i, kk=kk: (kk * (T // tm) + i, 0))
             for kk in range(TOP_K)]
    specs += [pl.BlockSpec((tm, TOP_K), lambda i: (i, 0)), tile, gt_spec]
    if final_g is not None:
        ops.append(final_g.reshape(1, D_MODEL))
        specs.append(pl.BlockSpec((1, D_MODEL), lambda i: (0, 0)))
    return pl.pallas_call(
        functools.partial(_combine_body, final=final_g is not None),
        grid=(T // tm,),
        in_specs=specs,
        out_specs=tile,
        out_shape=jax.ShapeDtypeStruct((T, D_MODEL), F32),
        compiler_params=_params(1),
        name="combine",
    )(*ops)


def _route(logits, T):
    top_val, top_idx = lax.top_k(logits, TOP_K)
    weights = jax.nn.softmax(top_val, axis=-1)
    A = T * TOP_K
    G = max(8, min(256, A // N_EXPERTS))
    ng = -(-A // G) + N_EXPERTS
    flat_e = top_idx.reshape(A).astype(jnp.int32)
    order = jnp.argsort(flat_e).astype(jnp.int32)
    experts = jnp.arange(N_EXPERTS, dtype=jnp.int32)
    counts = jnp.sum(flat_e[:, None] == experts[None, :], axis=0, dtype=jnp.int32)
    padded = (counts + G - 1) // G * G
    pad_end = jnp.cumsum(padded)
    pad_start = pad_end - padded
    start = jnp.cumsum(counts) - counts
    g0 = jnp.arange(ng, dtype=jnp.int32) * G
    group_e = jnp.minimum(jnp.sum(pad_end[None, :] <= g0[:, None], axis=1, dtype=jnp.int32), N_EXPERTS - 1)
    slot = jnp.arange(ng * G, dtype=jnp.int32)
    e = jnp.repeat(group_e, G)
    q = slot - pad_start[e]
    valid = q < counts[e]
    asg = order[jnp.clip(start[e] + q, 0, A - 1)]
    gather_idx = jnp.where(valid, asg // TOP_K, 0)
    scatter_idx = jnp.where(valid, (asg % TOP_K) * T + asg // TOP_K, A + slot % G)
    return weights, gather_idx, scatter_idx, group_e, G


def _layer_weights(l, P):
    mu_w, mu_a, mu_g = P["mu_wag"][l]
    firsts = [(mu_w, P["decay_w1"][l]), (mu_a, P["iclr_a1"][l]), (mu_g, P["gate_g1"][l])]
    seconds = [P["decay_w2"][l], P["iclr_a2"][l], P["gate_g2"][l]]
    if l > 0:
        firsts.append((P["vres_mu"][l - 1], P["vres_v1"][l - 1]))
        seconds.append(P["vres_v2"][l - 1])
    n_used = sum(w.shape[1] for _, w in firsts)
    zpad = jnp.zeros((D_MODEL, LORA_COLS - n_used), F32)
    cur = jnp.concatenate([(1.0 - mu)[:, None] * w for mu, w in firsts] + [zpad], axis=1)
    prev = jnp.concatenate([mu[:, None] * w for mu, w in firsts] + [zpad], axis=1)
    w_big = jnp.concatenate([P["w_in"][l], cur, prev], axis=1).astype(BF16)
    w2 = jnp.zeros((LORA_COLS, 4 * RWKV_WIDTH), F32)
    r0 = 0
    for i, s in enumerate(seconds):
        w2 = w2.at[r0:r0 + s.shape[0], i * RWKV_WIDTH:(i + 1) * RWKV_WIDTH].set(s)
        r0 += s.shape[0]
    return w_big, w2.astype(BF16)


def _trunk(x, c, P, wkv0, h_prev, k_buf, v_buf):
    B, S, _ = x.shape
    T = B * S
    depth = P["w_ada"].shape[0]
    n_pairs = ATT_WIDTH // LANES
    unslab = lambda t: jnp.transpose(t.reshape(n_pairs, B, -1, LANES), (1, 2, 0, 3)).reshape(B, -1, ATT_WIDTH)
    cs = jax.nn.silu(c)
    x = x.reshape(T, D_MODEL)
    new_k, new_v, new_wkv, new_shift = [], [], [], []
    v_first = None
    for l in range(depth):
        mod = _mm_rows(cs, P["w_ada"][l].astype(BF16), tn=1024) + P["b_ada"][l]
        sh1, sc1, gt1, sh2, sc2, gt2 = jnp.split(mod, 6, axis=-1)
        w_big, w_lora2 = _layer_weights(l, P)
        qkv, kv_rows, rest = _pre_proj(x, sc1, sh1, w_big, S)
        rest_prev = _mm_rows(h_prev[l], w_big[:, 3 * ATT_WIDTH:])
        x_last = x.reshape(B, S, D_MODEL)[:, -1]
        new_shift.append(_rms_mod(x_last, sc1, sh1))
        keep = min(MAX_WINDOW, S)
        kv_rows = kv_rows.reshape(B, S, N_ATT_HEADS, 2 * HEAD_DIM)[:, S - keep:]
        k_rows, v_rows = kv_rows[..., :HEAD_DIM], kv_rows[..., HEAD_DIM:]

        if k_buf is None:
            att = _att_prompt(qkv, B, P["att_out_g"][l])
        else:
            assert S == 1
            depth_b, _, n_buf = k_buf.shape[:3]
            q = unslab(qkv[:n_pairs])
            att = _att_decode(q, k_rows.reshape(B, 1, ATT_WIDTH), v_rows.reshape(B, 1, ATT_WIDTH),
                              k_buf.reshape(depth_b, B, n_buf, ATT_WIDTH), v_buf.reshape(depth_b, B, n_buf, ATT_WIDTH),
                              l, P["att_out_g"][l])
            att = jnp.transpose(att.reshape(B, n_pairs, LANES), (1, 0, 2))

        zero = jnp.zeros((RWKV_WIDTH,), F32)
        vecs = jnp.stack([P["decay_w0"][l], P["iclr_a0"][l], P["vres_v0"][l - 1] if l > 0 else zero,
                          P["k_k"][l], P["k_a"][l], zero, zero, zero])
        r, lw, k, v, a, b, g = _rwkv_prep(rest, rest_prev, P["mu_rkv"][l].reshape(1, N_RKV), vecs, w_lora2,
                                          v_first, S)
        if l == 0:
            v_first = v
        Sp = -(-S // WKV_CHUNK) * WKV_CHUNK
        seq = lambda t: jnp.pad(t.reshape(B, S, RWKV_WIDTH), ((0, 0), (0, Sp - S), (0, 0)))
        y, wkv = _wkv(seq(r), seq(lw), seq(k), seq(v), seq(a), seq(b), wkv0[l])
        y = y[:, :S].reshape(T, RWKV_WIDTH)

        vecs_out = jnp.stack([P["lnx_w"][l], P["lnx_b"][l], P["r_k"][l].reshape(RWKV_WIDTH)] + [zero] * 5)
        rw = jnp.pad(P["router_w"][l], ((0, 0), (0, LANES - N_EXPERTS)))
        rb = jnp.pad(P["router_b"][l], (0, LANES - N_EXPERTS)).reshape(1, LANES)
        x1, h2, logits = _out_proj(att, y, r, k, v, g, x, gt1, sc2, sh2, vecs_out, P["w_out"][l].astype(BF16),
                                   rw, rb, S)

        weights, gather_idx, scatter_idx, group_e, G = _route(logits[:, :N_EXPERTS], T)
        moe_out = _moe_experts(h2, gather_idx, scatter_idx, group_e, P["moe_w1"], P["moe_b1"],
                               P["moe_w2"], P["moe_b2"], l, G=G)
        x = _combine(moe_out, weights, x1, gt2, P["final_g"] if l == depth - 1 else None, S)

        new_k.append(k_rows)
        new_v.append(v_rows)
        new_wkv.append(wkv)
    return x.reshape(B, S, D_MODEL), jnp.stack(new_k), jnp.stack(new_v), jnp.stack(new_wkv), jnp.stack(new_shift)


def kernel(x_prompt, x_sample, c_prompt, c_sample, state_attn_k, state_attn_v, state_wkv, state_shift, w_ada, b_ada, w_in, att_out_g, mu_rkv, mu_wag, decay_w0, decay_w1, decay_w2, iclr_a0, iclr_a1, iclr_a2, gate_g1, gate_g2, vres_mu, vres_v0, vres_v1, vres_v2, k_k, k_a, r_k, lnx_w, lnx_b, w_out, router_w, router_b, moe_w1, moe_b1, moe_w2, moe_b2, final_g):
    P = dict(w_ada=w_ada, b_ada=b_ada, w_in=w_in, att_out_g=att_out_g, mu_rkv=mu_rkv, mu_wag=mu_wag,
             decay_w0=decay_w0, decay_w1=decay_w1, decay_w2=decay_w2, iclr_a0=iclr_a0, iclr_a1=iclr_a1,
             iclr_a2=iclr_a2, gate_g1=gate_g1, gate_g2=gate_g2, vres_mu=vres_mu, vres_v0=vres_v0,
             vres_v1=vres_v1, vres_v2=vres_v2, k_k=k_k, k_a=k_a, r_k=r_k, lnx_w=lnx_w, lnx_b=lnx_b,
             w_out=w_out, router_w=router_w, router_b=router_b, moe_w1=moe_w1, moe_b1=moe_b1,
             moe_w2=moe_w2, moe_b2=moe_b2, final_g=final_g)
    depth = w_ada.shape[0]
    B = x_prompt.shape[0]
    wkv0_prompt = jnp.zeros((depth, B, N_RWKV_HEADS, HEAD_DIM, HEAD_DIM), F32)
    shift0_prompt = jnp.zeros((depth, B, D_MODEL), x_prompt.dtype)
    y_p, k_p, v_p, wkv_p, shift_p = _trunk(x_prompt, c_prompt, P, wkv0_prompt, shift0_prompt, None, None)
    y_s, k_s, v_s, wkv_s, shift_s = _trunk(x_sample, c_sample, P, state_wkv, state_shift, state_attn_k, state_attn_v)
    return (y_p, y_s, k_p, v_p, wkv_p, shift_p, k_s, v_s, wkv_s, shift_s)
```

```python
import functools
import math

import jax
import jax.numpy as jnp
from jax import lax
from jax.experimental import pallas as pl
from jax.experimental.pallas import tpu as pltpu

F32 = jnp.float32
BF16 = jnp.bfloat16
HIGHEST = lax.Precision.HIGHEST

D_MODEL = 1024
HEAD_DIM = 64
N_ATT_HEADS = 8
N_RWKV_HEADS = 8
ATT_WIDTH = N_ATT_HEADS * HEAD_DIM
RWKV_WIDTH = N_RWKV_HEADS * HEAD_DIM
IN_COLS = 3 * ATT_WIDTH + 3 * RWKV_WIDTH
DILATIONS = (1, 4, 16)
ATT_BLOCK = 128
MAX_WINDOW = 2048
N_EXPERTS = 32
TOP_K = 4
D_EXPERT = 1024
SWIGLU_LIMIT = 7.0
SWIGLU_ALPHA = 1.702
NORM_EPS = 1e-5
LNX_EPS = 64e-5
NEG_BIG = -1e30
WKV_CHUNK = 64
LANES = 128
VMEM_LIMIT = 56 * 1024 * 1024
LORA_WIDTHS = (64, 64, 128, 32)
LORA_COLS = 384
N_RKV = 3 * RWKV_WIDTH
N_REST = N_RKV + 2 * LORA_COLS
ROW_TILE_SUBLANES = D_MODEL // LANES


def _params(n_grid):
    return pltpu.CompilerParams(dimension_semantics=("arbitrary",) * n_grid,
                                vmem_limit_bytes=VMEM_LIMIT)


def _dot(a, b, precision=None):
    return jnp.dot(a, b, preferred_element_type=F32, precision=precision)


def _dot_nt(a, b, precision=None):
    return lax.dot_general(a, b, (((1,), (1,)), ((), ())), preferred_element_type=F32, precision=precision)


def _dot_tn(a, b, precision=None):
    return lax.dot_general(a, b, (((0,), (0,)), ((), ())), preferred_element_type=F32, precision=precision)


def _split3(x):
    hi = x.astype(BF16)
    r1 = x - hi.astype(F32)
    mid = r1.astype(BF16)
    lo = (r1 - mid.astype(F32)).astype(BF16)
    return hi, mid, lo


def _head_sum_matrix():
    hr = lax.broadcasted_iota(jnp.int32, (LANES, LANES), 0) // HEAD_DIM
    hc = lax.broadcasted_iota(jnp.int32, (LANES, LANES), 1) // HEAD_DIM
    return (hr == hc).astype(F32)


def _rms_mod(x, sc, sh):
    return x * lax.rsqrt(jnp.mean(x * x, axis=-1, keepdims=True) + NORM_EPS) * (1.0 + sc) + sh


def _row_tile(T, S, tm):
    tm = min(tm, T)
    assert T % tm == 0 and (S == 1 or S % tm == 0)
    return tm


def _mod_operand(v, S, tm):
    B, N = v.shape
    if S == 1:
        return v.reshape(1, B, N), pl.BlockSpec((None, tm, N), lambda i: (0, i, 0))
    return v.reshape(B, 1, N), pl.BlockSpec((None, 1, N), lambda i: (i * tm // S, 0, 0))


def _mm_body(x_ref, w_ref, o_ref):
    o_ref[...] = _dot(x_ref[...].astype(BF16), w_ref[...])


def _mm_rows(x, w, *, tn=None):
    M, K = x.shape
    N = w.shape[1]
    Mp = -(-M // 8) * 8
    if Mp != M:
        x = jnp.pad(x, ((0, Mp - M), (0, 0)))
    tn = N if tn is None else tn
    assert N % tn == 0
    out = pl.pallas_call(
        _mm_body,
        grid=(N // tn,),
        in_specs=[pl.BlockSpec((Mp, K), lambda j: (0, 0)),
                  pl.BlockSpec((K, tn), lambda j: (0, j))],
        out_specs=pl.BlockSpec((Mp, tn), lambda j: (0, j)),
        out_shape=jax.ShapeDtypeStruct((Mp, N), F32),
        compiler_params=_params(1),
        name="mm_rows",
    )(x, w)
    return out[:M]


def _pre_proj_body(x_ref, sc_ref, sh_ref, w_ref, qkv_ref, kv_ref, rest_ref):
    h = _rms_mod(x_ref[...], sc_ref[...], sh_ref[...])
    acc = _dot(h.astype(BF16), w_ref[...])
    n_slabs = qkv_ref.shape[0]
    for i in range(n_slabs):
        qkv_ref[i] = acc[:, i * LANES:(i + 1) * LANES]
    tm = acc.shape[0]
    for h in range(N_ATT_HEADS):
        k_h = acc[:, ATT_WIDTH + h * HEAD_DIM:ATT_WIDTH + (h + 1) * HEAD_DIM]
        v_h = acc[:, 2 * ATT_WIDTH + h * HEAD_DIM:2 * ATT_WIDTH + (h + 1) * HEAD_DIM]
        kv_ref[pl.ds(h, tm, stride=N_ATT_HEADS), :] = jnp.concatenate([k_h, v_h], axis=1)
    rest_ref[...] = acc[:, n_slabs * LANES:]


def _pre_proj(x, sc, sh, w, S, *, tm=256):
    T, K = x.shape
    N = w.shape[1]
    tm = _row_tile(T, S, tm)
    n_slabs = 3 * ATT_WIDTH // LANES
    n_rest = N - 3 * ATT_WIDTH
    sc_op, sc_spec = _mod_operand(sc, S, tm)
    sh_op, sh_spec = _mod_operand(sh, S, tm)
    return pl.pallas_call(
        _pre_proj_body,
        grid=(T // tm,),
        in_specs=[pl.BlockSpec((tm, K), lambda i: (i, 0)), sc_spec, sh_spec,
                  pl.BlockSpec((K, N), lambda i: (0, 0))],
        out_specs=[pl.BlockSpec((n_slabs, tm, LANES), lambda i: (0, i, 0)),
                   pl.BlockSpec((tm * N_ATT_HEADS, LANES), lambda i: (i, 0)),
                   pl.BlockSpec((tm, n_rest), lambda i: (i, 0))],
        out_shape=[jax.ShapeDtypeStruct((n_slabs, T, LANES), F32), jax.ShapeDtypeStruct((T * N_ATT_HEADS, LANES), F32),
                   jax.ShapeDtypeStruct((T, n_rest), F32)],
        compiler_params=_params(1),
        name="pre_proj",
    )(x, sc_op, sh_op, w)


def _att_prompt_body(qkv_ref, g_ref, o_ref, m_ref, l_ref, *, seq):
    scale = HEAD_DIM ** -0.5
    qi = lax.broadcasted_iota(jnp.int32, (ATT_BLOCK, ATT_BLOCK), 0)
    kj = lax.broadcasted_iota(jnp.int32, (ATT_BLOCK, ATT_BLOCK), 1)
    mask_cur = kj <= qi
    lane = lax.broadcasted_iota(jnp.int32, (ATT_BLOCK, LANES), 1)
    first_head = lane < HEAD_DIM
    n_pairs = ATT_WIDTH // LANES

    for branch, dil in enumerate(DILATIONS):
        nb = seq // (ATT_BLOCK * dil)

        def block(bi, carry, dil=dil, nb=nb, branch=branch):
            r = bi // nb
            j = bi % nb
            jp = jnp.maximum(j - 1, 0)
            if dil > 1:
                rows = pl.ds(r + dil * ATT_BLOCK * j, ATT_BLOCK, stride=dil)
                prows = pl.ds(r + dil * ATT_BLOCK * jp, ATT_BLOCK, stride=dil)
            else:
                rows = pl.ds(pl.multiple_of(ATT_BLOCK * j, ATT_BLOCK), ATT_BLOCK)
                prows = pl.ds(pl.multiple_of(ATT_BLOCK * jp, ATT_BLOCK), ATT_BLOCK)
            mp = kj >= qi + jnp.where(j > 0, 0, ATT_BLOCK)
            if branch > 0:
                m_old = m_ref[rows, :]
                l_old = l_ref[rows, :]
            heads = [(p, hh) for p in range(n_pairs) for hh in range(2)]
            sel = lambda hh: first_head if hh == 0 else jnp.logical_not(first_head)
            k1 = [qkv_ref[n_pairs + p, prows, :].astype(BF16) for p in range(n_pairs)]
            k2 = [qkv_ref[n_pairs + p, rows, :].astype(BF16) for p in range(n_pairs)]
            qh = []
            for p in range(n_pairs):
                q = qkv_ref[p, rows, :] * scale
                qh += [jnp.where(sel(hh), q, 0.0).astype(BF16) for hh in range(2)]
            s1 = [_dot_nt(qh[i], k1[p]) for i, (p, hh) in enumerate(heads)]
            s2 = [_dot_nt(qh[i], k2[p]) for i, (p, hh) in enumerate(heads)]
            p1, p2, mbs = [], [], []
            for i in range(len(heads)):
                a1 = jnp.where(mp, s1[i], NEG_BIG)
                a2 = jnp.where(mask_cur, s2[i], NEG_BIG)
                mb = jnp.max(jnp.maximum(a1, a2), axis=1, keepdims=True)
                p1.append(jnp.exp(a1 - mb).astype(BF16))
                p2.append(jnp.exp(a2 - mb).astype(BF16))
                mbs.append(mb)
            ones = jnp.ones((ATT_BLOCK, LANES), BF16)
            obs, lbs = [], []
            for p in range(n_pairs):
                v1 = qkv_ref[2 * n_pairs + p, prows, :]
                v2 = qkv_ref[2 * n_pairs + p, rows, :]
                for hh in range(2):
                    i = 2 * p + hh
                    v1h = jnp.where(sel(hh), v1, 0.0).astype(BF16)
                    v2h = jnp.where(sel(hh), v2, 0.0).astype(BF16)
                    obs.append(_dot(p1[i], v1h) + _dot(p2[i], v2h))
                    lbs.append(_dot(p1[i], ones) + _dot(p2[i], ones))
            m_new = jnp.zeros((ATT_BLOCK, LANES), F32)
            l_new = jnp.zeros((ATT_BLOCK, LANES), F32)
            for p in range(n_pairs):
                out = jnp.zeros((ATT_BLOCK, LANES), F32)
                alpha = jnp.zeros((ATT_BLOCK, LANES), F32)
                for hh in range(2):
                    h = 2 * p + hh
                    mb, lb, ob = mbs[h], lbs[h], obs[h]
                    if branch > 0:
                        mo = m_old[:, h:h + 1]
                        lo = l_old[:, h:h + 1]
                        mn = jnp.maximum(mo, mb)
                        a_old = jnp.exp(mo - mn)
                        a_new = jnp.exp(mb - mn)
                        lb = a_old * lo + a_new * lb
                        ob = ob * a_new
                        alpha = jnp.where(sel(hh), a_old, alpha)
                        mb = mn
                    out = out + ob
                    m_new = jnp.where(lane == h, mb, m_new)
                    l_new = jnp.where(lane == h, lb, l_new)
                if branch > 0:
                    out = out + alpha * o_ref[p, rows, :]
                o_ref[p, rows, :] = out
            m_ref[rows, :] = m_new
            l_ref[rows, :] = l_new
            return carry

        lax.fori_loop(0, seq // ATT_BLOCK, block, 0)

    head_sum = _head_sum_matrix()

    def finish(bi, carry):
        rows = pl.ds(pl.multiple_of(bi * ATT_BLOCK, ATT_BLOCK), ATT_BLOCK)
        l_all = l_ref[rows, :]
        for p in range(n_pairs):
            den = jnp.where(first_head, l_all[:, 2 * p:2 * p + 1], l_all[:, 2 * p + 1:2 * p + 2])
            att = o_ref[p, rows, :] / den
            ms = _dot(att * att, head_sum, HIGHEST) * (1.0 / HEAD_DIM)
            o_ref[p, rows, :] = att * lax.rsqrt(ms + NORM_EPS) * g_ref[:, p * LANES:(p + 1) * LANES]
        return carry

    lax.fori_loop(0, seq // ATT_BLOCK, finish, 0)


def _att_prompt(qkv, B, gain):
    n3, T, _ = qkv.shape
    S = T // B
    n_pairs = n3 // 3
    assert S % (ATT_BLOCK * DILATIONS[-1]) == 0
    return pl.pallas_call(
        functools.partial(_att_prompt_body, seq=S),
        grid=(B,),
        in_specs=[pl.BlockSpec((n3, S, LANES), lambda b: (0, b, 0), pipeline_mode=pl.Buffered(1)),
                  pl.BlockSpec((1, ATT_WIDTH), lambda b: (0, 0))],
        out_specs=pl.BlockSpec((n_pairs, S, LANES), lambda b: (0, b, 0)),
        out_shape=jax.ShapeDtypeStruct((n_pairs, T, LANES), F32),
        scratch_shapes=[pltpu.VMEM((S, LANES), F32), pltpu.VMEM((S, LANES), F32)],
        compiler_params=_params(1),
        name="att_prompt",
    )(qkv, gain.reshape(1, ATT_WIDTH))


def _att_decode_body(q_ref, kn_ref, vn_ref, *refs):
    nd = len(DILATIONS)
    kb_refs, vb_refs, (g_ref, o_ref) = refs[:nd], refs[nd:2 * nd], refs[2 * nd:]
    q = q_ref[...] * HEAD_DIM ** -0.5
    kn = kn_ref[...]
    vn = vn_ref[...]
    s_new = jnp.sum(q * kn, axis=-1, keepdims=True)
    parts = []
    for kb_ref, vb_ref in zip(kb_refs, vb_refs):
        s = jnp.sum(kb_ref[...] * q[None], axis=-1, keepdims=True)
        m = jnp.maximum(jnp.max(s, axis=0), s_new)
        p = jnp.exp(s - m[None])
        p_new = jnp.exp(s_new - m)
        den = jnp.sum(p, axis=0) + p_new
        num = jnp.sum(p * vb_ref[...], axis=0) + p_new * vn
        parts.append((num, den, m))
    m_all = jnp.maximum(jnp.maximum(parts[0][2], parts[1][2]), parts[2][2])
    num = jnp.zeros_like(vn)
    den = jnp.zeros_like(s_new)
    for n_b, d_b, m_b in parts:
        w = jnp.exp(m_b - m_all)
        num = num + n_b * w
        den = den + d_b * w
    att = num / den
    ms = jnp.mean(att * att, axis=-1, keepdims=True)
    o_ref[...] = att * lax.rsqrt(ms + NORM_EPS) * g_ref[...]


def _att_decode(q, k_new, v_new, k_buf, v_buf, layer, gain):
    depth, B, n_buf, H, E = k_buf.shape
    tile = pl.BlockSpec((None, H, E), lambda b: (b, 0, 0))
    views, specs = [], []
    for buf in (k_buf, v_buf):
        for dil in DILATIONS:
            assert n_buf % (ATT_BLOCK * dil) == 0
            views.append(buf.reshape(depth, B, n_buf // dil, dil, H, E))
            specs.append(pl.BlockSpec((None, None, ATT_BLOCK, None, H, E),
                                      lambda b, dil=dil: (layer, b, n_buf // (dil * ATT_BLOCK) - 1, 0, 0, 0)))
    return pl.pallas_call(
        _att_decode_body,
        grid=(B,),
        in_specs=[tile, tile, tile] + specs + [pl.BlockSpec((H, E), lambda b: (0, 0))],
        out_specs=tile,
        out_shape=jax.ShapeDtypeStruct((B, H, E), F32),
        compiler_params=_params(1),
        name="att_decode",
    )(q, k_new, v_new, *views, gain.reshape(H, E))


def _rwkv_prep_body(*refs, S, tm, has_vfirst):
    if has_vfirst:
        cur_ref, prev_ref, mu_ref, vec_ref, w2_ref, vf_ref = refs[:6]
        rest = refs[6:]
    else:
        cur_ref, prev_ref, mu_ref, vec_ref, w2_ref = refs[:5]
        vf_ref, rest = None, refs[5:]
    r_ref, lw_ref, k_ref, v_ref, a_ref, b_ref, g_ref, carry = rest
    c_lp = N_RKV + LORA_COLS
    rkv = cur_ref[:, :N_RKV]
    lp = cur_ref[:, c_lp:]
    if S == 1:
        rkv_prev = prev_ref[:, :N_RKV]
        lp_prev = prev_ref[:, c_lp:]
    else:
        i = pl.program_id(0)

        @pl.when(i == 0)
        def _():
            carry[...] = jnp.zeros_like(carry)

        first = (i % (S // tm)) == 0
        prev_row = jnp.where(first, prev_ref[...], carry[...])
        row0 = lax.broadcasted_iota(jnp.int32, (tm, 1), 0) == 0
        rkv_prev = jnp.where(row0, prev_row[:, :N_RKV], pltpu.roll(rkv, 1, 0))
        lp_prev = jnp.where(row0, prev_row[:, c_lp:], pltpu.roll(lp, 1, 0))
        carry[...] = cur_ref[tm - 1:tm, :]
    rkv = rkv + (rkv_prev - rkv) * mu_ref[...]
    r = rkv[:, :RWKV_WIDTH]
    kr = rkv[:, RWKV_WIDTH:2 * RWKV_WIDTH]
    vr = rkv[:, 2 * RWKV_WIDTH:]
    l1 = cur_ref[:, N_RKV:c_lp] + lp_prev
    t0 = l1[:, :LANES]
    lane = lax.broadcasted_iota(jnp.int32, (tm, LANES), 1)
    act = jnp.concatenate([jnp.where(lane < LORA_WIDTHS[0], jnp.tanh(t0), t0),
                           jax.nn.sigmoid(l1[:, LANES:2 * LANES]),
                           l1[:, 2 * LANES:]], axis=1)
    l2 = _dot(act.astype(BF16), w2_ref[...])
    w0, a0, v0, k_k, k_a = (vec_ref[j:j + 1, :] for j in range(5))
    z = w0 + l2[:, :RWKV_WIDTH]
    softplus_neg = jnp.maximum(-z, 0.0) + jnp.log(1.0 + jnp.exp(-jnp.abs(z)))
    lw_ref[...] = -jnp.exp(-softplus_neg - 0.5)
    a = jax.nn.sigmoid(a0 + l2[:, RWKV_WIDTH:2 * RWKV_WIDTH])
    g_ref[...] = l2[:, 2 * RWKV_WIDTH:3 * RWKV_WIDTH]
    if has_vfirst:
        vr = vr + (vf_ref[...] - vr) * jax.nn.sigmoid(v0 + l2[:, 3 * RWKV_WIDTH:])
    kk = kr * k_k
    head_sum = _head_sum_matrix()
    ss = jnp.concatenate([_dot(kk[:, p * LANES:(p + 1) * LANES] ** 2, head_sum, HIGHEST)
                          for p in range(RWKV_WIDTH // LANES)], axis=1)
    kk = kk / jnp.maximum(jnp.sqrt(ss), 1e-12)
    r_ref[...] = r
    k_ref[...] = kr * (1.0 + (a - 1.0) * k_a)
    v_ref[...] = vr
    a_ref[...] = -kk
    b_ref[...] = kk * a


def _rwkv_prep(rest, rest_prev, mu_rkv, vecs, w2, v_first, S, *, tm=256):
    T = rest.shape[0]
    tm = _row_tile(T, S, tm)
    prev_op, prev_spec = _mod_operand(rest_prev, S, tm)
    tile = pl.BlockSpec((tm, RWKV_WIDTH), lambda i: (i, 0))
    const = lambda a: pl.BlockSpec(a.shape, lambda i: (0,) * a.ndim)
    ops = [rest, prev_op, mu_rkv, vecs, w2]
    specs = [pl.BlockSpec((tm, N_REST), lambda i: (i, 0)), prev_spec, const(mu_rkv), const(vecs), const(w2)]
    if v_first is not None:
        ops.append(v_first)
        specs.append(tile)
    return pl.pallas_call(
        functools.partial(_rwkv_prep_body, S=S, tm=tm, has_vfirst=v_first is not None),
        grid=(T // tm,),
        in_specs=specs,
        out_specs=[tile] * 7,
        out_shape=[jax.ShapeDtypeStruct((T, RWKV_WIDTH), F32)] * 7,
        scratch_shapes=[pltpu.VMEM((1, N_REST), F32)],
        compiler_params=_params(1),
        name="rwkv_prep",
    )(*ops)


def _wkv_body(r_ref, lw_ref, k_ref, v_ref, a_ref, b_ref, s0_ref, y_ref, sn_ref, s_scr, *, n_chunks):
    C = WKV_CHUNK
    C2 = 2 * C
    n_pairs = RWKV_WIDTH // LANES
    t = pl.program_id(1)

    @pl.when(t == 0)
    def _():
        s_scr[...] = s0_ref[...]

    row = lax.broadcasted_iota(jnp.int32, (C2, C2), 0)
    col = lax.broadcasted_iota(jnp.int32, (C2, C2), 1)
    same_head = (row // C) == (col // C)
    lower_strict = same_head & (col < row)
    lower_incl = same_head & (col <= row)
    eye = (row == col).astype(F32)
    tri = (lax.broadcasted_iota(jnp.int32, (C, C), 1) <= lax.broadcasted_iota(jnp.int32, (C, C), 0)).astype(BF16)
    first_head = lax.broadcasted_iota(jnp.int32, (C, LANES), 1) < HEAD_DIM
    block_diag = _head_sum_matrix() > 0.5
    ones = jnp.ones((C, LANES), BF16)
    bf = lambda x: x.astype(BF16)

    def split(x):
        return jnp.concatenate([jnp.where(first_head, x, 0.0), jnp.where(first_head, 0.0, x)], axis=0)

    items = [(c, p) for c in range(n_chunks) for p in range(n_pairs)]
    ld = lambda ref, c, p: ref[c * C:(c + 1) * C, p * LANES:(p + 1) * LANES]
    each = lambda f, *ls: [f(*xs) for xs in zip(*ls)]
    r, lw, k, v, a, b = ([ld(ref, c, p) for c, p in items] for ref in (r_ref, lw_ref, k_ref, v_ref, a_ref, b_ref))

    lw3 = each(_split3, lw)
    cum = each(lambda t3: sum(_dot(tri, x) for x in t3), lw3)
    x_all = each(lambda a_, r_, lw_, c_: bf(jnp.concatenate([split(a_ * jnp.exp(c_ - lw_)), split(r_ * jnp.exp(c_))],
                                                            axis=0)), a, r, lw, cum)

    def keys(b_, k_, c_):
        g_inv = jnp.exp(-c_)
        bt, kt = b_ * g_inv, k_ * g_inv
        return bf(jnp.concatenate([bt, bt, kt, kt], axis=0))

    y_all = each(keys, b, k, cum)
    gram = each(_dot_nt, x_all, y_all)
    l_pow = each(lambda g_: jnp.where(lower_strict, g_[0:C2, 0:C2], 0.0), gram)
    m_ak = each(lambda g_: bf(jnp.where(lower_strict, g_[0:C2, C2:], 0.0)), gram)
    m_rb = each(lambda g_: bf(jnp.where(lower_incl, g_[C2:, 0:C2], 0.0)), gram)
    m_rk = each(lambda g_: bf(jnp.where(lower_incl, g_[C2:, C2:], 0.0)), gram)
    v_st = each(lambda v_: bf(split(v_)), v)
    t_inv = each(lambda l_: eye + l_, l_pow)
    for _ in range(int(math.log2(C)) - 1):
        l_pow = each(lambda l_: _dot(bf(l_), bf(l_)), l_pow)
        t_inv = each(lambda t_, l_: t_ + _dot(bf(t_), bf(l_)), t_inv, l_pow)
    t_inv = each(bf, t_inv)
    akv = each(_dot, m_ak, v_st)
    rkv = each(_dot, m_rk, v_st)
    tail = each(lambda c_: jnp.exp(c_[C - 1:C, :] - c_), cum)
    b_end = each(lambda b_, t_: bf(b_ * t_), b, tail)
    kv_end = each(lambda k_, t_, v_: _dot_tn(bf(k_ * t_), bf(v_)), k, tail, v)
    g_col = each(lambda t3: jnp.exp(sum(_dot_tn(x, ones) for x in t3)), lw3)

    state = [s_scr[p] for p in range(n_pairs)]
    for c in range(n_chunks):
        sel = lambda ls: ls[c * n_pairs:(c + 1) * n_pairs]
        xs = each(lambda x_, s_: _dot(x_, bf(s_)), sel(x_all), state)
        u_st = each(lambda t_, xs_, akv_: _dot(t_, bf(xs_[0:C2] + akv_)), sel(t_inv), xs, sel(akv))
        y_st = each(lambda xs_, m_, u_, rkv_: xs_[C2:] + _dot(m_, bf(u_)) + rkv_, xs, sel(m_rb), u_st, sel(rkv))
        upd = each(lambda b_, u_: _dot_tn(b_, bf(u_[0:C] + u_[C:])), sel(b_end), u_st)
        for p in range(n_pairs):
            y_ref[c * C:(c + 1) * C, p * LANES:(p + 1) * LANES] = y_st[p][0:C] + y_st[p][C:]
        state = each(lambda g_, s_, u_, kv_: g_ * s_ + jnp.where(block_diag, u_ + kv_, 0.0),
                     sel(g_col), state, upd, sel(kv_end))
    for p in range(n_pairs):
        s_scr[p] = state[p]

    @pl.when(t == pl.num_programs(1) - 1)
    def _():
        sn_ref[...] = s_scr[...]


def _wkv(r, lw, k, v, a, b, s0, *, chunks_per_step=4):
    B, T, W = r.shape
    H = N_RWKV_HEADS
    n_pairs = W // LANES
    tt = min(T, WKV_CHUNK * chunks_per_step)
    assert T % tt == 0 and tt % WKV_CHUNK == 0
    s0t = jnp.swapaxes(s0, -1, -2).reshape(B, n_pairs, 2, HEAD_DIM, HEAD_DIM)
    z = jnp.zeros_like(s0t[:, :, 0])
    s0bd = jnp.concatenate([jnp.concatenate([s0t[:, :, 0], z], axis=-1),
                            jnp.concatenate([z, s0t[:, :, 1]], axis=-1)], axis=-2)
    seq = pl.BlockSpec((None, tt, W), lambda bi, ti: (bi, ti, 0))
    st = pl.BlockSpec((None, n_pairs, LANES, LANES), lambda bi, ti: (bi, 0, 0, 0))
    y, sbd = pl.pallas_call(
        functools.partial(_wkv_body, n_chunks=tt // WKV_CHUNK),
        grid=(B, T // tt),
        in_specs=[seq] * 6 + [st],
        out_specs=[seq, st],
        out_shape=[jax.ShapeDtypeStruct((B, T, W), F32), jax.ShapeDtypeStruct((B, n_pairs, LANES, LANES), F32)],
        scratch_shapes=[pltpu.VMEM((n_pairs, LANES, LANES), F32)],
        compiler_params=_params(2),
        name="wkv",
    )(r, lw, k, v, a, b, s0bd)
    sn = jnp.stack([sbd[:, :, :HEAD_DIM, :HEAD_DIM], sbd[:, :, HEAD_DIM:, HEAD_DIM:]], axis=2)
    return y, jnp.swapaxes(sn.reshape(B, H, HEAD_DIM, HEAD_DIM), -1, -2)


def _wkv_step_body(r_ref, lw_ref, k_ref, v_ref, a_ref, b_ref, s_ref, y_ref, sn_ref):
    row0 = lax.broadcasted_iota(jnp.int32, (8, HEAD_DIM), 0) == 0
    ys = []
    for h in range(N_RWKV_HEADS):
        cols = slice(h * HEAD_DIM, (h + 1) * HEAD_DIM)
        r8, k8, v8, a8, b8 = (jnp.where(row0, ref[:, cols], 0.0) for ref in (r_ref, k_ref, v_ref, a_ref, b_ref))
        s = s_ref[h]
        s_new = (s * jnp.exp(lw_ref[:, cols]) + _dot(s, _dot_tn(a8, b8, HIGHEST), HIGHEST)
                 + _dot_tn(v8, k8, HIGHEST))
        sn_ref[h] = s_new
        ys.append(_dot_nt(r8, s_new, HIGHEST)[0:1])
    y_ref[...] = jnp.concatenate(ys, axis=1)


def _wkv_step(r, lw, k, v, a, b, states, layer):
    B, W = r.shape
    H = N_RWKV_HEADS
    row = pl.BlockSpec((None, 1, W), lambda i: (i, 0, 0))
    vec = lambda t: t.reshape(B, 1, W)
    y, sn = pl.pallas_call(
        _wkv_step_body,
        grid=(B,),
        in_specs=[row] * 6 + [pl.BlockSpec((None, None, H, HEAD_DIM, HEAD_DIM), lambda i: (layer, i, 0, 0, 0))],
        out_specs=[row, pl.BlockSpec((None, H, HEAD_DIM, HEAD_DIM), lambda i: (i, 0, 0, 0))],
        out_shape=[jax.ShapeDtypeStruct((B, 1, W), F32), jax.ShapeDtypeStruct((B, H, HEAD_DIM, HEAD_DIM), F32)],
        compiler_params=_params(1),
        name="wkv_step",
    )(vec(r), vec(lw), vec(k), vec(v), vec(a), vec(b), states)
    return y.reshape(B, W), sn


def _out_proj_body(att_ref, y_ref, r_ref, k_ref, v_ref, g_ref, x_ref, gt_ref, sc_ref, sh_ref,
                   vec_ref, wo_ref, rw_ref, rb_ref, x1_ref, h2_ref, logit_ref):
    head_sum = _head_sum_matrix().astype(BF16)
    hsum = lambda t: sum(_dot(part, head_sum) for part in _split3(t))
    n_pairs = RWKV_WIDTH // LANES
    parts = [att_ref[p].astype(BF16) for p in range(ATT_WIDTH // LANES)]
    for p in range(n_pairs):
        cols = slice(p * LANES, (p + 1) * LANES)
        y = y_ref[:, cols]
        d = y - hsum(y) * (1.0 / HEAD_DIM)
        var = hsum(d * d) * (1.0 / HEAD_DIM)
        yn = d * lax.rsqrt(var + LNX_EPS) * vec_ref[0:1, cols] + vec_ref[1:2, cols]
        rk = r_ref[:, cols] * k_ref[:, cols] * vec_ref[2:3, cols]
        bonus = hsum(rk) * v_ref[:, cols]
        parts.append(((yn + bonus) * g_ref[:, cols]).astype(BF16))
    mix = _dot(jnp.concatenate(parts, axis=1), wo_ref[...])
    x1 = x_ref[...] + gt_ref[...] * mix
    x1_ref[...] = x1
    h2 = _rms_mod(x1, sc_ref[...], sh_ref[...])
    tm = h2.shape[0]
    for j in range(ROW_TILE_SUBLANES):
        h2_ref[pl.ds(j, tm, stride=ROW_TILE_SUBLANES), :] = h2[:, j * LANES:(j + 1) * LANES]
    logit_ref[...] = _dot(h2, rw_ref[...], HIGHEST) + rb_ref[...]


def _out_proj(att, y, r, k, v, g, x, gt, sc, sh, vecs, w_out, rw, rb, S, *, tm=256):
    T = x.shape[0]
    tm = _row_tile(T, S, tm)
    n_pairs = att.shape[0]
    tile = lambda n: pl.BlockSpec((tm, n), lambda i: (i, 0))
    const = lambda a: pl.BlockSpec(a.shape, lambda i: (0,) * a.ndim)
    mods = [_mod_operand(m, S, tm) for m in (gt, sc, sh)]
    return pl.pallas_call(
        _out_proj_body,
        grid=(T // tm,),
        in_specs=[pl.BlockSpec((n_pairs, tm, LANES), lambda i: (0, i, 0))] + [tile(RWKV_WIDTH)] * 5 + [tile(D_MODEL)]
                 + [m[1] for m in mods] + [const(vecs), const(w_out), const(rw), const(rb)],
        out_specs=[tile(D_MODEL), pl.BlockSpec((tm * ROW_TILE_SUBLANES, LANES), lambda i: (i, 0)), tile(LANES)],
        out_shape=[jax.ShapeDtypeStruct((T, D_MODEL), F32), jax.ShapeDtypeStruct((T * ROW_TILE_SUBLANES, LANES), F32),
                   jax.ShapeDtypeStruct((T, LANES), F32)],
        compiler_params=_params(1),
        name="out_proj",
    )(att, y, r, k, v, g, x, *[m[0] for m in mods], vecs, w_out, rw, rb)


def _moe_body(ge_ref, gidx_ref, sidx_ref, x_hbm, w1_ref, b1_ref, w2_ref, b2_ref, out_hbm,
              xb0, xb1, ob0, ob1, w1b, w2b, gsem, ssem, *, G, ng):
    s = pl.program_id(0)

    def step(cur, x_cur, x_oth, o_cur, o_oth):
        oth = 1 - cur

        tile_rows = lambda i: pl.ds(i * ROW_TILE_SUBLANES, ROW_TILE_SUBLANES)

        hbm_rows = lambda first: pl.ds(pl.multiple_of(first, ROW_TILE_SUBLANES), ROW_TILE_SUBLANES)

        def start_gather():
            for i in range(G):
                pltpu.make_async_copy(x_hbm.at[hbm_rows(gidx_ref[0, 0, i])], x_cur.at[tile_rows(i)],
                                      gsem.at[cur]).start(priority=i % 2)

        def start_scatter():
            for i in range(G):
                pltpu.make_async_copy(o_cur.at[tile_rows(i)], out_hbm.at[hbm_rows(sidx_ref[0, 0, i])],
                                      ssem.at[cur]).start(priority=i % 2)

        def wait_rows(buf, sem):
            pltpu.make_async_copy(buf, buf, sem).wait()

        def evaluate():
            col = lambda j: pl.ds(j, G, stride=ROW_TILE_SUBLANES)
            x = jnp.concatenate([x_oth[col(j), :] for j in range(ROW_TILE_SUBLANES)], axis=1).astype(BF16)
            u = _dot(x, w1b[...]) + b1_ref[...]
            glu = jnp.minimum(u[:, :D_EXPERT], SWIGLU_LIMIT)
            lin = jnp.clip(u[:, D_EXPERT:], -SWIGLU_LIMIT, SWIGLU_LIMIT)
            act = glu * jax.nn.sigmoid(SWIGLU_ALPHA * glu) * (lin + 1.0)
            y = _dot(act.astype(BF16), w2b[...]) + b2_ref[...]
            for j in range(ROW_TILE_SUBLANES):
                o_oth[col(j), :] = y[:, j * LANES:(j + 1) * LANES]

        @pl.when((s >= 1) & (s <= ng))
        def _():
            wait_rows(x_oth, gsem.at[oth])
            e = jnp.clip(s - 1, 0, ng - 1)

            @pl.when((s == 1) | (ge_ref[e] != ge_ref[jnp.maximum(e - 1, 0)]))
            def _():
                w1b[...] = w1_ref[...].astype(BF16)
                w2b[...] = w2_ref[...].astype(BF16)

        @pl.when(s >= 3)
        def _():
            wait_rows(o_oth, ssem.at[oth])

        @pl.when(s == 0)
        def _():
            start_gather()

        @pl.when(s == 1)
        def _():
            start_gather()
            evaluate()

        @pl.when((s >= 2) & (s < ng))
        def _():
            start_gather()
            start_scatter()
            evaluate()

        @pl.when(s == ng)
        def _():
            start_scatter()
            evaluate()

        @pl.when(s == ng + 1)
        def _():
            start_scatter()
            wait_rows(o_cur, ssem.at[cur])

    @pl.when(s % 2 == 0)
    def _():
        step(0, xb0, xb1, ob0, ob1)

    @pl.when(s % 2 == 1)
    def _():
        step(1, xb1, xb0, ob1, ob0)


def _moe_experts(x, gather_idx, scatter_idx, group_e, w1, b1, w2, b2, layer, *, G):
    T = x.shape[0] // ROW_TILE_SUBLANES
    ng = group_e.shape[0]
    assert ng >= 2
    depth = w1.shape[0]
    gidx = (gather_idx * ROW_TILE_SUBLANES).reshape(ng, 1, G)
    sidx = (scatter_idx * ROW_TILE_SUBLANES).reshape(ng, 1, G)
    smem_blk = lambda f: pl.BlockSpec((1, 1, G), f, memory_space=pltpu.SMEM)
    expert = lambda s, ge: (layer, ge[jnp.clip(s - 1, 0, ng - 1)], 0, 0)
    row_tile = (G * ROW_TILE_SUBLANES, LANES)
    grid_spec = pltpu.PrefetchScalarGridSpec(
        num_scalar_prefetch=1,
        grid=(ng + 2,),
        in_specs=[
            smem_blk(lambda s, ge: (jnp.minimum(s, ng - 1), 0, 0)),
            smem_blk(lambda s, ge: (jnp.clip(s - 2, 0, ng - 1), 0, 0)),
            pl.BlockSpec(memory_space=pl.ANY),
            pl.BlockSpec((None, None, D_MODEL, 2 * D_EXPERT), expert),
            pl.BlockSpec((None, None, 1, 2 * D_EXPERT), expert),
            pl.BlockSpec((None, None, D_EXPERT, D_MODEL), expert),
            pl.BlockSpec((None, None, 1, D_MODEL), expert),
        ],
        out_specs=pl.BlockSpec(memory_space=pl.ANY),
        scratch_shapes=[
            pltpu.VMEM(row_tile, F32),
            pltpu.VMEM(row_tile, F32),
            pltpu.VMEM(row_tile, F32),
            pltpu.VMEM(row_tile, F32),
            pltpu.VMEM((D_MODEL, 2 * D_EXPERT), BF16),
            pltpu.VMEM((D_EXPERT, D_MODEL), BF16),
            pltpu.SemaphoreType.DMA((2,)),
            pltpu.SemaphoreType.DMA((2,)),
        ],
    )
    return pl.pallas_call(
        functools.partial(_moe_body, G=G, ng=ng),
        grid_spec=grid_spec,
        out_shape=jax.ShapeDtypeStruct(((T * TOP_K + G) * ROW_TILE_SUBLANES, LANES), F32),
        compiler_params=_params(1),
        name="moe_experts",
    )(group_e, gidx, sidx, x, w1, b1.reshape(depth, N_EXPERTS, 1, -1), w2, b2.reshape(depth, N_EXPERTS, 1, -1))


def _combine_body(*refs, final):
    outs = refs[:TOP_K]
    w_ref, x_ref, gt_ref = refs[TOP_K:TOP_K + 3]
    w = w_ref[...]
    tm = w.shape[0]
    rows = lambda ref: jnp.concatenate([ref[pl.ds(j, tm, stride=ROW_TILE_SUBLANES), :]
                                        for j in range(ROW_TILE_SUBLANES)], axis=1)
    acc = rows(outs[0]) * w[:, 0:1]
    for kk in range(1, TOP_K):
        acc = acc + rows(outs[kk]) * w[:, kk:kk + 1]
    x2 = x_ref[...] + gt_ref[...] * acc
    if final:
        fg_ref, o_ref = refs[TOP_K + 3:]
        o_ref[...] = x2 * lax.rsqrt(jnp.mean(x2 * x2, axis=-1, keepdims=True) + NORM_EPS) * fg_ref[...]
    else:
        refs[TOP_K + 3][...] = x2


def _combine(moe_out, weights, x, gt, final_g, S, *, tm=512):
    T = x.shape[0]
    tm = _row_tile(T, S, tm)
    gt_op, gt_spec = _mod_operand(gt, S, tm)
    tile = pl.BlockSpec((tm, D_MODEL), lambda i: (i, 0))
    ops = [moe_out] * TOP_K + [weights, x, gt_op]
    specs = [pl.BlockSpec((tm * ROW_TILE_SUBLANES, LANES), lambda i, kk=kk: (kk * (T // tm) + i, 0))
             for kk in range(TOP_K)]
    specs += [pl.BlockSpec((tm, TOP_K), lambda i: (i, 0)), tile, gt_spec]
    if final_g is not None:
        ops.append(final_g.reshape(1, D_MODEL))
        specs.append(pl.BlockSpec((1, D_MODEL), lambda i: (0, 0)))
    return pl.pallas_call(
        functools.partial(_combine_body, final=final_g is not None),
        grid=(T // tm,),
        in_specs=specs,
        out_specs=tile,
        out_shape=jax.ShapeDtypeStruct((T, D_MODEL), F32),
        compiler_params=_params(1),
        name="combine",
    )(*ops)


def _route(logits, T):
    top_val, top_idx = lax.top_k(logits, TOP_K)
    weights = jax.nn.softmax(top_val, axis=-1)
    A = T * TOP_K
    G = max(8, min(256, A // N_EXPERTS))
    ng = -(-A // G) + N_EXPERTS
    flat_e = top_idx.reshape(A).astype(jnp.int32)
    order = jnp.argsort(flat_e).astype(jnp.int32)
    experts = jnp.arange(N_EXPERTS, dtype=jnp.int32)
    counts = jnp.sum(flat_e[:, None] == experts[None, :], axis=0, dtype=jnp.int32)
    padded = (counts + G - 1) // G * G
    pad_end = jnp.cumsum(padded)
    pad_start = pad_end - padded
    start = jnp.cumsum(counts) - counts
    g0 = jnp.arange(ng, dtype=jnp.int32) * G
    group_e = jnp.minimum(jnp.sum(pad_end[None, :] <= g0[:, None], axis=1, dtype=jnp.int32), N_EXPERTS - 1)
    slot = jnp.arange(ng * G, dtype=jnp.int32)
    e = jnp.repeat(group_e, G)
    q = slot - pad_start[e]
    valid = q < counts[e]
    asg = order[jnp.clip(start[e] + q, 0, A - 1)]
    gather_idx = jnp.where(valid, asg // TOP_K, 0)
    scatter_idx = jnp.where(valid, (asg % TOP_K) * T + asg // TOP_K, A + slot % G)
    return weights, gather_idx, scatter_idx, group_e, G


def _layer_weights(l, P):
    mu_w, mu_a, mu_g = P["mu_wag"][l]
    firsts = [(mu_w, P["decay_w1"][l]), (mu_a, P["iclr_a1"][l]), (mu_g, P["gate_g1"][l])]
    seconds = [P["decay_w2"][l], P["iclr_a2"][l], P["gate_g2"][l]]
    if l > 0:
        firsts.append((P["vres_mu"][l - 1], P["vres_v1"][l - 1]))
        seconds.append(P["vres_v2"][l - 1])
    n_used = sum(w.shape[1] for _, w in firsts)
    zpad = jnp.zeros((D_MODEL, LORA_COLS - n_used), F32)
    cur = jnp.concatenate([(1.0 - mu)[:, None] * w for mu, w in firsts] + [zpad], axis=1)
    prev = jnp.concatenate([mu[:, None] * w for mu, w in firsts] + [zpad], axis=1)
    w_big = jnp.concatenate([P["w_in"][l], cur, prev], axis=1).astype(BF16)
    w2 = jnp.zeros((LORA_COLS, 4 * RWKV_WIDTH), F32)
    r0 = 0
    for i, s in enumerate(seconds):
        w2 = w2.at[r0:r0 + s.shape[0], i * RWKV_WIDTH:(i + 1) * RWKV_WIDTH].set(s)
        r0 += s.shape[0]
    return w_big, w2.astype(BF16)


def _trunk(x, c, P, wkv0, h_prev, k_buf, v_buf):
    B, S, _ = x.shape
    T = B * S
    depth = P["w_ada"].shape[0]
    n_pairs = ATT_WIDTH // LANES
    unslab = lambda t: jnp.transpose(t.reshape(n_pairs, B, -1, LANES), (1, 2, 0, 3)).reshape(B, -1, ATT_WIDTH)
    cs = jax.nn.silu(c)
    x = x.reshape(T, D_MODEL)
    new_k, new_v, new_wkv, new_shift = [], [], [], []
    v_first = None
    for l in range(depth):
        mod = _mm_rows(cs, P["w_ada"][l].astype(BF16), tn=1024) + P["b_ada"][l]
        sh1, sc1, gt1, sh2, sc2, gt2 = jnp.split(mod, 6, axis=-1)
        w_big, w_lora2 = _layer_weights(l, P)
        qkv, kv_rows, rest = _pre_proj(x, sc1, sh1, w_big, S)
        rest_prev = _mm_rows(h_prev[l], w_big[:, 3 * ATT_WIDTH:])
        x_last = x.reshape(B, S, D_MODEL)[:, -1]
        new_shift.append(_rms_mod(x_last, sc1, sh1))
        keep = min(MAX_WINDOW, S)
        kv_rows = kv_rows.reshape(B, S, N_ATT_HEADS, 2 * HEAD_DIM)[:, S - keep:]
        k_rows, v_rows = kv_rows[..., :HEAD_DIM], kv_rows[..., HEAD_DIM:]

        if k_buf is None:
            att = _att_prompt(qkv, B, P["att_out_g"][l])
        else:
            assert S == 1
            q = unslab(qkv[:n_pairs]).reshape(B, N_ATT_HEADS, HEAD_DIM)
            att = _att_decode(q, k_rows.reshape(B, N_ATT_HEADS, HEAD_DIM), v_rows.reshape(B, N_ATT_HEADS, HEAD_DIM),
                              k_buf, v_buf, l, P["att_out_g"][l])
            att = jnp.transpose(att.reshape(B, n_pairs, LANES), (1, 0, 2))

        zero = jnp.zeros((RWKV_WIDTH,), F32)
        vecs = jnp.stack([P["decay_w0"][l], P["iclr_a0"][l], P["vres_v0"][l - 1] if l > 0 else zero,
                          P["k_k"][l], P["k_a"][l], zero, zero, zero])
        r, lw, k, v, a, b, g = _rwkv_prep(rest, rest_prev, P["mu_rkv"][l].reshape(1, N_RKV), vecs, w_lora2,
                                          v_first, S)
        if l == 0:
            v_first = v
        if S == 1:
            y, wkv = _wkv_step(r, lw, k, v, a, b, wkv0, l)
        else:
            seq = lambda t: t.reshape(B, S, RWKV_WIDTH)
            y, wkv = _wkv(seq(r), seq(lw), seq(k), seq(v), seq(a), seq(b), wkv0[l])
            y = y.reshape(T, RWKV_WIDTH)

        vecs_out = jnp.stack([P["lnx_w"][l], P["lnx_b"][l], P["r_k"][l].reshape(RWKV_WIDTH)] + [zero] * 5)
        rw = jnp.pad(P["router_w"][l], ((0, 0), (0, LANES - N_EXPERTS)))
        rb = jnp.pad(P["router_b"][l], (0, LANES - N_EXPERTS)).reshape(1, LANES)
        x1, h2, logits = _out_proj(att, y, r, k, v, g, x, gt1, sc2, sh2, vecs_out, P["w_out"][l].astype(BF16),
                                   rw, rb, S)

        weights, gather_idx, scatter_idx, group_e, G = _route(logits[:, :N_EXPERTS], T)
        moe_out = _moe_experts(h2, gather_idx, scatter_idx, group_e, P["moe_w1"], P["moe_b1"],
                               P["moe_w2"], P["moe_b2"], l, G=G)
        x = _combine(moe_out, weights, x1, gt2, P["final_g"] if l == depth - 1 else None, S)

        new_k.append(k_rows)
        new_v.append(v_rows)
        new_wkv.append(wkv)
    return x.reshape(B, S, D_MODEL), jnp.stack(new_k), jnp.stack(new_v), jnp.stack(new_wkv), jnp.stack(new_shift)


def kernel(x_prompt, x_sample, c_prompt, c_sample, state_attn_k, state_attn_v, state_wkv, state_shift, w_ada, b_ada, w_in, att_out_g, mu_rkv, mu_wag, decay_w0, decay_w1, decay_w2, iclr_a0, iclr_a1, iclr_a2, gate_g1, gate_g2, vres_mu, vres_v0, vres_v1, vres_v2, k_k, k_a, r_k, lnx_w, lnx_b, w_out, router_w, router_b, moe_w1, moe_b1, moe_w2, moe_b2, final_g):
    P = dict(w_ada=w_ada, b_ada=b_ada, w_in=w_in, att_out_g=att_out_g, mu_rkv=mu_rkv, mu_wag=mu_wag,
             decay_w0=decay_w0, decay_w1=decay_w1, decay_w2=decay_w2, iclr_a0=iclr_a0, iclr_a1=iclr_a1,
             iclr_a2=iclr_a2, gate_g1=gate_g1, gate_g2=gate_g2, vres_mu=vres_mu, vres_v0=vres_v0,
             vres_v1=vres_v1, vres_v2=vres_v2, k_k=k_k, k_a=k_a, r_k=r_k, lnx_w=lnx_w, lnx_b=lnx_b,
             w_out=w_out, router_w=router_w, router_b=router_b, moe_w1=moe_w1, moe_b1=moe_b1,
             moe_w2=moe_w2, moe_b2=moe_b2, final_g=final_g)
    depth = w_ada.shape[0]
    B = x_prompt.shape[0]
    wkv0_prompt = jnp.zeros((depth, B, N_RWKV_HEADS, HEAD_DIM, HEAD_DIM), F32)
    shift0_prompt = jnp.zeros((depth, B, D_MODEL), x_prompt.dtype)
    y_p, k_p, v_p, wkv_p, shift_p = _trunk(x_prompt, c_prompt, P, wkv0_prompt, shift0_prompt, None, None)
    y_s, k_s, v_s, wkv_s, shift_s = _trunk(x_sample, c_sample, P, state_wkv, state_shift, state_attn_k, state_attn_v)
    return (y_p, y_s, k_p, v_p, wkv_p, shift_p, k_s, v_s, wkv_s, shift_s)
```

```python
import functools
import math

import jax
import jax.numpy as jnp
from jax import lax
from jax.experimental import pallas as pl
from jax.experimental.pallas import tpu as pltpu

F32 = jnp.float32
BF16 = jnp.bfloat16
HIGHEST = lax.Precision.HIGHEST

D_MODEL = 1024
HEAD_DIM = 64
N_ATT_HEADS = 8
N_RWKV_HEADS = 8
ATT_WIDTH = N_ATT_HEADS * HEAD_DIM
RWKV_WIDTH = N_RWKV_HEADS * HEAD_DIM
IN_COLS = 3 * ATT_WIDTH + 3 * RWKV_WIDTH
DILATIONS = (1, 4, 16)
ATT_BLOCK = 128
MAX_WINDOW = 2048
N_EXPERTS = 32
TOP_K = 4
D_EXPERT = 1024
SWIGLU_LIMIT = 7.0
SWIGLU_ALPHA = 1.702
NORM_EPS = 1e-5
LNX_EPS = 64e-5
NEG_BIG = -1e30
WKV_CHUNK = 64
LANES = 128
VMEM_LIMIT = 56 * 1024 * 1024
LORA_WIDTHS = (64, 64, 128, 32)
LORA_COLS = 384
N_RKV = 3 * RWKV_WIDTH
N_REST = N_RKV + 2 * LORA_COLS
ROW_TILE_SUBLANES = D_MODEL // LANES


def _params(n_grid):
    return pltpu.CompilerParams(dimension_semantics=("arbitrary",) * n_grid,
                                vmem_limit_bytes=VMEM_LIMIT)


def _dot(a, b, precision=None):
    return jnp.dot(a, b, preferred_element_type=F32, precision=precision)


def _dot_nt(a, b, precision=None):
    return lax.dot_general(a, b, (((1,), (1,)), ((), ())), preferred_element_type=F32, precision=precision)


def _dot_tn(a, b, precision=None):
    return lax.dot_general(a, b, (((0,), (0,)), ((), ())), preferred_element_type=F32, precision=precision)


def _split3(x):
    hi = x.astype(BF16)
    r1 = x - hi.astype(F32)
    mid = r1.astype(BF16)
    lo = (r1 - mid.astype(F32)).astype(BF16)
    return hi, mid, lo


def _head_sum_matrix():
    hr = lax.broadcasted_iota(jnp.int32, (LANES, LANES), 0) // HEAD_DIM
    hc = lax.broadcasted_iota(jnp.int32, (LANES, LANES), 1) // HEAD_DIM
    return (hr == hc).astype(F32)


def _rms_mod(x, sc, sh):
    return x * lax.rsqrt(jnp.mean(x * x, axis=-1, keepdims=True) + NORM_EPS) * (1.0 + sc) + sh


def _row_tile(T, S, tm):
    tm = min(tm, T)
    assert T % tm == 0 and (S == 1 or S % tm == 0)
    return tm


def _mod_operand(v, S, tm):
    B, N = v.shape
    if S == 1:
        return v.reshape(1, B, N), pl.BlockSpec((None, tm, N), lambda i: (0, i, 0))
    return v.reshape(B, 1, N), pl.BlockSpec((None, 1, N), lambda i: (i * tm // S, 0, 0))


def _mm_body(x_ref, w_ref, o_ref):
    o_ref[...] = _dot(x_ref[...].astype(BF16), w_ref[...])


def _mm_rows(x, w, *, tn=None):
    M, K = x.shape
    N = w.shape[1]
    Mp = -(-M // 8) * 8
    if Mp != M:
        x = jnp.pad(x, ((0, Mp - M), (0, 0)))
    tn = N if tn is None else tn
    assert N % tn == 0
    out = pl.pallas_call(
        _mm_body,
        grid=(N // tn,),
        in_specs=[pl.BlockSpec((Mp, K), lambda j: (0, 0)),
                  pl.BlockSpec((K, tn), lambda j: (0, j))],
        out_specs=pl.BlockSpec((Mp, tn), lambda j: (0, j)),
        out_shape=jax.ShapeDtypeStruct((Mp, N), F32),
        compiler_params=_params(1),
        name="mm_rows",
    )(x, w)
    return out[:M]


def _pre_proj_body(x_ref, sc_ref, sh_ref, w_ref, qkv_ref, kv_ref, rest_ref):
    h = _rms_mod(x_ref[...], sc_ref[...], sh_ref[...])
    acc = _dot(h.astype(BF16), w_ref[...])
    n_slabs = qkv_ref.shape[0]
    for i in range(n_slabs):
        qkv_ref[i] = acc[:, i * LANES:(i + 1) * LANES]
    tm = acc.shape[0]
    for h in range(N_ATT_HEADS):
        k_h = acc[:, ATT_WIDTH + h * HEAD_DIM:ATT_WIDTH + (h + 1) * HEAD_DIM]
        v_h = acc[:, 2 * ATT_WIDTH + h * HEAD_DIM:2 * ATT_WIDTH + (h + 1) * HEAD_DIM]
        kv_ref[pl.ds(h, tm, stride=N_ATT_HEADS), :] = jnp.concatenate([k_h, v_h], axis=1)
    rest_ref[...] = acc[:, n_slabs * LANES:]


def _pre_proj(x, sc, sh, w, S, *, tm=256):
    T, K = x.shape
    N = w.shape[1]
    tm = _row_tile(T, S, tm)
    n_slabs = 3 * ATT_WIDTH // LANES
    n_rest = N - 3 * ATT_WIDTH
    sc_op, sc_spec = _mod_operand(sc, S, tm)
    sh_op, sh_spec = _mod_operand(sh, S, tm)
    return pl.pallas_call(
        _pre_proj_body,
        grid=(T // tm,),
        in_specs=[pl.BlockSpec((tm, K), lambda i: (i, 0)), sc_spec, sh_spec,
                  pl.BlockSpec((K, N), lambda i: (0, 0))],
        out_specs=[pl.BlockSpec((n_slabs, tm, LANES), lambda i: (0, i, 0)),
                   pl.BlockSpec((tm * N_ATT_HEADS, LANES), lambda i: (i, 0)),
                   pl.BlockSpec((tm, n_rest), lambda i: (i, 0))],
        out_shape=[jax.ShapeDtypeStruct((n_slabs, T, LANES), F32), jax.ShapeDtypeStruct((T * N_ATT_HEADS, LANES), F32),
                   jax.ShapeDtypeStruct((T, n_rest), F32)],
        compiler_params=_params(1),
        name="pre_proj",
    )(x, sc_op, sh_op, w)


def _att_prompt_body(qkv_ref, g_ref, o_ref, m_ref, l_ref, *, seq):
    scale = HEAD_DIM ** -0.5
    qi = lax.broadcasted_iota(jnp.int32, (ATT_BLOCK, ATT_BLOCK), 0)
    kj = lax.broadcasted_iota(jnp.int32, (ATT_BLOCK, ATT_BLOCK), 1)
    mask_cur = kj <= qi
    lane = lax.broadcasted_iota(jnp.int32, (ATT_BLOCK, LANES), 1)
    first_head = lane < HEAD_DIM
    n_pairs = ATT_WIDTH // LANES

    for branch, dil in enumerate(DILATIONS):
        nb = seq // (ATT_BLOCK * dil)

        def block(bi, carry, dil=dil, nb=nb, branch=branch):
            r = bi // nb
            j = bi % nb
            jp = jnp.maximum(j - 1, 0)
            if dil > 1:
                rows = pl.ds(r + dil * ATT_BLOCK * j, ATT_BLOCK, stride=dil)
                prows = pl.ds(r + dil * ATT_BLOCK * jp, ATT_BLOCK, stride=dil)
            else:
                rows = pl.ds(pl.multiple_of(ATT_BLOCK * j, ATT_BLOCK), ATT_BLOCK)
                prows = pl.ds(pl.multiple_of(ATT_BLOCK * jp, ATT_BLOCK), ATT_BLOCK)
            mp = kj >= qi + jnp.where(j > 0, 0, ATT_BLOCK)
            if branch > 0:
                m_old = m_ref[rows, :]
                l_old = l_ref[rows, :]
            heads = [(p, hh) for p in range(n_pairs) for hh in range(2)]
            sel = lambda hh: first_head if hh == 0 else jnp.logical_not(first_head)
            k1 = [qkv_ref[n_pairs + p, prows, :].astype(BF16) for p in range(n_pairs)]
            k2 = [qkv_ref[n_pairs + p, rows, :].astype(BF16) for p in range(n_pairs)]
            qh = []
            for p in range(n_pairs):
                q = qkv_ref[p, rows, :] * scale
                qh += [jnp.where(sel(hh), q, 0.0).astype(BF16) for hh in range(2)]
            s1 = [_dot_nt(qh[i], k1[p]) for i, (p, hh) in enumerate(heads)]
            s2 = [_dot_nt(qh[i], k2[p]) for i, (p, hh) in enumerate(heads)]
            p1, p2, mbs = [], [], []
            for i in range(len(heads)):
                a1 = jnp.where(mp, s1[i], NEG_BIG)
                a2 = jnp.where(mask_cur, s2[i], NEG_BIG)
                mb = jnp.max(jnp.maximum(a1, a2), axis=1, keepdims=True)
                p1.append(jnp.exp(a1 - mb).astype(BF16))
                p2.append(jnp.exp(a2 - mb).astype(BF16))
                mbs.append(mb)
            ones = jnp.ones((ATT_BLOCK, LANES), BF16)
            obs, lbs = [], []
            for p in range(n_pairs):
                v1 = qkv_ref[2 * n_pairs + p, prows, :]
                v2 = qkv_ref[2 * n_pairs + p, rows, :]
                for hh in range(2):
                    i = 2 * p + hh
                    v1h = jnp.where(sel(hh), v1, 0.0).astype(BF16)
                    v2h = jnp.where(sel(hh), v2, 0.0).astype(BF16)
                    obs.append(_dot(p1[i], v1h) + _dot(p2[i], v2h))
                    lbs.append(_dot(p1[i], ones) + _dot(p2[i], ones))
            if branch > 0:
                m_old3, l_old3 = _split3(m_old), _split3(l_old)
            m_new = jnp.zeros((ATT_BLOCK, LANES), F32)
            l_new = jnp.zeros((ATT_BLOCK, LANES), F32)
            sel_r = lax.broadcasted_iota(jnp.int32, (LANES, LANES), 0)
            sel_c = lax.broadcasted_iota(jnp.int32, (LANES, LANES), 1)
            for p in range(n_pairs):
                m_pair = jnp.where(first_head, mbs[2 * p], mbs[2 * p + 1])
                l_pair = jnp.where(first_head, lbs[2 * p], lbs[2 * p + 1])
                out = obs[2 * p] + obs[2 * p + 1]
                if branch > 0:
                    spread = (sel_r == jnp.where(sel_c >= HEAD_DIM, HEAD_DIM + p, p)).astype(BF16)
                    mo = sum(_dot(t, spread) for t in m_old3)
                    lo = sum(_dot(t, spread) for t in l_old3)
                    mn = jnp.maximum(mo, m_pair)
                    a_old = jnp.exp(mo - mn)
                    a_new = jnp.exp(m_pair - mn)
                    l_pair = a_old * lo + a_new * l_pair
                    out = out * a_new + a_old * o_ref[p, rows, :]
                    m_pair = mn
                o_ref[p, rows, :] = out
                mine = (lane == p) | (lane == HEAD_DIM + p)
                m_new = jnp.where(mine, m_pair, m_new)
                l_new = jnp.where(mine, l_pair, l_new)
            m_ref[rows, :] = m_new
            l_ref[rows, :] = l_new
            return carry

        lax.fori_loop(0, seq // ATT_BLOCK, block, 0)

    head_sum = _head_sum_matrix()

    def finish(bi, carry):
        rows = pl.ds(pl.multiple_of(bi * ATT_BLOCK, ATT_BLOCK), ATT_BLOCK)
        l_all = l_ref[rows, :]
        for p in range(n_pairs):
            den = jnp.where(first_head, l_all[:, p:p + 1], l_all[:, HEAD_DIM + p:HEAD_DIM + p + 1])
            att = o_ref[p, rows, :] / den
            ms = _dot(att * att, head_sum, HIGHEST) * (1.0 / HEAD_DIM)
            o_ref[p, rows, :] = att * lax.rsqrt(ms + NORM_EPS) * g_ref[:, p * LANES:(p + 1) * LANES]
        return carry

    lax.fori_loop(0, seq // ATT_BLOCK, finish, 0)


def _att_prompt(qkv, B, gain):
    n3, T, _ = qkv.shape
    S = T // B
    n_pairs = n3 // 3
    assert S % (ATT_BLOCK * DILATIONS[-1]) == 0
    return pl.pallas_call(
        functools.partial(_att_prompt_body, seq=S),
        grid=(B,),
        in_specs=[pl.BlockSpec((n3, S, LANES), lambda b: (0, b, 0), pipeline_mode=pl.Buffered(1)),
                  pl.BlockSpec((1, ATT_WIDTH), lambda b: (0, 0))],
        out_specs=pl.BlockSpec((n_pairs, S, LANES), lambda b: (0, b, 0)),
        out_shape=jax.ShapeDtypeStruct((n_pairs, T, LANES), F32),
        scratch_shapes=[pltpu.VMEM((S, LANES), F32), pltpu.VMEM((S, LANES), F32)],
        compiler_params=_params(1),
        name="att_prompt",
    )(qkv, gain.reshape(1, ATT_WIDTH))


def _att_decode_body(q_ref, kn_ref, vn_ref, kt_ref, vt_ref, g_ref, o_ref, *, n_buf):
    H = N_ATT_HEADS
    q = q_ref[...] * HEAD_DIM ** -0.5
    kn = kn_ref[...]
    vn = vn_ref[...]
    s_new = jnp.sum(q * kn, axis=-1, keepdims=True)
    row0 = lax.broadcasted_iota(jnp.int32, (8, HEAD_DIM), 0) == 0
    s = jnp.concatenate([_dot(jnp.where(row0, q[h:h + 1], 0.0).astype(BF16), kt_ref[h].astype(BF16))[0:1]
                         for h in range(H)], axis=0)
    pos = lax.broadcasted_iota(jnp.int32, (H, n_buf), 1)
    stats = []
    for dil in DILATIONS:
        valid = (pos >= n_buf - ATT_BLOCK * dil) & ((pos & (dil - 1)) == 0)
        sd = jnp.where(valid, s, NEG_BIG)
        m = jnp.maximum(jnp.max(sd, axis=1, keepdims=True), s_new)
        p = jnp.exp(sd - m)
        p_new = jnp.exp(s_new - m)
        stats.append((p, p_new, jnp.sum(p, axis=1, keepdims=True) + p_new, m))
    m_all = jnp.maximum(jnp.maximum(stats[0][3], stats[1][3]), stats[2][3])
    pad = jnp.zeros((8 - len(DILATIONS), n_buf), F32)
    rows = []
    for h in range(H):
        hs = slice(h, h + 1)
        probs = jnp.concatenate([st[0][hs] for st in stats] + [pad], axis=0)
        nums = _dot_nt(probs.astype(BF16), vt_ref[h].astype(BF16))
        num = jnp.zeros((1, HEAD_DIM), F32)
        den = jnp.zeros((1, 1), F32)
        for i, (p, p_new, d_b, m_b) in enumerate(stats):
            w = jnp.exp(m_b[hs] - m_all[hs])
            num = num + (nums[i:i + 1] + p_new[hs] * vn[hs]) * w
            den = den + d_b[hs] * w
        rows.append(num / den)
    att = jnp.concatenate(rows, axis=0)
    ms = jnp.mean(att * att, axis=-1, keepdims=True)
    o_ref[...] = att * lax.rsqrt(ms + NORM_EPS) * g_ref[...]


def _att_decode(q, k_new, v_new, k_buf, v_buf, layer, gain):
    depth, B, n_buf, H, E = k_buf.shape
    assert n_buf >= ATT_BLOCK * DILATIONS[-1] and all(d & (d - 1) == 0 for d in DILATIONS)
    tile = pl.BlockSpec((None, H, E), lambda b: (b, 0, 0))
    buf = pl.BlockSpec((None, None, H, E, n_buf), lambda b: (layer, b, 0, 0, 0))
    view = lambda t: jnp.transpose(t, (0, 1, 3, 4, 2))
    return pl.pallas_call(
        functools.partial(_att_decode_body, n_buf=n_buf),
        grid=(B,),
        in_specs=[tile, tile, tile, buf, buf, pl.BlockSpec((H, E), lambda b: (0, 0))],
        out_specs=tile,
        out_shape=jax.ShapeDtypeStruct((B, H, E), F32),
        compiler_params=_params(1),
        name="att_decode",
    )(q, k_new, v_new, view(k_buf), view(v_buf), gain.reshape(H, E))


def _rwkv_prep_body(*refs, S, tm, has_vfirst):
    if has_vfirst:
        cur_ref, prev_ref, mu_ref, vec_ref, w2_ref, vf_ref = refs[:6]
        rest = refs[6:]
    else:
        cur_ref, prev_ref, mu_ref, vec_ref, w2_ref = refs[:5]
        vf_ref, rest = None, refs[5:]
    r_ref, lw_ref, k_ref, v_ref, a_ref, b_ref, g_ref, carry = rest
    c_lp = N_RKV + LORA_COLS
    rkv = cur_ref[:, :N_RKV]
    lp = cur_ref[:, c_lp:]
    if S == 1:
        rkv_prev = prev_ref[:, :N_RKV]
        lp_prev = prev_ref[:, c_lp:]
    else:
        i = pl.program_id(0)

        @pl.when(i == 0)
        def _():
            carry[...] = jnp.zeros_like(carry)

        first = (i % (S // tm)) == 0
        prev_row = jnp.where(first, prev_ref[...], carry[...])
        row0 = lax.broadcasted_iota(jnp.int32, (tm, 1), 0) == 0
        rkv_prev = jnp.where(row0, prev_row[:, :N_RKV], pltpu.roll(rkv, 1, 0))
        lp_prev = jnp.where(row0, prev_row[:, c_lp:], pltpu.roll(lp, 1, 0))
        carry[...] = cur_ref[tm - 1:tm, :]
    rkv = rkv + (rkv_prev - rkv) * mu_ref[...]
    r = rkv[:, :RWKV_WIDTH]
    kr = rkv[:, RWKV_WIDTH:2 * RWKV_WIDTH]
    vr = rkv[:, 2 * RWKV_WIDTH:]
    l1 = cur_ref[:, N_RKV:c_lp] + lp_prev
    t0 = l1[:, :LANES]
    lane = lax.broadcasted_iota(jnp.int32, (tm, LANES), 1)
    act = jnp.concatenate([jnp.where(lane < LORA_WIDTHS[0], jnp.tanh(t0), t0),
                           jax.nn.sigmoid(l1[:, LANES:2 * LANES]),
                           l1[:, 2 * LANES:]], axis=1)
    l2 = _dot(act.astype(BF16), w2_ref[...])
    w0, a0, v0, k_k, k_a = (vec_ref[j:j + 1, :] for j in range(5))
    z = w0 + l2[:, :RWKV_WIDTH]
    softplus_neg = jnp.maximum(-z, 0.0) + jnp.log(1.0 + jnp.exp(-jnp.abs(z)))
    lw_ref[...] = -jnp.exp(-softplus_neg - 0.5)
    a = jax.nn.sigmoid(a0 + l2[:, RWKV_WIDTH:2 * RWKV_WIDTH])
    g_ref[...] = l2[:, 2 * RWKV_WIDTH:3 * RWKV_WIDTH]
    if has_vfirst:
        vr = vr + (vf_ref[...] - vr) * jax.nn.sigmoid(v0 + l2[:, 3 * RWKV_WIDTH:])
    kk = kr * k_k
    head_sum = _head_sum_matrix()
    ss = jnp.concatenate([_dot(kk[:, p * LANES:(p + 1) * LANES] ** 2, head_sum, HIGHEST)
                          for p in range(RWKV_WIDTH // LANES)], axis=1)
    kk = kk / jnp.maximum(jnp.sqrt(ss), 1e-12)
    r_ref[...] = r
    k_ref[...] = kr * (1.0 + (a - 1.0) * k_a)
    v_ref[...] = vr
    a_ref[...] = -kk
    b_ref[...] = kk * a


def _rwkv_prep(rest, rest_prev, mu_rkv, vecs, w2, v_first, S, *, tm=256):
    T = rest.shape[0]
    tm = _row_tile(T, S, tm)
    prev_op, prev_spec = _mod_operand(rest_prev, S, tm)
    tile = pl.BlockSpec((tm, RWKV_WIDTH), lambda i: (i, 0))
    const = lambda a: pl.BlockSpec(a.shape, lambda i: (0,) * a.ndim)
    ops = [rest, prev_op, mu_rkv, vecs, w2]
    specs = [pl.BlockSpec((tm, N_REST), lambda i: (i, 0)), prev_spec, const(mu_rkv), const(vecs), const(w2)]
    if v_first is not None:
        ops.append(v_first)
        specs.append(tile)
    return pl.pallas_call(
        functools.partial(_rwkv_prep_body, S=S, tm=tm, has_vfirst=v_first is not None),
        grid=(T // tm,),
        in_specs=specs,
        out_specs=[tile] * 7,
        out_shape=[jax.ShapeDtypeStruct((T, RWKV_WIDTH), F32)] * 7,
        scratch_shapes=[pltpu.VMEM((1, N_REST), F32)],
        compiler_params=_params(1),
        name="rwkv_prep",
    )(*ops)


def _wkv_body(r_ref, lw_ref, k_ref, v_ref, a_ref, b_ref, s0_ref, y_ref, sn_ref, s_scr, *, n_chunks):
    C = WKV_CHUNK
    C2 = 2 * C
    n_pairs = RWKV_WIDTH // LANES
    t = pl.program_id(1)

    @pl.when(t == 0)
    def _():
        s_scr[...] = s0_ref[...]

    row = lax.broadcasted_iota(jnp.int32, (C2, C2), 0)
    col = lax.broadcasted_iota(jnp.int32, (C2, C2), 1)
    same_head = (row // C) == (col // C)
    lower_strict = same_head & (col < row)
    lower_incl = same_head & (col <= row)
    eye = (row == col).astype(F32)
    tri = (lax.broadcasted_iota(jnp.int32, (C, C), 1) <= lax.broadcasted_iota(jnp.int32, (C, C), 0)).astype(BF16)
    first_head = lax.broadcasted_iota(jnp.int32, (C, LANES), 1) < HEAD_DIM
    block_diag = _head_sum_matrix() > 0.5
    ones = jnp.ones((C, LANES), BF16)
    bf = lambda x: x.astype(BF16)

    def split(x):
        return jnp.concatenate([jnp.where(first_head, x, 0.0), jnp.where(first_head, 0.0, x)], axis=0)

    items = [(c, p) for c in range(n_chunks) for p in range(n_pairs)]
    ld = lambda ref, c, p: ref[c * C:(c + 1) * C, p * LANES:(p + 1) * LANES]
    each = lambda f, *ls: [f(*xs) for xs in zip(*ls)]
    r, lw, k, v, a, b = ([ld(ref, c, p) for c, p in items] for ref in (r_ref, lw_ref, k_ref, v_ref, a_ref, b_ref))

    lw3 = each(_split3, lw)
    cum = each(lambda t3: sum(_dot(tri, x) for x in t3), lw3)
    x_all = each(lambda a_, r_, lw_, c_: bf(jnp.concatenate([split(a_ * jnp.exp(c_ - lw_)), split(r_ * jnp.exp(c_))],
                                                            axis=0)), a, r, lw, cum)

    def keys(b_, k_, c_):
        g_inv = jnp.exp(-c_)
        bt, kt = b_ * g_inv, k_ * g_inv
        return bf(jnp.concatenate([bt, bt, kt, kt], axis=0))

    y_all = each(keys, b, k, cum)
    gram = each(_dot_nt, x_all, y_all)
    l_pow = each(lambda g_: jnp.where(lower_strict, g_[0:C2, 0:C2], 0.0), gram)
    m_ak = each(lambda g_: bf(jnp.where(lower_strict, g_[0:C2, C2:], 0.0)), gram)
    m_rb = each(lambda g_: bf(jnp.where(lower_incl, g_[C2:, 0:C2], 0.0)), gram)
    m_rk = each(lambda g_: bf(jnp.where(lower_incl, g_[C2:, C2:], 0.0)), gram)
    v_st = each(lambda v_: bf(split(v_)), v)
    t_inv = each(lambda l_: eye + l_, l_pow)
    for _ in range(int(math.log2(C)) - 1):
        l_pow = each(lambda l_: _dot(bf(l_), bf(l_)), l_pow)
        t_inv = each(lambda t_, l_: t_ + _dot(bf(t_), bf(l_)), t_inv, l_pow)
    t_inv = each(bf, t_inv)
    akv = each(_dot, m_ak, v_st)
    rkv = each(_dot, m_rk, v_st)
    tail = each(lambda c_: jnp.exp(c_[C - 1:C, :] - c_), cum)
    b_end = each(lambda b_, t_: bf(b_ * t_), b, tail)
    kv_end = each(lambda k_, t_, v_: _dot_tn(bf(k_ * t_), bf(v_)), k, tail, v)
    g_col = each(lambda t3: jnp.exp(sum(_dot_tn(x, ones) for x in t3)), lw3)

    state = [s_scr[p] for p in range(n_pairs)]
    for c in range(n_chunks):
        sel = lambda ls: ls[c * n_pairs:(c + 1) * n_pairs]
        xs = each(lambda x_, s_: _dot(x_, bf(s_)), sel(x_all), state)
        u_st = each(lambda t_, xs_, akv_: _dot(t_, bf(xs_[0:C2] + akv_)), sel(t_inv), xs, sel(akv))
        y_st = each(lambda xs_, m_, u_, rkv_: xs_[C2:] + _dot(m_, bf(u_)) + rkv_, xs, sel(m_rb), u_st, sel(rkv))
        upd = each(lambda b_, u_: _dot_tn(b_, bf(u_[0:C] + u_[C:])), sel(b_end), u_st)
        for p in range(n_pairs):
            y_ref[c * C:(c + 1) * C, p * LANES:(p + 1) * LANES] = y_st[p][0:C] + y_st[p][C:]
        state = each(lambda g_, s_, u_, kv_: g_ * s_ + jnp.where(block_diag, u_ + kv_, 0.0),
                     sel(g_col), state, upd, sel(kv_end))
    for p in range(n_pairs):
        s_scr[p] = state[p]

    @pl.when(t == pl.num_programs(1) - 1)
    def _():
        sn_ref[...] = s_scr[...]


def _wkv(r, lw, k, v, a, b, s0, *, chunks_per_step=4):
    B, T, W = r.shape
    H = N_RWKV_HEADS
    n_pairs = W // LANES
    tt = min(T, WKV_CHUNK * chunks_per_step)
    assert T % tt == 0 and tt % WKV_CHUNK == 0
    s0t = jnp.swapaxes(s0, -1, -2).reshape(B, n_pairs, 2, HEAD_DIM, HEAD_DIM)
    z = jnp.zeros_like(s0t[:, :, 0])
    s0bd = jnp.concatenate([jnp.concatenate([s0t[:, :, 0], z], axis=-1),
                            jnp.concatenate([z, s0t[:, :, 1]], axis=-1)], axis=-2)
    seq = pl.BlockSpec((None, tt, W), lambda bi, ti: (bi, ti, 0))
    st = pl.BlockSpec((None, n_pairs, LANES, LANES), lambda bi, ti: (bi, 0, 0, 0))
    y, sbd = pl.pallas_call(
        functools.partial(_wkv_body, n_chunks=tt // WKV_CHUNK),
        grid=(B, T // tt),
        in_specs=[seq] * 6 + [st],
        out_specs=[seq, st],
        out_shape=[jax.ShapeDtypeStruct((B, T, W), F32), jax.ShapeDtypeStruct((B, n_pairs, LANES, LANES), F32)],
        scratch_shapes=[pltpu.VMEM((n_pairs, LANES, LANES), F32)],
        compiler_params=_params(2),
        name="wkv",
    )(r, lw, k, v, a, b, s0bd)
    sn = jnp.stack([sbd[:, :, :HEAD_DIM, :HEAD_DIM], sbd[:, :, HEAD_DIM:, HEAD_DIM:]], axis=2)
    return y, jnp.swapaxes(sn.reshape(B, H, HEAD_DIM, HEAD_DIM), -1, -2)


def _wkv_step_body(r_ref, lw_ref, k_ref, v_ref, a_ref, b_ref, s_ref, y_ref, sn_ref):
    row0 = lax.broadcasted_iota(jnp.int32, (8, HEAD_DIM), 0) == 0
    ys = []
    for h in range(N_RWKV_HEADS):
        cols = slice(h * HEAD_DIM, (h + 1) * HEAD_DIM)
        r8, k8, v8, a8, b8 = (jnp.where(row0, ref[:, cols], 0.0) for ref in (r_ref, k_ref, v_ref, a_ref, b_ref))
        s = s_ref[h]
        s_new = (s * jnp.exp(lw_ref[:, cols]) + _dot(s, _dot_tn(a8, b8, HIGHEST), HIGHEST)
                 + _dot_tn(v8, k8, HIGHEST))
        sn_ref[h] = s_new
        ys.append(_dot_nt(r8, s_new, HIGHEST)[0:1])
    y_ref[...] = jnp.concatenate(ys, axis=1)


def _wkv_step(r, lw, k, v, a, b, states, layer):
    B, W = r.shape
    H = N_RWKV_HEADS
    row = pl.BlockSpec((None, 1, W), lambda i: (i, 0, 0))
    vec = lambda t: t.reshape(B, 1, W)
    y, sn = pl.pallas_call(
        _wkv_step_body,
        grid=(B,),
        in_specs=[row] * 6 + [pl.BlockSpec((None, None, H, HEAD_DIM, HEAD_DIM), lambda i: (layer, i, 0, 0, 0))],
        out_specs=[row, pl.BlockSpec((None, H, HEAD_DIM, HEAD_DIM), lambda i: (i, 0, 0, 0))],
        out_shape=[jax.ShapeDtypeStruct((B, 1, W), F32), jax.ShapeDtypeStruct((B, H, HEAD_DIM, HEAD_DIM), F32)],
        compiler_params=_params(1),
        name="wkv_step",
    )(vec(r), vec(lw), vec(k), vec(v), vec(a), vec(b), states)
    return y.reshape(B, W), sn


def _out_proj_body(att_ref, y_ref, r_ref, k_ref, v_ref, g_ref, x_ref, gt_ref, sc_ref, sh_ref,
                   vec_ref, wo_ref, rw_ref, rb_ref, x1_ref, h2_ref, logit_ref):
    head_sum = _head_sum_matrix().astype(BF16)
    hsum = lambda t: sum(_dot(part, head_sum) for part in _split3(t))
    n_pairs = RWKV_WIDTH // LANES
    parts = [att_ref[p].astype(BF16) for p in range(ATT_WIDTH // LANES)]
    for p in range(n_pairs):
        cols = slice(p * LANES, (p + 1) * LANES)
        y = y_ref[:, cols]
        d = y - hsum(y) * (1.0 / HEAD_DIM)
        var = hsum(d * d) * (1.0 / HEAD_DIM)
        yn = d * lax.rsqrt(var + LNX_EPS) * vec_ref[0:1, cols] + vec_ref[1:2, cols]
        rk = r_ref[:, cols] * k_ref[:, cols] * vec_ref[2:3, cols]
        bonus = hsum(rk) * v_ref[:, cols]
        parts.append(((yn + bonus) * g_ref[:, cols]).astype(BF16))
    mix = _dot(jnp.concatenate(parts, axis=1), wo_ref[...])
    x1 = x_ref[...] + gt_ref[...] * mix
    x1_ref[...] = x1
    h2 = _rms_mod(x1, sc_ref[...], sh_ref[...])
    tm = h2.shape[0]
    for j in range(ROW_TILE_SUBLANES):
        h2_ref[pl.ds(j, tm, stride=ROW_TILE_SUBLANES), :] = h2[:, j * LANES:(j + 1) * LANES]
    logit_ref[...] = _dot(h2, rw_ref[...], HIGHEST) + rb_ref[...]


def _out_proj(att, y, r, k, v, g, x, gt, sc, sh, vecs, w_out, rw, rb, S, *, tm=256):
    T = x.shape[0]
    tm = _row_tile(T, S, tm)
    n_pairs = att.shape[0]
    tile = lambda n: pl.BlockSpec((tm, n), lambda i: (i, 0))
    const = lambda a: pl.BlockSpec(a.shape, lambda i: (0,) * a.ndim)
    mods = [_mod_operand(m, S, tm) for m in (gt, sc, sh)]
    return pl.pallas_call(
        _out_proj_body,
        grid=(T // tm,),
        in_specs=[pl.BlockSpec((n_pairs, tm, LANES), lambda i: (0, i, 0))] + [tile(RWKV_WIDTH)] * 5 + [tile(D_MODEL)]
                 + [m[1] for m in mods] + [const(vecs), const(w_out), const(rw), const(rb)],
        out_specs=[tile(D_MODEL), pl.BlockSpec((tm * ROW_TILE_SUBLANES, LANES), lambda i: (i, 0)), tile(LANES)],
        out_shape=[jax.ShapeDtypeStruct((T, D_MODEL), F32), jax.ShapeDtypeStruct((T * ROW_TILE_SUBLANES, LANES), F32),
                   jax.ShapeDtypeStruct((T, LANES), F32)],
        compiler_params=_params(1),
        name="out_proj",
    )(att, y, r, k, v, g, x, *[m[0] for m in mods], vecs, w_out, rw, rb)


def _moe_body(ge_ref, gidx_ref, sidx_ref, x_hbm, w1_ref, b1_ref, w2_ref, b2_ref, out_hbm,
              xb0, xb1, ob0, ob1, w1b, w2b, gsem, ssem, *, G, ng):
    s = pl.program_id(0)

    def step(cur, x_cur, x_oth, o_cur, o_oth):
        oth = 1 - cur

        tile_rows = lambda i: pl.ds(i * ROW_TILE_SUBLANES, ROW_TILE_SUBLANES)

        hbm_rows = lambda first: pl.ds(pl.multiple_of(first, ROW_TILE_SUBLANES), ROW_TILE_SUBLANES)

        def start_gather():
            for i in range(G):
                pltpu.make_async_copy(x_hbm.at[hbm_rows(gidx_ref[0, 0, i])], x_cur.at[tile_rows(i)],
                                      gsem.at[cur]).start(priority=i % 2)

        def start_scatter():
            for i in range(G):
                pltpu.make_async_copy(o_cur.at[tile_rows(i)], out_hbm.at[hbm_rows(sidx_ref[0, 0, i])],
                                      ssem.at[cur]).start(priority=i % 2)

        def wait_rows(buf, sem):
            pltpu.make_async_copy(buf, buf, sem).wait()

        def evaluate():
            col = lambda j: pl.ds(j, G, stride=ROW_TILE_SUBLANES)
            x = jnp.concatenate([x_oth[col(j), :] for j in range(ROW_TILE_SUBLANES)], axis=1).astype(BF16)
            u = _dot(x, w1b[...]) + b1_ref[...]
            glu = jnp.minimum(u[:, :D_EXPERT], SWIGLU_LIMIT)
            lin = jnp.clip(u[:, D_EXPERT:], -SWIGLU_LIMIT, SWIGLU_LIMIT)
            act = glu * jax.nn.sigmoid(SWIGLU_ALPHA * glu) * (lin + 1.0)
            y = _dot(act.astype(BF16), w2b[...]) + b2_ref[...]
            for j in range(ROW_TILE_SUBLANES):
                o_oth[col(j), :] = y[:, j * LANES:(j + 1) * LANES]

        @pl.when((s >= 1) & (s <= ng))
        def _():
            wait_rows(x_oth, gsem.at[oth])
            e = jnp.clip(s - 1, 0, ng - 1)

            @pl.when((s == 1) | (ge_ref[e] != ge_ref[jnp.maximum(e - 1, 0)]))
            def _():
                w1b[...] = w1_ref[...].astype(BF16)
                w2b[...] = w2_ref[...].astype(BF16)

        @pl.when(s >= 3)
        def _():
            wait_rows(o_oth, ssem.at[oth])

        @pl.when(s == 0)
        def _():
            start_gather()

        @pl.when(s == 1)
        def _():
            start_gather()
            evaluate()

        @pl.when((s >= 2) & (s < ng))
        def _():
            start_gather()
            start_scatter()
            evaluate()

        @pl.when(s == ng)
        def _():
            start_scatter()
            evaluate()

        @pl.when(s == ng + 1)
        def _():
            start_scatter()
            wait_rows(o_cur, ssem.at[cur])

    @pl.when(s % 2 == 0)
    def _():
        step(0, xb0, xb1, ob0, ob1)

    @pl.when(s % 2 == 1)
    def _():
        step(1, xb1, xb0, ob1, ob0)


def _moe_experts(x, gather_idx, scatter_idx, group_e, w1, b1, w2, b2, layer, *, G):
    T = x.shape[0] // ROW_TILE_SUBLANES
    ng = group_e.shape[0]
    assert ng >= 2
    depth = w1.shape[0]
    gidx = (gather_idx * ROW_TILE_SUBLANES).reshape(ng, 1, G)
    sidx = (scatter_idx * ROW_TILE_SUBLANES).reshape(ng, 1, G)
    smem_blk = lambda f: pl.BlockSpec((1, 1, G), f, memory_space=pltpu.SMEM)
    expert = lambda s, ge: (layer, ge[jnp.clip(s - 1, 0, ng - 1)], 0, 0)
    row_tile = (G * ROW_TILE_SUBLANES, LANES)
    grid_spec = pltpu.PrefetchScalarGridSpec(
        num_scalar_prefetch=1,
        grid=(ng + 2,),
        in_specs=[
            smem_blk(lambda s, ge: (jnp.minimum(s, ng - 1), 0, 0)),
            smem_blk(lambda s, ge: (jnp.clip(s - 2, 0, ng - 1), 0, 0)),
            pl.BlockSpec(memory_space=pl.ANY),
            pl.BlockSpec((None, None, D_MODEL, 2 * D_EXPERT), expert),
            pl.BlockSpec((None, None, 1, 2 * D_EXPERT), expert),
            pl.BlockSpec((None, None, D_EXPERT, D_MODEL), expert),
            pl.BlockSpec((None, None, 1, D_MODEL), expert),
        ],
        out_specs=pl.BlockSpec(memory_space=pl.ANY),
        scratch_shapes=[
            pltpu.VMEM(row_tile, F32),
            pltpu.VMEM(row_tile, F32),
            pltpu.VMEM(row_tile, F32),
            pltpu.VMEM(row_tile, F32),
            pltpu.VMEM((D_MODEL, 2 * D_EXPERT), BF16),
            pltpu.VMEM((D_EXPERT, D_MODEL), BF16),
            pltpu.SemaphoreType.DMA((2,)),
            pltpu.SemaphoreType.DMA((2,)),
        ],
    )
    return pl.pallas_call(
        functools.partial(_moe_body, G=G, ng=ng),
        grid_spec=grid_spec,
        out_shape=jax.ShapeDtypeStruct(((T * TOP_K + G) * ROW_TILE_SUBLANES, LANES), F32),
        compiler_params=_params(1),
        name="moe_experts",
    )(group_e, gidx, sidx, x, w1, b1.reshape(depth, N_EXPERTS, 1, -1), w2, b2.reshape(depth, N_EXPERTS, 1, -1))


def _combine_body(*refs, final):
    outs = refs[:TOP_K]
    w_ref, x_ref, gt_ref = refs[TOP_K:TOP_K + 3]
    w = w_ref[...]
    tm = w.shape[0]
    rows = lambda ref: jnp.concatenate([ref[pl.ds(j, tm, stride=ROW_TILE_SUBLANES), :]
                                        for j in range(ROW_TILE_SUBLANES)], axis=1)
    acc = rows(outs[0]) * w[:, 0:1]
    for kk in range(1, TOP_K):
        acc = acc + rows(outs[kk]) * w[:, kk:kk + 1]
    x2 = x_ref[...] + gt_ref[...] * acc
    if final:
        fg_ref, o_ref = refs[TOP_K + 3:]
        o_ref[...] = x2 * lax.rsqrt(jnp.mean(x2 * x2, axis=-1, keepdims=True) + NORM_EPS) * fg_ref[...]
    else:
        refs[TOP_K + 3][...] = x2


def _combine(moe_out, weights, x, gt, final_g, S, *, tm=512):
    T = x.shape[0]
    tm = _row_tile(T, S, tm)
    gt_op, gt_spec = _mod_operand(gt, S, tm)
    tile = pl.BlockSpec((tm, D_MODEL), lambda i: (i, 0))
    ops = [moe_out] * TOP_K + [weights, x, gt_op]
    specs = [pl.BlockSpec((tm * ROW_TILE_SUBLANES, LANES), lambda i, kk=kk: (kk * (T // tm) + i, 0))
             for kk in range(TOP_K)]
    specs += [pl.BlockSpec((tm, TOP_K), lambda i: (i, 0)), tile, gt_spec]
    if final_g is not None:
        ops.append(final_g.reshape(1, D_MODEL))
        specs.append(pl.BlockSpec((1, D_MODEL), lambda i: (0, 0)))
    return pl.pallas_call(
        functools.partial(_combine_body, final=final_g is not None),
        grid=(T // tm,),
        in_specs=specs,
        out_specs=tile,
        out_shape=jax.ShapeDtypeStruct((T, D_MODEL), F32),
        compiler_params=_params(1),
        name="combine",
    )(*ops)


def _route(logits, T):
    top_val, top_idx = lax.top_k(logits, TOP_K)
    weights = jax.nn.softmax(top_val, axis=-1)
    A = T * TOP_K
    G = max(8, min(256, A // N_EXPERTS))
    ng = -(-A // G) + N_EXPERTS
    flat_e = top_idx.reshape(A).astype(jnp.int32)
    order = jnp.argsort(flat_e).astype(jnp.int32)
    experts = jnp.arange(N_EXPERTS, dtype=jnp.int32)
    counts = jnp.sum(flat_e[:, None] == experts[None, :], axis=0, dtype=jnp.int32)
    padded = (counts + G - 1) // G * G
    pad_end = jnp.cumsum(padded)
    pad_start = pad_end - padded
    start = jnp.cumsum(counts) - counts
    g0 = jnp.arange(ng, dtype=jnp.int32) * G
    group_e = jnp.minimum(jnp.sum(pad_end[None, :] <= g0[:, None], axis=1, dtype=jnp.int32), N_EXPERTS - 1)
    slot = jnp.arange(ng * G, dtype=jnp.int32)
    e = jnp.repeat(group_e, G)
    q = slot - pad_start[e]
    valid = q < counts[e]
    asg = order[jnp.clip(start[e] + q, 0, A - 1)]
    gather_idx = jnp.where(valid, asg // TOP_K, 0)
    scatter_idx = jnp.where(valid, (asg % TOP_K) * T + asg // TOP_K, A + slot % G)
    return weights, gather_idx, scatter_idx, group_e, G


def _layer_weights(l, P):
    mu_w, mu_a, mu_g = P["mu_wag"][l]
    firsts = [(mu_w, P["decay_w1"][l]), (mu_a, P["iclr_a1"][l]), (mu_g, P["gate_g1"][l])]
    seconds = [P["decay_w2"][l], P["iclr_a2"][l], P["gate_g2"][l]]
    if l > 0:
        firsts.append((P["vres_mu"][l - 1], P["vres_v1"][l - 1]))
        seconds.append(P["vres_v2"][l - 1])
    n_used = sum(w.shape[1] for _, w in firsts)
    zpad = jnp.zeros((D_MODEL, LORA_COLS - n_used), F32)
    cur = jnp.concatenate([(1.0 - mu)[:, None] * w for mu, w in firsts] + [zpad], axis=1)
    prev = jnp.concatenate([mu[:, None] * w for mu, w in firsts] + [zpad], axis=1)
    w_big = jnp.concatenate([P["w_in"][l], cur, prev], axis=1).astype(BF16)
    w2 = jnp.zeros((LORA_COLS, 4 * RWKV_WIDTH), F32)
    r0 = 0
    for i, s in enumerate(seconds):
        w2 = w2.at[r0:r0 + s.shape[0], i * RWKV_WIDTH:(i + 1) * RWKV_WIDTH].set(s)
        r0 += s.shape[0]
    return w_big, w2.astype(BF16)


def _trunk(x, c, P, wkv0, h_prev, k_buf, v_buf):
    B, S, _ = x.shape
    T = B * S
    depth = P["w_ada"].shape[0]
    n_pairs = ATT_WIDTH // LANES
    unslab = lambda t: jnp.transpose(t.reshape(n_pairs, B, -1, LANES), (1, 2, 0, 3)).reshape(B, -1, ATT_WIDTH)
    cs = jax.nn.silu(c)
    x = x.reshape(T, D_MODEL)
    new_k, new_v, new_wkv, new_shift = [], [], [], []
    v_first = None
    for l in range(depth):
        mod = _mm_rows(cs, P["w_ada"][l].astype(BF16), tn=1024) + P["b_ada"][l]
        sh1, sc1, gt1, sh2, sc2, gt2 = jnp.split(mod, 6, axis=-1)
        w_big, w_lora2 = _layer_weights(l, P)
        qkv, kv_rows, rest = _pre_proj(x, sc1, sh1, w_big, S)
        rest_prev = _mm_rows(h_prev[l], w_big[:, 3 * ATT_WIDTH:])
        x_last = x.reshape(B, S, D_MODEL)[:, -1]
        new_shift.append(_rms_mod(x_last, sc1, sh1))
        keep = min(MAX_WINDOW, S)
        kv_rows = kv_rows.reshape(B, S, N_ATT_HEADS, 2 * HEAD_DIM)[:, S - keep:]
        k_rows, v_rows = kv_rows[..., :HEAD_DIM], kv_rows[..., HEAD_DIM:]

        if k_buf is None:
            att = _att_prompt(qkv, B, P["att_out_g"][l])
        else:
            assert S == 1
            q = unslab(qkv[:n_pairs]).reshape(B, N_ATT_HEADS, HEAD_DIM)
            att = _att_decode(q, k_rows.reshape(B, N_ATT_HEADS, HEAD_DIM), v_rows.reshape(B, N_ATT_HEADS, HEAD_DIM),
                              k_buf, v_buf, l, P["att_out_g"][l])
            att = jnp.transpose(att.reshape(B, n_pairs, LANES), (1, 0, 2))

        zero = jnp.zeros((RWKV_WIDTH,), F32)
        vecs = jnp.stack([P["decay_w0"][l], P["iclr_a0"][l], P["vres_v0"][l - 1] if l > 0 else zero,
                          P["k_k"][l], P["k_a"][l], zero, zero, zero])
        r, lw, k, v, a, b, g = _rwkv_prep(rest, rest_prev, P["mu_rkv"][l].reshape(1, N_RKV), vecs, w_lora2,
                                          v_first, S)
        if l == 0:
            v_first = v
        if S == 1:
            y, wkv = _wkv_step(r, lw, k, v, a, b, wkv0, l)
        else:
            seq = lambda t: t.reshape(B, S, RWKV_WIDTH)
            y, wkv = _wkv(seq(r), seq(lw), seq(k), seq(v), seq(a), seq(b), wkv0[l])
            y = y.reshape(T, RWKV_WIDTH)

        vecs_out = jnp.stack([P["lnx_w"][l], P["lnx_b"][l], P["r_k"][l].reshape(RWKV_WIDTH)] + [zero] * 5)
        rw = jnp.pad(P["router_w"][l], ((0, 0), (0, LANES - N_EXPERTS)))
        rb = jnp.pad(P["router_b"][l], (0, LANES - N_EXPERTS)).reshape(1, LANES)
        x1, h2, logits = _out_proj(att, y, r, k, v, g, x, gt1, sc2, sh2, vecs_out, P["w_out"][l].astype(BF16),
                                   rw, rb, S)

        weights, gather_idx, scatter_idx, group_e, G = _route(logits[:, :N_EXPERTS], T)
        moe_out = _moe_experts(h2, gather_idx, scatter_idx, group_e, P["moe_w1"], P["moe_b1"],
                               P["moe_w2"], P["moe_b2"], l, G=G)
        x = _combine(moe_out, weights, x1, gt2, P["final_g"] if l == depth - 1 else None, S)

        new_k.append(k_rows)
        new_v.append(v_rows)
        new_wkv.append(wkv)
    return x.reshape(B, S, D_MODEL), jnp.stack(new_k), jnp.stack(new_v), jnp.stack(new_wkv), jnp.stack(new_shift)


def kernel(x_prompt, x_sample, c_prompt, c_sample, state_attn_k, state_attn_v, state_wkv, state_shift, w_ada, b_ada, w_in, att_out_g, mu_rkv, mu_wag, decay_w0, decay_w1, decay_w2, iclr_a0, iclr_a1, iclr_a2, gate_g1, gate_g2, vres_mu, vres_v0, vres_v1, vres_v2, k_k, k_a, r_k, lnx_w, lnx_b, w_out, router_w, router_b, moe_w1, moe_b1, moe_w2, moe_b2, final_g):
    P = dict(w_ada=w_ada, b_ada=b_ada, w_in=w_in, att_out_g=att_out_g, mu_rkv=mu_rkv, mu_wag=mu_wag,
             decay_w0=decay_w0, decay_w1=decay_w1, decay_w2=decay_w2, iclr_a0=iclr_a0, iclr_a1=iclr_a1,
             iclr_a2=iclr_a2, gate_g1=gate_g1, gate_g2=gate_g2, vres_mu=vres_mu, vres_v0=vres_v0,
             vres_v1=vres_v1, vres_v2=vres_v2, k_k=k_k, k_a=k_a, r_k=r_k, lnx_w=lnx_w, lnx_b=lnx_b,
             w_out=w_out, router_w=router_w, router_b=router_b, moe_w1=moe_w1, moe_b1=moe_b1,
             moe_w2=moe_w2, moe_b2=moe_b2, final_g=final_g)
    depth = w_ada.shape[0]
    B = x_prompt.shape[0]
    wkv0_prompt = jnp.zeros((depth, B, N_RWKV_HEADS, HEAD_DIM, HEAD_DIM), F32)
    shift0_prompt = jnp.zeros((depth, B, D_MODEL), x_prompt.dtype)
    y_p, k_p, v_p, wkv_p, shift_p = _trunk(x_prompt, c_prompt, P, wkv0_prompt, shift0_prompt, None, None)
    y_s, k_s, v_s, wkv_s, shift_s = _trunk(x_sample, c_sample, P, state_wkv, state_shift, state_attn_k, state_attn_v)
    return (y_p, y_s, k_p, v_p, wkv_p, shift_p, k_s, v_s, wkv_s, shift_s)
```

```python
import functools
import math

import jax
import jax.numpy as jnp
from jax import lax
from jax.experimental import pallas as pl
from jax.experimental.pallas import tpu as pltpu

F32 = jnp.float32
BF16 = jnp.bfloat16
HIGHEST = lax.Precision.HIGHEST

D_MODEL = 1024
HEAD_DIM = 64
N_ATT_HEADS = 8
N_RWKV_HEADS = 8
ATT_WIDTH = N_ATT_HEADS * HEAD_DIM
RWKV_WIDTH = N_RWKV_HEADS * HEAD_DIM
IN_COLS = 3 * ATT_WIDTH + 3 * RWKV_WIDTH
DILATIONS = (1, 4, 16)
ATT_BLOCK = 128
MAX_WINDOW = 2048
N_EXPERTS = 32
TOP_K = 4
D_EXPERT = 1024
SWIGLU_LIMIT = 7.0
SWIGLU_ALPHA = 1.702
NORM_EPS = 1e-5
LNX_EPS = 64e-5
NEG_BIG = -1e30
WKV_CHUNK = 64
LANES = 128
VMEM_LIMIT = 56 * 1024 * 1024
LORA_WIDTHS = (64, 64, 128, 32)
LORA_COLS = 384
N_RKV = 3 * RWKV_WIDTH
N_REST = N_RKV + 2 * LORA_COLS
ROW_TILE_SUBLANES = D_MODEL // LANES


def _params(n_grid):
    return pltpu.CompilerParams(dimension_semantics=("arbitrary",) * n_grid,
                                vmem_limit_bytes=VMEM_LIMIT)


def _dot(a, b, precision=None):
    return jnp.dot(a, b, preferred_element_type=F32, precision=precision)


def _dot_nt(a, b, precision=None):
    return lax.dot_general(a, b, (((1,), (1,)), ((), ())), preferred_element_type=F32, precision=precision)


def _dot_tn(a, b, precision=None):
    return lax.dot_general(a, b, (((0,), (0,)), ((), ())), preferred_element_type=F32, precision=precision)


def _split3(x):
    hi = x.astype(BF16)
    r1 = x - hi.astype(F32)
    mid = r1.astype(BF16)
    lo = (r1 - mid.astype(F32)).astype(BF16)
    return hi, mid, lo


def _head_sum_matrix():
    hr = lax.broadcasted_iota(jnp.int32, (LANES, LANES), 0) // HEAD_DIM
    hc = lax.broadcasted_iota(jnp.int32, (LANES, LANES), 1) // HEAD_DIM
    return (hr == hc).astype(F32)


def _rms_mod(x, sc, sh):
    return x * lax.rsqrt(jnp.mean(x * x, axis=-1, keepdims=True) + NORM_EPS) * (1.0 + sc) + sh


def _row_tile(T, S, tm):
    tm = min(tm, T)
    assert T % tm == 0 and (S == 1 or S % tm == 0)
    return tm


def _mod_operand(v, S, tm):
    B, N = v.shape
    if S == 1:
        return v.reshape(1, B, N), pl.BlockSpec((None, tm, N), lambda i: (0, i, 0))
    return v.reshape(B, 1, N), pl.BlockSpec((None, 1, N), lambda i: (i * tm // S, 0, 0))


def _mm_body(x_ref, w_ref, o_ref):
    o_ref[...] = _dot(x_ref[...].astype(BF16), w_ref[...])


def _mm_rows(x, w, *, tn=None):
    M, K = x.shape
    N = w.shape[1]
    Mp = -(-M // 8) * 8
    if Mp != M:
        x = jnp.pad(x, ((0, Mp - M), (0, 0)))
    tn = N if tn is None else tn
    assert N % tn == 0
    out = pl.pallas_call(
        _mm_body,
        grid=(N // tn,),
        in_specs=[pl.BlockSpec((Mp, K), lambda j: (0, 0)),
                  pl.BlockSpec((K, tn), lambda j: (0, j))],
        out_specs=pl.BlockSpec((Mp, tn), lambda j: (0, j)),
        out_shape=jax.ShapeDtypeStruct((Mp, N), F32),
        compiler_params=_params(1),
        name="mm_rows",
    )(x, w)
    return out[:M]


def _pre_proj_body(x_ref, sc_ref, sh_ref, w_ref, qkv_ref, kv_ref, rest_ref):
    h = _rms_mod(x_ref[...], sc_ref[...], sh_ref[...])
    acc = _dot(h.astype(BF16), w_ref[...])
    n_slabs = qkv_ref.shape[0]
    for i in range(n_slabs):
        qkv_ref[i] = acc[:, i * LANES:(i + 1) * LANES]
    tm = acc.shape[0]
    for h in range(N_ATT_HEADS):
        k_h = acc[:, ATT_WIDTH + h * HEAD_DIM:ATT_WIDTH + (h + 1) * HEAD_DIM]
        v_h = acc[:, 2 * ATT_WIDTH + h * HEAD_DIM:2 * ATT_WIDTH + (h + 1) * HEAD_DIM]
        kv_ref[pl.ds(h, tm, stride=N_ATT_HEADS), :] = jnp.concatenate([k_h, v_h], axis=1)
    rest_ref[...] = acc[:, n_slabs * LANES:]


def _pre_proj(x, sc, sh, w, S, *, tm=256):
    T, K = x.shape
    N = w.shape[1]
    tm = _row_tile(T, S, tm)
    n_slabs = 3 * ATT_WIDTH // LANES
    n_rest = N - 3 * ATT_WIDTH
    sc_op, sc_spec = _mod_operand(sc, S, tm)
    sh_op, sh_spec = _mod_operand(sh, S, tm)
    return pl.pallas_call(
        _pre_proj_body,
        grid=(T // tm,),
        in_specs=[pl.BlockSpec((tm, K), lambda i: (i, 0)), sc_spec, sh_spec,
                  pl.BlockSpec((K, N), lambda i: (0, 0))],
        out_specs=[pl.BlockSpec((n_slabs, tm, LANES), lambda i: (0, i, 0)),
                   pl.BlockSpec((tm * N_ATT_HEADS, LANES), lambda i: (i, 0)),
                   pl.BlockSpec((tm, n_rest), lambda i: (i, 0))],
        out_shape=[jax.ShapeDtypeStruct((n_slabs, T, LANES), F32), jax.ShapeDtypeStruct((T * N_ATT_HEADS, LANES), F32),
                   jax.ShapeDtypeStruct((T, n_rest), F32)],
        compiler_params=_params(1),
        name="pre_proj",
    )(x, sc_op, sh_op, w)


def _att_prompt_body(qkv_ref, g_ref, o_ref, m_ref, l_ref, *, seq):
    scale = HEAD_DIM ** -0.5
    qi = lax.broadcasted_iota(jnp.int32, (ATT_BLOCK, ATT_BLOCK), 0)
    kj = lax.broadcasted_iota(jnp.int32, (ATT_BLOCK, ATT_BLOCK), 1)
    mask_cur = kj <= qi
    lane = lax.broadcasted_iota(jnp.int32, (ATT_BLOCK, LANES), 1)
    first_head = lane < HEAD_DIM
    n_pairs = ATT_WIDTH // LANES

    for branch, dil in enumerate(DILATIONS):
        nb = seq // (ATT_BLOCK * dil)

        def block(bi, carry, dil=dil, nb=nb, branch=branch):
            r = bi // nb
            j = bi % nb
            jp = jnp.maximum(j - 1, 0)
            if dil > 1:
                rows = pl.ds(r + dil * ATT_BLOCK * j, ATT_BLOCK, stride=dil)
                prows = pl.ds(r + dil * ATT_BLOCK * jp, ATT_BLOCK, stride=dil)
            else:
                rows = pl.ds(pl.multiple_of(ATT_BLOCK * j, ATT_BLOCK), ATT_BLOCK)
                prows = pl.ds(pl.multiple_of(ATT_BLOCK * jp, ATT_BLOCK), ATT_BLOCK)
            mp = kj >= qi + jnp.where(j > 0, 0, ATT_BLOCK)
            if branch > 0:
                m_old = m_ref[rows, :]
                l_old = l_ref[rows, :]
            heads = [(p, hh) for p in range(n_pairs) for hh in range(2)]
            sel = lambda hh: first_head if hh == 0 else jnp.logical_not(first_head)
            k1 = [qkv_ref[n_pairs + p, prows, :].astype(BF16) for p in range(n_pairs)]
            k2 = [qkv_ref[n_pairs + p, rows, :].astype(BF16) for p in range(n_pairs)]
            qh = []
            for p in range(n_pairs):
                q = qkv_ref[p, rows, :] * scale
                qh += [jnp.where(sel(hh), q, 0.0).astype(BF16) for hh in range(2)]
            s1 = [_dot_nt(qh[i], k1[p]) for i, (p, hh) in enumerate(heads)]
            s2 = [_dot_nt(qh[i], k2[p]) for i, (p, hh) in enumerate(heads)]
            p1, p2, mbs = [], [], []
            for i in range(len(heads)):
                a1 = jnp.where(mp, s1[i], NEG_BIG)
                a2 = jnp.where(mask_cur, s2[i], NEG_BIG)
                mb = jnp.max(jnp.maximum(a1, a2), axis=1, keepdims=True)
                p1.append(jnp.exp(a1 - mb).astype(BF16))
                p2.append(jnp.exp(a2 - mb).astype(BF16))
                mbs.append(mb)
            ones = jnp.ones((ATT_BLOCK, LANES), BF16)
            obs, lbs = [], []
            for p in range(n_pairs):
                v1 = qkv_ref[2 * n_pairs + p, prows, :]
                v2 = qkv_ref[2 * n_pairs + p, rows, :]
                for hh in range(2):
                    i = 2 * p + hh
                    v1h = jnp.where(sel(hh), v1, 0.0).astype(BF16)
                    v2h = jnp.where(sel(hh), v2, 0.0).astype(BF16)
                    obs.append(_dot(p1[i], v1h) + _dot(p2[i], v2h))
                    lbs.append(_dot(p1[i], ones) + _dot(p2[i], ones))
            if branch > 0:
                m_old3, l_old3 = _split3(m_old), _split3(l_old)
            m_new = jnp.zeros((ATT_BLOCK, LANES), F32)
            l_new = jnp.zeros((ATT_BLOCK, LANES), F32)
            sel_r = lax.broadcasted_iota(jnp.int32, (LANES, LANES), 0)
            sel_c = lax.broadcasted_iota(jnp.int32, (LANES, LANES), 1)
            for p in range(n_pairs):
                m_pair = jnp.where(first_head, mbs[2 * p], mbs[2 * p + 1])
                l_pair = jnp.where(first_head, lbs[2 * p], lbs[2 * p + 1])
                out = obs[2 * p] + obs[2 * p + 1]
                if branch > 0:
                    spread = (sel_r == jnp.where(sel_c >= HEAD_DIM, HEAD_DIM + p, p)).astype(BF16)
                    mo = sum(_dot(t, spread) for t in m_old3)
                    lo = sum(_dot(t, spread) for t in l_old3)
                    mn = jnp.maximum(mo, m_pair)
                    a_old = jnp.exp(mo - mn)
                    a_new = jnp.exp(m_pair - mn)
                    l_pair = a_old * lo + a_new * l_pair
                    out = out * a_new + a_old * o_ref[p, rows, :]
                    m_pair = mn
                o_ref[p, rows, :] = out
                mine = (lane == p) | (lane == HEAD_DIM + p)
                m_new = jnp.where(mine, m_pair, m_new)
                l_new = jnp.where(mine, l_pair, l_new)
            m_ref[rows, :] = m_new
            l_ref[rows, :] = l_new
            return carry

        lax.fori_loop(0, seq // ATT_BLOCK, block, 0)

    head_sum = _head_sum_matrix()

    def finish(bi, carry):
        rows = pl.ds(pl.multiple_of(bi * ATT_BLOCK, ATT_BLOCK), ATT_BLOCK)
        l_all = l_ref[rows, :]
        for p in range(n_pairs):
            den = jnp.where(first_head, l_all[:, p:p + 1], l_all[:, HEAD_DIM + p:HEAD_DIM + p + 1])
            att = o_ref[p, rows, :] / den
            ms = _dot(att * att, head_sum, HIGHEST) * (1.0 / HEAD_DIM)
            o_ref[p, rows, :] = att * lax.rsqrt(ms + NORM_EPS) * g_ref[:, p * LANES:(p + 1) * LANES]
        return carry

    lax.fori_loop(0, seq // ATT_BLOCK, finish, 0)


def _att_prompt(qkv, B, gain):
    n3, T, _ = qkv.shape
    S = T // B
    n_pairs = n3 // 3
    assert S % (ATT_BLOCK * DILATIONS[-1]) == 0
    return pl.pallas_call(
        functools.partial(_att_prompt_body, seq=S),
        grid=(B,),
        in_specs=[pl.BlockSpec((n3, S, LANES), lambda b: (0, b, 0), pipeline_mode=pl.Buffered(1)),
                  pl.BlockSpec((1, ATT_WIDTH), lambda b: (0, 0))],
        out_specs=pl.BlockSpec((n_pairs, S, LANES), lambda b: (0, b, 0)),
        out_shape=jax.ShapeDtypeStruct((n_pairs, T, LANES), F32),
        scratch_shapes=[pltpu.VMEM((S, LANES), F32), pltpu.VMEM((S, LANES), F32)],
        compiler_params=_params(1),
        name="att_prompt",
    )(qkv, gain.reshape(1, ATT_WIDTH))


def _att_decode_body(q_ref, kn_ref, vn_ref, kt_ref, vt_ref, g_ref, o_ref, *, n_buf):
    H = N_ATT_HEADS
    q = q_ref[...] * HEAD_DIM ** -0.5
    kn = kn_ref[...]
    vn = vn_ref[...]
    s_new = jnp.sum(q * kn, axis=-1, keepdims=True)
    row0 = lax.broadcasted_iota(jnp.int32, (8, HEAD_DIM), 0) == 0
    s = jnp.concatenate([_dot(jnp.where(row0, q[h:h + 1], 0.0).astype(BF16), kt_ref[h].astype(BF16))[0:1]
                         for h in range(H)], axis=0)
    pos = lax.broadcasted_iota(jnp.int32, (H, n_buf), 1)
    stats = []
    for dil in DILATIONS:
        valid = (pos >= n_buf - ATT_BLOCK * dil) & ((pos & (dil - 1)) == 0)
        sd = jnp.where(valid, s, NEG_BIG)
        m = jnp.maximum(jnp.max(sd, axis=1, keepdims=True), s_new)
        p = jnp.exp(sd - m)
        p_new = jnp.exp(s_new - m)
        stats.append((p, p_new, jnp.sum(p, axis=1, keepdims=True) + p_new, m))
    m_all = jnp.maximum(jnp.maximum(stats[0][3], stats[1][3]), stats[2][3])
    pad = jnp.zeros((8 - len(DILATIONS), n_buf), F32)
    rows = []
    for h in range(H):
        hs = slice(h, h + 1)
        probs = jnp.concatenate([st[0][hs] for st in stats] + [pad], axis=0)
        nums = _dot_nt(probs.astype(BF16), vt_ref[h].astype(BF16))
        num = jnp.zeros((1, HEAD_DIM), F32)
        den = jnp.zeros((1, 1), F32)
        for i, (p, p_new, d_b, m_b) in enumerate(stats):
            w = jnp.exp(m_b[hs] - m_all[hs])
            num = num + (nums[i:i + 1] + p_new[hs] * vn[hs]) * w
            den = den + d_b[hs] * w
        rows.append(num / den)
    att = jnp.concatenate(rows, axis=0)
    ms = jnp.mean(att * att, axis=-1, keepdims=True)
    o_ref[...] = att * lax.rsqrt(ms + NORM_EPS) * g_ref[...]


def _att_decode(q, k_new, v_new, k_buf, v_buf, layer, gain):
    depth, B, n_buf, H, E = k_buf.shape
    assert n_buf >= ATT_BLOCK * DILATIONS[-1] and all(d & (d - 1) == 0 for d in DILATIONS)
    tile = pl.BlockSpec((None, H, E), lambda b: (b, 0, 0))
    buf = pl.BlockSpec((None, None, H, E, n_buf), lambda b: (layer, b, 0, 0, 0))
    view = lambda t: jnp.transpose(t, (0, 1, 3, 4, 2))
    return pl.pallas_call(
        functools.partial(_att_decode_body, n_buf=n_buf),
        grid=(B,),
        in_specs=[tile, tile, tile, buf, buf, pl.BlockSpec((H, E), lambda b: (0, 0))],
        out_specs=tile,
        out_shape=jax.ShapeDtypeStruct((B, H, E), F32),
        compiler_params=_params(1),
        name="att_decode",
    )(q, k_new, v_new, view(k_buf), view(v_buf), gain.reshape(H, E))


def _rwkv_prep_body(*refs, S, tm, has_vfirst):
    if has_vfirst:
        cur_ref, prev_ref, mu_ref, vec_ref, w2_ref, vf_ref = refs[:6]
        rest = refs[6:]
    else:
        cur_ref, prev_ref, mu_ref, vec_ref, w2_ref = refs[:5]
        vf_ref, rest = None, refs[5:]
    r_ref, lw_ref, k_ref, v_ref, a_ref, b_ref, g_ref, carry = rest
    c_lp = N_RKV + LORA_COLS
    rkv = cur_ref[:, :N_RKV]
    lp = cur_ref[:, c_lp:]
    if S == 1:
        rkv_prev = prev_ref[:, :N_RKV]
        lp_prev = prev_ref[:, c_lp:]
    else:
        i = pl.program_id(0)

        @pl.when(i == 0)
        def _():
            carry[...] = jnp.zeros_like(carry)

        first = (i % (S // tm)) == 0
        prev_row = jnp.where(first, prev_ref[...], carry[...])
        row0 = lax.broadcasted_iota(jnp.int32, (tm, 1), 0) == 0
        rkv_prev = jnp.where(row0, prev_row[:, :N_RKV], pltpu.roll(rkv, 1, 0))
        lp_prev = jnp.where(row0, prev_row[:, c_lp:], pltpu.roll(lp, 1, 0))
        carry[...] = cur_ref[tm - 1:tm, :]
    rkv = rkv + (rkv_prev - rkv) * mu_ref[...]
    r = rkv[:, :RWKV_WIDTH]
    kr = rkv[:, RWKV_WIDTH:2 * RWKV_WIDTH]
    vr = rkv[:, 2 * RWKV_WIDTH:]
    l1 = cur_ref[:, N_RKV:c_lp] + lp_prev
    t0 = l1[:, :LANES]
    lane = lax.broadcasted_iota(jnp.int32, (tm, LANES), 1)
    act = jnp.concatenate([jnp.where(lane < LORA_WIDTHS[0], jnp.tanh(t0), t0),
                           jax.nn.sigmoid(l1[:, LANES:2 * LANES]),
                           l1[:, 2 * LANES:]], axis=1)
    l2 = _dot(act.astype(BF16), w2_ref[...])
    w0, a0, v0, k_k, k_a = (vec_ref[j:j + 1, :] for j in range(5))
    z = w0 + l2[:, :RWKV_WIDTH]
    softplus_neg = jnp.maximum(-z, 0.0) + jnp.log(1.0 + jnp.exp(-jnp.abs(z)))
    lw_ref[...] = -jnp.exp(-softplus_neg - 0.5)
    a = jax.nn.sigmoid(a0 + l2[:, RWKV_WIDTH:2 * RWKV_WIDTH])
    g_ref[...] = l2[:, 2 * RWKV_WIDTH:3 * RWKV_WIDTH]
    if has_vfirst:
        vr = vr + (vf_ref[...] - vr) * jax.nn.sigmoid(v0 + l2[:, 3 * RWKV_WIDTH:])
    kk = kr * k_k
    head_sum = _head_sum_matrix()
    ss = jnp.concatenate([_dot(kk[:, p * LANES:(p + 1) * LANES] ** 2, head_sum, HIGHEST)
                          for p in range(RWKV_WIDTH // LANES)], axis=1)
    kk = kk / jnp.maximum(jnp.sqrt(ss), 1e-12)
    r_ref[...] = r
    k_ref[...] = kr * (1.0 + (a - 1.0) * k_a)
    v_ref[...] = vr
    a_ref[...] = -kk
    b_ref[...] = kk * a


def _rwkv_prep(rest, rest_prev, mu_rkv, vecs, w2, v_first, S, *, tm=256):
    T = rest.shape[0]
    tm = _row_tile(T, S, tm)
    prev_op, prev_spec = _mod_operand(rest_prev, S, tm)
    tile = pl.BlockSpec((tm, RWKV_WIDTH), lambda i: (i, 0))
    const = lambda a: pl.BlockSpec(a.shape, lambda i: (0,) * a.ndim)
    ops = [rest, prev_op, mu_rkv, vecs, w2]
    specs = [pl.BlockSpec((tm, N_REST), lambda i: (i, 0)), prev_spec, const(mu_rkv), const(vecs), const(w2)]
    if v_first is not None:
        ops.append(v_first)
        specs.append(tile)
    return pl.pallas_call(
        functools.partial(_rwkv_prep_body, S=S, tm=tm, has_vfirst=v_first is not None),
        grid=(T // tm,),
        in_specs=specs,
        out_specs=[tile] * 7,
        out_shape=[jax.ShapeDtypeStruct((T, RWKV_WIDTH), F32)] * 7,
        scratch_shapes=[pltpu.VMEM((1, N_REST), F32)],
        compiler_params=_params(1),
        name="rwkv_prep",
    )(*ops)


def _wkv_body(r_ref, lw_ref, k_ref, v_ref, a_ref, b_ref, s0_ref, y_ref, sn_ref, s_scr, *, n_chunks):
    C = WKV_CHUNK
    C2 = 2 * C
    n_pairs = RWKV_WIDTH // LANES
    t = pl.program_id(1)

    @pl.when(t == 0)
    def _():
        s_scr[...] = s0_ref[...]

    row = lax.broadcasted_iota(jnp.int32, (C2, C2), 0)
    col = lax.broadcasted_iota(jnp.int32, (C2, C2), 1)
    same_head = (row // C) == (col // C)
    lower_strict = same_head & (col < row)
    lower_incl = same_head & (col <= row)
    eye = (row == col).astype(F32)
    tri = (lax.broadcasted_iota(jnp.int32, (C, C), 1) <= lax.broadcasted_iota(jnp.int32, (C, C), 0)).astype(BF16)
    first_head = lax.broadcasted_iota(jnp.int32, (C, LANES), 1) < HEAD_DIM
    block_diag = _head_sum_matrix() > 0.5
    ones = jnp.ones((C, LANES), BF16)
    bf = lambda x: x.astype(BF16)

    def split(x):
        return jnp.concatenate([jnp.where(first_head, x, 0.0), jnp.where(first_head, 0.0, x)], axis=0)

    items = [(c, p) for c in range(n_chunks) for p in range(n_pairs)]
    ld = lambda ref, c, p: ref[c * C:(c + 1) * C, p * LANES:(p + 1) * LANES]
    each = lambda f, *ls: [f(*xs) for xs in zip(*ls)]
    r, lw, k, v, a, b = ([ld(ref, c, p) for c, p in items] for ref in (r_ref, lw_ref, k_ref, v_ref, a_ref, b_ref))

    lw3 = each(_split3, lw)
    cum = each(lambda t3: sum(_dot(tri, x) for x in t3), lw3)
    x_all = each(lambda a_, r_, lw_, c_: bf(jnp.concatenate([split(a_ * jnp.exp(c_ - lw_)), split(r_ * jnp.exp(c_))],
                                                            axis=0)), a, r, lw, cum)

    def keys(b_, k_, c_):
        g_inv = jnp.exp(-c_)
        bt, kt = b_ * g_inv, k_ * g_inv
        return bf(jnp.concatenate([bt, bt, kt, kt], axis=0))

    y_all = each(keys, b, k, cum)
    gram = each(_dot_nt, x_all, y_all)
    l_pow = each(lambda g_: jnp.where(lower_strict, g_[0:C2, 0:C2], 0.0), gram)
    m_ak = each(lambda g_: bf(jnp.where(lower_strict, g_[0:C2, C2:], 0.0)), gram)
    m_rb = each(lambda g_: bf(jnp.where(lower_incl, g_[C2:, 0:C2], 0.0)), gram)
    m_rk = each(lambda g_: bf(jnp.where(lower_incl, g_[C2:, C2:], 0.0)), gram)
    v_st = each(lambda v_: bf(split(v_)), v)
    t_inv = each(lambda l_: eye + l_, l_pow)
    for _ in range(int(math.log2(C)) - 1):
        l_pow = each(lambda l_: _dot(bf(l_), bf(l_)), l_pow)
        t_inv = each(lambda t_, l_: t_ + _dot(bf(t_), bf(l_)), t_inv, l_pow)
    t_inv = each(bf, t_inv)
    akv = each(_dot, m_ak, v_st)
    rkv = each(_dot, m_rk, v_st)
    tail = each(lambda c_: jnp.exp(c_[C - 1:C, :] - c_), cum)
    b_end = each(lambda b_, t_: bf(b_ * t_), b, tail)
    kv_end = each(lambda k_, t_, v_: _dot_tn(bf(k_ * t_), bf(v_)), k, tail, v)
    g_col = each(lambda t3: jnp.exp(sum(_dot_tn(x, ones) for x in t3)), lw3)

    state = [s_scr[p] for p in range(n_pairs)]
    for c in range(n_chunks):
        sel = lambda ls: ls[c * n_pairs:(c + 1) * n_pairs]
        xs = each(lambda x_, s_: _dot(x_, bf(s_)), sel(x_all), state)
        u_st = each(lambda t_, xs_, akv_: _dot(t_, bf(xs_[0:C2] + akv_)), sel(t_inv), xs, sel(akv))
        y_st = each(lambda xs_, m_, u_, rkv_: xs_[C2:] + _dot(m_, bf(u_)) + rkv_, xs, sel(m_rb), u_st, sel(rkv))
        upd = each(lambda b_, u_: _dot_tn(b_, bf(u_[0:C] + u_[C:])), sel(b_end), u_st)
        for p in range(n_pairs):
            y_ref[c * C:(c + 1) * C, p * LANES:(p + 1) * LANES] = y_st[p][0:C] + y_st[p][C:]
        state = each(lambda g_, s_, u_, kv_: g_ * s_ + jnp.where(block_diag, u_ + kv_, 0.0),
                     sel(g_col), state, upd, sel(kv_end))
    for p in range(n_pairs):
        s_scr[p] = state[p]

    @pl.when(t == pl.num_programs(1) - 1)
    def _():
        sn_ref[...] = s_scr[...]


def _wkv(r, lw, k, v, a, b, s0, *, chunks_per_step=4):
    B, T, W = r.shape
    H = N_RWKV_HEADS
    n_pairs = W // LANES
    tt = min(T, WKV_CHUNK * chunks_per_step)
    assert T % tt == 0 and tt % WKV_CHUNK == 0
    s0t = jnp.swapaxes(s0, -1, -2).reshape(B, n_pairs, 2, HEAD_DIM, HEAD_DIM)
    z = jnp.zeros_like(s0t[:, :, 0])
    s0bd = jnp.concatenate([jnp.concatenate([s0t[:, :, 0], z], axis=-1),
                            jnp.concatenate([z, s0t[:, :, 1]], axis=-1)], axis=-2)
    seq = pl.BlockSpec((None, tt, W), lambda bi, ti: (bi, ti, 0))
    st = pl.BlockSpec((None, n_pairs, LANES, LANES), lambda bi, ti: (bi, 0, 0, 0))
    y, sbd = pl.pallas_call(
        functools.partial(_wkv_body, n_chunks=tt // WKV_CHUNK),
        grid=(B, T // tt),
        in_specs=[seq] * 6 + [st],
        out_specs=[seq, st],
        out_shape=[jax.ShapeDtypeStruct((B, T, W), F32), jax.ShapeDtypeStruct((B, n_pairs, LANES, LANES), F32)],
        scratch_shapes=[pltpu.VMEM((n_pairs, LANES, LANES), F32)],
        compiler_params=_params(2),
        name="wkv",
    )(r, lw, k, v, a, b, s0bd)
    sn = jnp.stack([sbd[:, :, :HEAD_DIM, :HEAD_DIM], sbd[:, :, HEAD_DIM:, HEAD_DIM:]], axis=2)
    return y, jnp.swapaxes(sn.reshape(B, H, HEAD_DIM, HEAD_DIM), -1, -2)


def _wkv_step_body(r_ref, lw_ref, k_ref, v_ref, a_ref, b_ref, s_ref, y_ref, sn_ref):
    row0 = lax.broadcasted_iota(jnp.int32, (8, HEAD_DIM), 0) == 0
    ys = []
    for h in range(N_RWKV_HEADS):
        cols = slice(h * HEAD_DIM, (h + 1) * HEAD_DIM)
        r8, k8, v8, a8, b8 = (jnp.where(row0, ref[:, cols], 0.0) for ref in (r_ref, k_ref, v_ref, a_ref, b_ref))
        s = s_ref[h]
        s_new = (s * jnp.exp(lw_ref[:, cols]) + _dot(s, _dot_tn(a8, b8, HIGHEST), HIGHEST)
                 + _dot_tn(v8, k8, HIGHEST))
        sn_ref[h] = s_new
        ys.append(_dot_nt(r8, s_new, HIGHEST)[0:1])
    y_ref[...] = jnp.concatenate(ys, axis=1)


def _wkv_step(r, lw, k, v, a, b, states, layer):
    B, W = r.shape
    H = N_RWKV_HEADS
    row = pl.BlockSpec((None, 1, W), lambda i: (i, 0, 0))
    vec = lambda t: t.reshape(B, 1, W)
    y, sn = pl.pallas_call(
        _wkv_step_body,
        grid=(B,),
        in_specs=[row] * 6 + [pl.BlockSpec((None, None, H, HEAD_DIM, HEAD_DIM), lambda i: (layer, i, 0, 0, 0))],
        out_specs=[row, pl.BlockSpec((None, H, HEAD_DIM, HEAD_DIM), lambda i: (i, 0, 0, 0))],
        out_shape=[jax.ShapeDtypeStruct((B, 1, W), F32), jax.ShapeDtypeStruct((B, H, HEAD_DIM, HEAD_DIM), F32)],
        compiler_params=_params(1),
        name="wkv_step",
    )(vec(r), vec(lw), vec(k), vec(v), vec(a), vec(b), states)
    return y.reshape(B, W), sn


def _out_proj_body(att_ref, y_ref, r_ref, k_ref, v_ref, g_ref, x_ref, gt_ref, sc_ref, sh_ref,
                   vec_ref, wo_ref, rw_ref, rb_ref, x1_ref, h2_ref, logit_ref):
    head_sum = _head_sum_matrix().astype(BF16)
    hsum = lambda t: sum(_dot(part, head_sum) for part in _split3(t))
    n_pairs = RWKV_WIDTH // LANES
    parts = [att_ref[p].astype(BF16) for p in range(ATT_WIDTH // LANES)]
    for p in range(n_pairs):
        cols = slice(p * LANES, (p + 1) * LANES)
        y = y_ref[:, cols]
        d = y - hsum(y) * (1.0 / HEAD_DIM)
        var = hsum(d * d) * (1.0 / HEAD_DIM)
        yn = d * lax.rsqrt(var + LNX_EPS) * vec_ref[0:1, cols] + vec_ref[1:2, cols]
        rk = r_ref[:, cols] * k_ref[:, cols] * vec_ref[2:3, cols]
        bonus = hsum(rk) * v_ref[:, cols]
        parts.append(((yn + bonus) * g_ref[:, cols]).astype(BF16))
    mix = _dot(jnp.concatenate(parts, axis=1), wo_ref[...])
    x1 = x_ref[...] + gt_ref[...] * mix
    x1_ref[...] = x1
    h2 = _rms_mod(x1, sc_ref[...], sh_ref[...])
    tm = h2.shape[0]
    for j in range(ROW_TILE_SUBLANES):
        h2_ref[pl.ds(j, tm, stride=ROW_TILE_SUBLANES), :] = h2[:, j * LANES:(j + 1) * LANES]
    logit_ref[...] = _dot(h2, rw_ref[...], HIGHEST) + rb_ref[...]


def _out_proj(att, y, r, k, v, g, x, gt, sc, sh, vecs, w_out, rw, rb, S, *, tm=256):
    T = x.shape[0]
    tm = _row_tile(T, S, tm)
    n_pairs = att.shape[0]
    tile = lambda n: pl.BlockSpec((tm, n), lambda i: (i, 0))
    const = lambda a: pl.BlockSpec(a.shape, lambda i: (0,) * a.ndim)
    mods = [_mod_operand(m, S, tm) for m in (gt, sc, sh)]
    return pl.pallas_call(
        _out_proj_body,
        grid=(T // tm,),
        in_specs=[pl.BlockSpec((n_pairs, tm, LANES), lambda i: (0, i, 0))] + [tile(RWKV_WIDTH)] * 5 + [tile(D_MODEL)]
                 + [m[1] for m in mods] + [const(vecs), const(w_out), const(rw), const(rb)],
        out_specs=[tile(D_MODEL), pl.BlockSpec((tm * ROW_TILE_SUBLANES, LANES), lambda i: (i, 0)), tile(LANES)],
        out_shape=[jax.ShapeDtypeStruct((T, D_MODEL), F32), jax.ShapeDtypeStruct((T * ROW_TILE_SUBLANES, LANES), F32),
                   jax.ShapeDtypeStruct((T, LANES), F32)],
        compiler_params=_params(1),
        name="out_proj",
    )(att, y, r, k, v, g, x, *[m[0] for m in mods], vecs, w_out, rw, rb)


def _moe_body(ge_ref, gidx_ref, sidx_ref, x_hbm, w1_ref, b1_ref, w2_ref, b2_ref, out_hbm,
              xb0, xb1, ob0, ob1, w1b, w2b, gsem, ssem, *, G, ng):
    s = pl.program_id(0)

    def step(cur, x_cur, x_oth, o_cur, o_oth):
        oth = 1 - cur

        tile_rows = lambda i: pl.ds(i * ROW_TILE_SUBLANES, ROW_TILE_SUBLANES)

        hbm_rows = lambda first: pl.ds(pl.multiple_of(first, ROW_TILE_SUBLANES), ROW_TILE_SUBLANES)

        def start_gather():
            for i in range(G):
                pltpu.make_async_copy(x_hbm.at[hbm_rows(gidx_ref[0, 0, i])], x_cur.at[tile_rows(i)],
                                      gsem.at[cur]).start(priority=i % 2)

        def start_scatter():
            for i in range(G):
                pltpu.make_async_copy(o_cur.at[tile_rows(i)], out_hbm.at[hbm_rows(sidx_ref[0, 0, i])],
                                      ssem.at[cur]).start(priority=i % 2)

        def wait_rows(buf, sem):
            pltpu.make_async_copy(buf, buf, sem).wait()

        def evaluate():
            col = lambda j: pl.ds(j, G, stride=ROW_TILE_SUBLANES)
            x = jnp.concatenate([x_oth[col(j), :] for j in range(ROW_TILE_SUBLANES)], axis=1).astype(BF16)
            u = _dot(x, w1b[...]) + b1_ref[...]
            glu = jnp.minimum(u[:, :D_EXPERT], SWIGLU_LIMIT)
            lin = jnp.clip(u[:, D_EXPERT:], -SWIGLU_LIMIT, SWIGLU_LIMIT)
            act = glu * jax.nn.sigmoid(SWIGLU_ALPHA * glu) * (lin + 1.0)
            y = _dot(act.astype(BF16), w2b[...]) + b2_ref[...]
            for j in range(ROW_TILE_SUBLANES):
                o_oth[col(j), :] = y[:, j * LANES:(j + 1) * LANES]

        @pl.when((s >= 1) & (s <= ng))
        def _():
            wait_rows(x_oth, gsem.at[oth])
            e = jnp.clip(s - 1, 0, ng - 1)

            @pl.when((s == 1) | (ge_ref[e] != ge_ref[jnp.maximum(e - 1, 0)]))
            def _():
                w1b[...] = w1_ref[...].astype(BF16)
                w2b[...] = w2_ref[...].astype(BF16)

        @pl.when(s >= 3)
        def _():
            wait_rows(o_oth, ssem.at[oth])

        @pl.when(s == 0)
        def _():
            start_gather()

        @pl.when(s == 1)
        def _():
            start_gather()
            evaluate()

        @pl.when((s >= 2) & (s < ng))
        def _():
            start_gather()
            start_scatter()
            evaluate()

        @pl.when(s == ng)
        def _():
            start_scatter()
            evaluate()

        @pl.when(s == ng + 1)
        def _():
            start_scatter()
            wait_rows(o_cur, ssem.at[cur])

    @pl.when(s % 2 == 0)
    def _():
        step(0, xb0, xb1, ob0, ob1)

    @pl.when(s % 2 == 1)
    def _():
        step(1, xb1, xb0, ob1, ob0)


def _moe_experts(x, gather_idx, scatter_idx, group_e, w1, b1, w2, b2, layer, *, G, n_rows):
    ng = group_e.shape[0]
    assert ng >= 2
    depth = w1.shape[0]
    gidx = (gather_idx * ROW_TILE_SUBLANES).reshape(ng, 1, G)
    sidx = (scatter_idx * ROW_TILE_SUBLANES).reshape(ng, 1, G)
    smem_blk = lambda f: pl.BlockSpec((1, 1, G), f, memory_space=pltpu.SMEM)
    expert = lambda s, ge: (layer, ge[jnp.clip(s - 1, 0, ng - 1)], 0, 0)
    row_tile = (G * ROW_TILE_SUBLANES, LANES)
    grid_spec = pltpu.PrefetchScalarGridSpec(
        num_scalar_prefetch=1,
        grid=(ng + 2,),
        in_specs=[
            smem_blk(lambda s, ge: (jnp.minimum(s, ng - 1), 0, 0)),
            smem_blk(lambda s, ge: (jnp.clip(s - 2, 0, ng - 1), 0, 0)),
            pl.BlockSpec(memory_space=pl.ANY),
            pl.BlockSpec((None, None, D_MODEL, 2 * D_EXPERT), expert),
            pl.BlockSpec((None, None, 1, 2 * D_EXPERT), expert),
            pl.BlockSpec((None, None, D_EXPERT, D_MODEL), expert),
            pl.BlockSpec((None, None, 1, D_MODEL), expert),
        ],
        out_specs=pl.BlockSpec(memory_space=pl.ANY),
        scratch_shapes=[
            pltpu.VMEM(row_tile, F32),
            pltpu.VMEM(row_tile, F32),
            pltpu.VMEM(row_tile, F32),
            pltpu.VMEM(row_tile, F32),
            pltpu.VMEM((D_MODEL, 2 * D_EXPERT), BF16),
            pltpu.VMEM((D_EXPERT, D_MODEL), BF16),
            pltpu.SemaphoreType.DMA((2,)),
            pltpu.SemaphoreType.DMA((2,)),
        ],
    )
    return pl.pallas_call(
        functools.partial(_moe_body, G=G, ng=ng),
        grid_spec=grid_spec,
        out_shape=jax.ShapeDtypeStruct(((n_rows + G) * ROW_TILE_SUBLANES, LANES), F32),
        compiler_params=_params(1),
        name="moe_experts",
    )(group_e, gidx, sidx, x, w1, b1.reshape(depth, N_EXPERTS, 1, -1), w2, b2.reshape(depth, N_EXPERTS, 1, -1))


def _combine_body(*refs, final):
    outs = refs[:TOP_K]
    w_ref, x_ref, gt_ref = refs[TOP_K:TOP_K + 3]
    w = w_ref[...]
    tm = w.shape[0]
    rows = lambda ref: jnp.concatenate([ref[pl.ds(j, tm, stride=ROW_TILE_SUBLANES), :]
                                        for j in range(ROW_TILE_SUBLANES)], axis=1)
    acc = rows(outs[0]) * w[:, 0:1]
    for kk in range(1, TOP_K):
        acc = acc + rows(outs[kk]) * w[:, kk:kk + 1]
    x2 = x_ref[...] + gt_ref[...] * acc
    if final:
        fg_ref, o_ref = refs[TOP_K + 3:]
        o_ref[...] = x2 * lax.rsqrt(jnp.mean(x2 * x2, axis=-1, keepdims=True) + NORM_EPS) * fg_ref[...]
    else:
        refs[TOP_K + 3][...] = x2


def _combine(moe_out, weights, x, gt, final_g, S, *, stride, offset, tm=512):
    T = x.shape[0]
    tm = _row_tile(T, S, tm)
    assert stride % tm == 0 and offset % tm == 0
    gt_op, gt_spec = _mod_operand(gt, S, tm)
    tile = pl.BlockSpec((tm, D_MODEL), lambda i: (i, 0))
    ops = [moe_out] * TOP_K + [weights, x, gt_op]
    specs = [pl.BlockSpec((tm * ROW_TILE_SUBLANES, LANES), lambda i, kk=kk: ((kk * stride + offset) // tm + i, 0))
             for kk in range(TOP_K)]
    specs += [pl.BlockSpec((tm, TOP_K), lambda i: (i, 0)), tile, gt_spec]
    if final_g is not None:
        ops.append(final_g.reshape(1, D_MODEL))
        specs.append(pl.BlockSpec((1, D_MODEL), lambda i: (0, 0)))
    return pl.pallas_call(
        functools.partial(_combine_body, final=final_g is not None),
        grid=(T // tm,),
        in_specs=specs,
        out_specs=tile,
        out_shape=jax.ShapeDtypeStruct((T, D_MODEL), F32),
        compiler_params=_params(1),
        name="combine",
    )(*ops)


def _route(logits, T, stride):
    top_val, top_idx = lax.top_k(logits, TOP_K)
    weights = jax.nn.softmax(top_val, axis=-1)
    A = T * TOP_K
    G = max(8, min(256, A // N_EXPERTS))
    ng = -(-A // G) + N_EXPERTS
    flat_e = top_idx.reshape(A).astype(jnp.int32)
    order = jnp.argsort(flat_e).astype(jnp.int32)
    experts = jnp.arange(N_EXPERTS, dtype=jnp.int32)
    counts = jnp.sum(flat_e[:, None] == experts[None, :], axis=0, dtype=jnp.int32)
    padded = (counts + G - 1) // G * G
    pad_end = jnp.cumsum(padded)
    pad_start = pad_end - padded
    start = jnp.cumsum(counts) - counts
    g0 = jnp.arange(ng, dtype=jnp.int32) * G
    group_e = jnp.minimum(jnp.sum(pad_end[None, :] <= g0[:, None], axis=1, dtype=jnp.int32), N_EXPERTS - 1)
    in_group = jnp.arange(G, dtype=jnp.int32)[None, :]
    q = (g0 - pad_start[group_e])[:, None] + in_group
    valid = q < counts[group_e][:, None]
    asg = order[jnp.clip(start[group_e][:, None] + q, 0, A - 1)]
    gather_idx = jnp.where(valid, asg // TOP_K, 0).reshape(ng * G)
    scatter_idx = jnp.where(valid, (asg % TOP_K) * stride + asg // TOP_K, TOP_K * stride + in_group).reshape(ng * G)
    return weights, gather_idx, scatter_idx, group_e, G


def _layer_weights(l, P):
    mu_w, mu_a, mu_g = P["mu_wag"][l]
    firsts = [(mu_w, P["decay_w1"][l]), (mu_a, P["iclr_a1"][l]), (mu_g, P["gate_g1"][l])]
    seconds = [P["decay_w2"][l], P["iclr_a2"][l], P["gate_g2"][l]]
    if l > 0:
        firsts.append((P["vres_mu"][l - 1], P["vres_v1"][l - 1]))
        seconds.append(P["vres_v2"][l - 1])
    n_used = sum(w.shape[1] for _, w in firsts)
    zpad = jnp.zeros((D_MODEL, LORA_COLS - n_used), F32)
    cur = jnp.concatenate([(1.0 - mu)[:, None] * w for mu, w in firsts] + [zpad], axis=1)
    prev = jnp.concatenate([mu[:, None] * w for mu, w in firsts] + [zpad], axis=1)
    w_big = jnp.concatenate([P["w_in"][l], cur, prev], axis=1).astype(BF16)
    w2 = jnp.zeros((LORA_COLS, 4 * RWKV_WIDTH), F32)
    r0 = 0
    for i, s in enumerate(seconds):
        w2 = w2.at[r0:r0 + s.shape[0], i * RWKV_WIDTH:(i + 1) * RWKV_WIDTH].set(s)
        r0 += s.shape[0]
    return w_big, w2.astype(BF16)


def _trunk(x, c, P, wkv0, h_prev, k_buf, v_buf):
    B, S, _ = x.shape
    T = B * S
    depth = P["w_ada"].shape[0]
    n_pairs = ATT_WIDTH // LANES
    unslab = lambda t: jnp.transpose(t.reshape(n_pairs, B, -1, LANES), (1, 2, 0, 3)).reshape(B, -1, ATT_WIDTH)
    cs = jax.nn.silu(c)
    x = x.reshape(T, D_MODEL)
    new_k, new_v, new_wkv, new_shift = [], [], [], []
    v_first = None
    for l in range(depth):
        mod = _mm_rows(cs, P["w_ada"][l].astype(BF16), tn=1024) + P["b_ada"][l]
        sh1, sc1, gt1, sh2, sc2, gt2 = jnp.split(mod, 6, axis=-1)
        w_big, w_lora2 = _layer_weights(l, P)
        qkv, kv_rows, rest = _pre_proj(x, sc1, sh1, w_big, S)
        rest_prev = _mm_rows(h_prev[l], w_big[:, 3 * ATT_WIDTH:])
        x_last = x.reshape(B, S, D_MODEL)[:, -1]
        new_shift.append(_rms_mod(x_last, sc1, sh1))
        keep = min(MAX_WINDOW, S)
        kv_rows = kv_rows.reshape(B, S, N_ATT_HEADS, 2 * HEAD_DIM)[:, S - keep:]
        k_rows, v_rows = kv_rows[..., :HEAD_DIM], kv_rows[..., HEAD_DIM:]

        if k_buf is None:
            att = _att_prompt(qkv, B, P["att_out_g"][l])
        else:
            assert S == 1
            q = unslab(qkv[:n_pairs]).reshape(B, N_ATT_HEADS, HEAD_DIM)
            att = _att_decode(q, k_rows.reshape(B, N_ATT_HEADS, HEAD_DIM), v_rows.reshape(B, N_ATT_HEADS, HEAD_DIM),
                              k_buf, v_buf, l, P["att_out_g"][l])
            att = jnp.transpose(att.reshape(B, n_pairs, LANES), (1, 0, 2))

        zero = jnp.zeros((RWKV_WIDTH,), F32)
        vecs = jnp.stack([P["decay_w0"][l], P["iclr_a0"][l], P["vres_v0"][l - 1] if l > 0 else zero,
                          P["k_k"][l], P["k_a"][l], zero, zero, zero])
        r, lw, k, v, a, b, g = _rwkv_prep(rest, rest_prev, P["mu_rkv"][l].reshape(1, N_RKV), vecs, w_lora2,
                                          v_first, S)
        if l == 0:
            v_first = v
        if S == 1:
            y, wkv = _wkv_step(r, lw, k, v, a, b, wkv0, l)
        else:
            seq = lambda t: t.reshape(B, S, RWKV_WIDTH)
            y, wkv = _wkv(seq(r), seq(lw), seq(k), seq(v), seq(a), seq(b), wkv0[l])
            y = y.reshape(T, RWKV_WIDTH)

        vecs_out = jnp.stack([P["lnx_w"][l], P["lnx_b"][l], P["r_k"][l].reshape(RWKV_WIDTH)] + [zero] * 5)
        rw = jnp.pad(P["router_w"][l], ((0, 0), (0, LANES - N_EXPERTS)))
        rb = jnp.pad(P["router_b"][l], (0, LANES - N_EXPERTS)).reshape(1, LANES)
        x1, h2, logits = _out_proj(att, y, r, k, v, g, x, gt1, sc2, sh2, vecs_out, P["w_out"][l].astype(BF16),
                                   rw, rb, S)

        moe_out, weights, stride, offset = yield h2, logits[:, :N_EXPERTS]
        x = _combine(moe_out, weights, x1, gt2, P["final_g"] if l == depth - 1 else None, S,
                     stride=stride, offset=offset)

        new_k.append(k_rows)
        new_v.append(v_rows)
        new_wkv.append(wkv)
    return x.reshape(B, S, D_MODEL), jnp.stack(new_k), jnp.stack(new_v), jnp.stack(new_wkv), jnp.stack(new_shift)


def kernel(x_prompt, x_sample, c_prompt, c_sample, state_attn_k, state_attn_v, state_wkv, state_shift, w_ada, b_ada, w_in, att_out_g, mu_rkv, mu_wag, decay_w0, decay_w1, decay_w2, iclr_a0, iclr_a1, iclr_a2, gate_g1, gate_g2, vres_mu, vres_v0, vres_v1, vres_v2, k_k, k_a, r_k, lnx_w, lnx_b, w_out, router_w, router_b, moe_w1, moe_b1, moe_w2, moe_b2, final_g):
    P = dict(w_ada=w_ada, b_ada=b_ada, w_in=w_in, att_out_g=att_out_g, mu_rkv=mu_rkv, mu_wag=mu_wag,
             decay_w0=decay_w0, decay_w1=decay_w1, decay_w2=decay_w2, iclr_a0=iclr_a0, iclr_a1=iclr_a1,
             iclr_a2=iclr_a2, gate_g1=gate_g1, gate_g2=gate_g2, vres_mu=vres_mu, vres_v0=vres_v0,
             vres_v1=vres_v1, vres_v2=vres_v2, k_k=k_k, k_a=k_a, r_k=r_k, lnx_w=lnx_w, lnx_b=lnx_b,
             w_out=w_out, router_w=router_w, router_b=router_b, moe_w1=moe_w1, moe_b1=moe_b1,
             moe_w2=moe_w2, moe_b2=moe_b2, final_g=final_g)
    depth = w_ada.shape[0]
    B = x_prompt.shape[0]
    wkv0_prompt = jnp.zeros((depth, B, N_RWKV_HEADS, HEAD_DIM, HEAD_DIM), F32)
    shift0_prompt = jnp.zeros((depth, B, D_MODEL), x_prompt.dtype)
    streams = [_trunk(x_prompt, c_prompt, P, wkv0_prompt, shift0_prompt, None, None),
               _trunk(x_sample, c_sample, P, state_wkv, state_shift, state_attn_k, state_attn_v)]
    pending = [next(s) for s in streams]
    results = [None] * len(streams)
    combine_tile = 512
    for l in range(depth):
        counts = [lg.shape[0] for _, lg in pending]
        T = sum(counts)
        stride = -(-T // combine_tile) * combine_tile
        h2 = jnp.concatenate([h for h, _ in pending] + [jnp.zeros(((stride - T) * ROW_TILE_SUBLANES, LANES), F32)], axis=0)
        logits = jnp.concatenate([lg for _, lg in pending] + [jnp.zeros((stride - T, N_EXPERTS), F32)], axis=0)
        weights, gather_idx, scatter_idx, group_e, G = _route(logits, stride, stride)
        moe_out = _moe_experts(h2, gather_idx, scatter_idx, group_e, moe_w1, moe_b1, moe_w2, moe_b2, l,
                               G=G, n_rows=TOP_K * stride)
        offset, nxt = 0, []
        for i, (s, n) in enumerate(zip(streams, counts)):
            try:
                nxt.append(s.send((moe_out, weights[offset:offset + n], stride, offset)))
            except StopIteration as done:
                results[i] = done.value
            offset += n
        pending = nxt
    (y_p, k_p, v_p, wkv_p, shift_p), (y_s, k_s, v_s, wkv_s, shift_s) = results
    return (y_p, y_s, k_p, v_p, wkv_p, shift_p, k_s, v_s, wkv_s, shift_s)
```

```python
import functools
import math

import jax
import jax.numpy as jnp
from jax import lax
from jax.experimental import pallas as pl
from jax.experimental.pallas import tpu as pltpu

F32 = jnp.float32
BF16 = jnp.bfloat16
HIGHEST = lax.Precision.HIGHEST

D_MODEL = 1024
HEAD_DIM = 64
N_ATT_HEADS = 8
N_RWKV_HEADS = 8
ATT_WIDTH = N_ATT_HEADS * HEAD_DIM
RWKV_WIDTH = N_RWKV_HEADS * HEAD_DIM
IN_COLS = 3 * ATT_WIDTH + 3 * RWKV_WIDTH
DILATIONS = (1, 4, 16)
ATT_BLOCK = 128
MAX_WINDOW = 2048
N_EXPERTS = 32
TOP_K = 4
D_EXPERT = 1024
SWIGLU_LIMIT = 7.0
SWIGLU_ALPHA = 1.702
NORM_EPS = 1e-5
LNX_EPS = 64e-5
NEG_BIG = -1e30
WKV_CHUNK = 64
LANES = 128
VMEM_LIMIT = 56 * 1024 * 1024
LORA_WIDTHS = (64, 64, 128, 32)
LORA_COLS = 384
N_RKV = 3 * RWKV_WIDTH
N_REST = N_RKV + 2 * LORA_COLS
ROW_TILE_SUBLANES = D_MODEL // LANES


def _params(n_grid):
    return pltpu.CompilerParams(dimension_semantics=("arbitrary",) * n_grid,
                                vmem_limit_bytes=VMEM_LIMIT)


def _dot(a, b, precision=None):
    return jnp.dot(a, b, preferred_element_type=F32, precision=precision)


def _dot_nt(a, b, precision=None):
    return lax.dot_general(a, b, (((1,), (1,)), ((), ())), preferred_element_type=F32, precision=precision)


def _dot_tn(a, b, precision=None):
    return lax.dot_general(a, b, (((0,), (0,)), ((), ())), preferred_element_type=F32, precision=precision)


def _split3(x):
    hi = x.astype(BF16)
    r1 = x - hi.astype(F32)
    mid = r1.astype(BF16)
    lo = (r1 - mid.astype(F32)).astype(BF16)
    return hi, mid, lo


def _head_sum_matrix():
    hr = lax.broadcasted_iota(jnp.int32, (LANES, LANES), 0) // HEAD_DIM
    hc = lax.broadcasted_iota(jnp.int32, (LANES, LANES), 1) // HEAD_DIM
    return (hr == hc).astype(F32)


def _rms_mod(x, sc, sh):
    return x * lax.rsqrt(jnp.mean(x * x, axis=-1, keepdims=True) + NORM_EPS) * (1.0 + sc) + sh


def _row_tile(T, S, tm):
    tm = min(tm, T)
    assert T % tm == 0 and (S == 1 or S % tm == 0)
    return tm


def _mod_operand(v, S, tm):
    B, N = v.shape
    if S == 1:
        return v.reshape(1, B, N), pl.BlockSpec((None, tm, N), lambda i: (0, i, 0))
    return v.reshape(B, 1, N), pl.BlockSpec((None, 1, N), lambda i: (i * tm // S, 0, 0))


def _mm_body(x_ref, w_ref, o_ref):
    o_ref[...] = _dot(x_ref[...].astype(BF16), w_ref[...])


def _mm_rows(x, w, *, tn=None):
    M, K = x.shape
    N = w.shape[1]
    Mp = -(-M // 8) * 8
    if Mp != M:
        x = jnp.pad(x, ((0, Mp - M), (0, 0)))
    tn = N if tn is None else tn
    assert N % tn == 0
    out = pl.pallas_call(
        _mm_body,
        grid=(N // tn,),
        in_specs=[pl.BlockSpec((Mp, K), lambda j: (0, 0)),
                  pl.BlockSpec((K, tn), lambda j: (0, j))],
        out_specs=pl.BlockSpec((Mp, tn), lambda j: (0, j)),
        out_shape=jax.ShapeDtypeStruct((Mp, N), F32),
        compiler_params=_params(1),
        name="mm_rows",
    )(x, w)
    return out[:M]


def _pre_proj_body(x_ref, sc_ref, sh_ref, w_ref, qkv_ref, kv_ref, rest_ref):
    h = _rms_mod(x_ref[...], sc_ref[...], sh_ref[...])
    acc = _dot(h.astype(BF16), w_ref[...])
    n_slabs = qkv_ref.shape[0]
    for i in range(n_slabs):
        qkv_ref[i] = acc[:, i * LANES:(i + 1) * LANES]
    tm = acc.shape[0]
    for h in range(N_ATT_HEADS):
        k_h = acc[:, ATT_WIDTH + h * HEAD_DIM:ATT_WIDTH + (h + 1) * HEAD_DIM]
        v_h = acc[:, 2 * ATT_WIDTH + h * HEAD_DIM:2 * ATT_WIDTH + (h + 1) * HEAD_DIM]
        kv_ref[pl.ds(h, tm, stride=N_ATT_HEADS), :] = jnp.concatenate([k_h, v_h], axis=1)
    rest_ref[...] = acc[:, n_slabs * LANES:]


def _pre_proj(x, sc, sh, w, S, *, tm=256):
    T, K = x.shape
    N = w.shape[1]
    tm = _row_tile(T, S, tm)
    n_slabs = 3 * ATT_WIDTH // LANES
    n_rest = N - 3 * ATT_WIDTH
    sc_op, sc_spec = _mod_operand(sc, S, tm)
    sh_op, sh_spec = _mod_operand(sh, S, tm)
    return pl.pallas_call(
        _pre_proj_body,
        grid=(T // tm,),
        in_specs=[pl.BlockSpec((tm, K), lambda i: (i, 0)), sc_spec, sh_spec,
                  pl.BlockSpec((K, N), lambda i: (0, 0))],
        out_specs=[pl.BlockSpec((n_slabs, tm, LANES), lambda i: (0, i, 0)),
                   pl.BlockSpec((tm * N_ATT_HEADS, LANES), lambda i: (i, 0)),
                   pl.BlockSpec((tm, n_rest), lambda i: (i, 0))],
        out_shape=[jax.ShapeDtypeStruct((n_slabs, T, LANES), F32), jax.ShapeDtypeStruct((T * N_ATT_HEADS, LANES), F32),
                   jax.ShapeDtypeStruct((T, n_rest), F32)],
        compiler_params=_params(1),
        name="pre_proj",
    )(x, sc_op, sh_op, w)


def _att_prompt_body(qkv_ref, g_ref, o_ref, m_ref, l_ref, *, seq):
    scale = HEAD_DIM ** -0.5
    qi = lax.broadcasted_iota(jnp.int32, (ATT_BLOCK, ATT_BLOCK), 0)
    kj = lax.broadcasted_iota(jnp.int32, (ATT_BLOCK, ATT_BLOCK), 1)
    mask_cur = kj <= qi
    lane = lax.broadcasted_iota(jnp.int32, (ATT_BLOCK, LANES), 1)
    first_head = lane < HEAD_DIM
    n_pairs = ATT_WIDTH // LANES

    for branch, dil in enumerate(DILATIONS):
        nb = seq // (ATT_BLOCK * dil)

        def block(bi, carry, dil=dil, nb=nb, branch=branch):
            r = bi // nb
            j = bi % nb
            jp = jnp.maximum(j - 1, 0)
            if dil > 1:
                rows = pl.ds(r + dil * ATT_BLOCK * j, ATT_BLOCK, stride=dil)
                prows = pl.ds(r + dil * ATT_BLOCK * jp, ATT_BLOCK, stride=dil)
            else:
                rows = pl.ds(pl.multiple_of(ATT_BLOCK * j, ATT_BLOCK), ATT_BLOCK)
                prows = pl.ds(pl.multiple_of(ATT_BLOCK * jp, ATT_BLOCK), ATT_BLOCK)
            mp = kj >= qi + jnp.where(j > 0, 0, ATT_BLOCK)
            if branch > 0:
                m_old = m_ref[rows, :]
                l_old = l_ref[rows, :]
            heads = [(p, hh) for p in range(n_pairs) for hh in range(2)]
            sel = lambda hh: first_head if hh == 0 else jnp.logical_not(first_head)
            k1 = [qkv_ref[n_pairs + p, prows, :].astype(BF16) for p in range(n_pairs)]
            k2 = [qkv_ref[n_pairs + p, rows, :].astype(BF16) for p in range(n_pairs)]
            qh = []
            for p in range(n_pairs):
                q = qkv_ref[p, rows, :] * scale
                qh += [jnp.where(sel(hh), q, 0.0).astype(BF16) for hh in range(2)]
            s1 = [_dot_nt(qh[i], k1[p]) for i, (p, hh) in enumerate(heads)]
            s2 = [_dot_nt(qh[i], k2[p]) for i, (p, hh) in enumerate(heads)]
            p1, p2, mbs = [], [], []
            for i in range(len(heads)):
                a1 = jnp.where(mp, s1[i], NEG_BIG)
                a2 = jnp.where(mask_cur, s2[i], NEG_BIG)
                mb = jnp.max(jnp.maximum(a1, a2), axis=1, keepdims=True)
                p1.append(jnp.exp(a1 - mb).astype(BF16))
                p2.append(jnp.exp(a2 - mb).astype(BF16))
                mbs.append(mb)
            ones = jnp.ones((ATT_BLOCK, LANES), BF16)
            obs, lbs = [], []
            for p in range(n_pairs):
                v1 = qkv_ref[2 * n_pairs + p, prows, :]
                v2 = qkv_ref[2 * n_pairs + p, rows, :]
                for hh in range(2):
                    i = 2 * p + hh
                    v1h = jnp.where(sel(hh), v1, 0.0).astype(BF16)
                    v2h = jnp.where(sel(hh), v2, 0.0).astype(BF16)
                    obs.append(_dot(p1[i], v1h) + _dot(p2[i], v2h))
                    lbs.append(_dot(p1[i], ones) + _dot(p2[i], ones))
            if branch > 0:
                m_old3, l_old3 = _split3(m_old), _split3(l_old)
            m_new = jnp.zeros((ATT_BLOCK, LANES), F32)
            l_new = jnp.zeros((ATT_BLOCK, LANES), F32)
            sel_r = lax.broadcasted_iota(jnp.int32, (LANES, LANES), 0)
            sel_c = lax.broadcasted_iota(jnp.int32, (LANES, LANES), 1)
            for p in range(n_pairs):
                m_pair = jnp.where(first_head, mbs[2 * p], mbs[2 * p + 1])
                l_pair = jnp.where(first_head, lbs[2 * p], lbs[2 * p + 1])
                out = obs[2 * p] + obs[2 * p + 1]
                if branch > 0:
                    spread = (sel_r == jnp.where(sel_c >= HEAD_DIM, HEAD_DIM + p, p)).astype(BF16)
                    mo = sum(_dot(t, spread) for t in m_old3)
                    lo = sum(_dot(t, spread) for t in l_old3)
                    mn = jnp.maximum(mo, m_pair)
                    a_old = jnp.exp(mo - mn)
                    a_new = jnp.exp(m_pair - mn)
                    l_pair = a_old * lo + a_new * l_pair
                    out = out * a_new + a_old * o_ref[p, rows, :]
                    m_pair = mn
                o_ref[p, rows, :] = out
                mine = (lane == p) | (lane == HEAD_DIM + p)
                m_new = jnp.where(mine, m_pair, m_new)
                l_new = jnp.where(mine, l_pair, l_new)
            m_ref[rows, :] = m_new
            l_ref[rows, :] = l_new
            return carry

        lax.fori_loop(0, seq // ATT_BLOCK, block, 0)

    head_sum = _head_sum_matrix()

    def finish(bi, carry):
        rows = pl.ds(pl.multiple_of(bi * ATT_BLOCK, ATT_BLOCK), ATT_BLOCK)
        l_all = l_ref[rows, :]
        for p in range(n_pairs):
            den = jnp.where(first_head, l_all[:, p:p + 1], l_all[:, HEAD_DIM + p:HEAD_DIM + p + 1])
            att = o_ref[p, rows, :] / den
            ms = _dot(att * att, head_sum, HIGHEST) * (1.0 / HEAD_DIM)
            o_ref[p, rows, :] = att * lax.rsqrt(ms + NORM_EPS) * g_ref[:, p * LANES:(p + 1) * LANES]
        return carry

    lax.fori_loop(0, seq // ATT_BLOCK, finish, 0)


def _att_prompt(qkv, B, gain):
    n3, T, _ = qkv.shape
    S = T // B
    n_pairs = n3 // 3
    assert S % (ATT_BLOCK * DILATIONS[-1]) == 0
    return pl.pallas_call(
        functools.partial(_att_prompt_body, seq=S),
        grid=(B,),
        in_specs=[pl.BlockSpec((n3, S, LANES), lambda b: (0, b, 0), pipeline_mode=pl.Buffered(1)),
                  pl.BlockSpec((1, ATT_WIDTH), lambda b: (0, 0))],
        out_specs=pl.BlockSpec((n_pairs, S, LANES), lambda b: (0, b, 0)),
        out_shape=jax.ShapeDtypeStruct((n_pairs, T, LANES), F32),
        scratch_shapes=[pltpu.VMEM((S, LANES), F32), pltpu.VMEM((S, LANES), F32)],
        compiler_params=_params(1),
        name="att_prompt",
    )(qkv, gain.reshape(1, ATT_WIDTH))


def _att_decode_body(q_ref, kn_ref, vn_ref, kt_ref, vt_ref, g_ref, o_ref, *, n_buf):
    H = N_ATT_HEADS
    q = q_ref[...] * HEAD_DIM ** -0.5
    kn = kn_ref[...]
    vn = vn_ref[...]
    s_new = jnp.sum(q * kn, axis=-1, keepdims=True)
    row0 = lax.broadcasted_iota(jnp.int32, (8, HEAD_DIM), 0) == 0
    s = jnp.concatenate([_dot(jnp.where(row0, q[h:h + 1], 0.0).astype(BF16), kt_ref[h].astype(BF16))[0:1]
                         for h in range(H)], axis=0)
    pos = lax.broadcasted_iota(jnp.int32, (H, n_buf), 1)
    stats = []
    for dil in DILATIONS:
        valid = (pos >= n_buf - ATT_BLOCK * dil) & ((pos & (dil - 1)) == 0)
        sd = jnp.where(valid, s, NEG_BIG)
        m = jnp.maximum(jnp.max(sd, axis=1, keepdims=True), s_new)
        p = jnp.exp(sd - m)
        p_new = jnp.exp(s_new - m)
        stats.append((p, p_new, jnp.sum(p, axis=1, keepdims=True) + p_new, m))
    m_all = jnp.maximum(jnp.maximum(stats[0][3], stats[1][3]), stats[2][3])
    pad = jnp.zeros((8 - len(DILATIONS), n_buf), F32)
    rows = []
    for h in range(H):
        hs = slice(h, h + 1)
        probs = jnp.concatenate([st[0][hs] for st in stats] + [pad], axis=0)
        nums = _dot_nt(probs.astype(BF16), vt_ref[h].astype(BF16))
        num = jnp.zeros((1, HEAD_DIM), F32)
        den = jnp.zeros((1, 1), F32)
        for i, (p, p_new, d_b, m_b) in enumerate(stats):
            w = jnp.exp(m_b[hs] - m_all[hs])
            num = num + (nums[i:i + 1] + p_new[hs] * vn[hs]) * w
            den = den + d_b[hs] * w
        rows.append(num / den)
    att = jnp.concatenate(rows, axis=0)
    ms = jnp.mean(att * att, axis=-1, keepdims=True)
    o_ref[...] = att * lax.rsqrt(ms + NORM_EPS) * g_ref[...]


def _att_decode(q, k_new, v_new, k_buf, v_buf, layer, gain):
    depth, B, n_buf, H, E = k_buf.shape
    assert n_buf >= ATT_BLOCK * DILATIONS[-1] and all(d & (d - 1) == 0 for d in DILATIONS)
    tile = pl.BlockSpec((None, H, E), lambda b: (b, 0, 0))
    buf = pl.BlockSpec((None, None, H, E, n_buf), lambda b: (layer, b, 0, 0, 0))
    view = lambda t: jnp.transpose(t, (0, 1, 3, 4, 2))
    return pl.pallas_call(
        functools.partial(_att_decode_body, n_buf=n_buf),
        grid=(B,),
        in_specs=[tile, tile, tile, buf, buf, pl.BlockSpec((H, E), lambda b: (0, 0))],
        out_specs=tile,
        out_shape=jax.ShapeDtypeStruct((B, H, E), F32),
        compiler_params=_params(1),
        name="att_decode",
    )(q, k_new, v_new, view(k_buf), view(v_buf), gain.reshape(H, E))


def _rwkv_prep_body(*refs, S, tm, has_vfirst):
    if has_vfirst:
        cur_ref, prev_ref, mu_ref, vec_ref, w2_ref, vf_ref = refs[:6]
        rest = refs[6:]
    else:
        cur_ref, prev_ref, mu_ref, vec_ref, w2_ref = refs[:5]
        vf_ref, rest = None, refs[5:]
    r_ref, lw_ref, k_ref, v_ref, a_ref, b_ref, g_ref, carry = rest
    c_lp = N_RKV + LORA_COLS
    rkv = cur_ref[:, :N_RKV]
    lp = cur_ref[:, c_lp:]
    if S == 1:
        rkv_prev = prev_ref[:, :N_RKV]
        lp_prev = prev_ref[:, c_lp:]
    else:
        i = pl.program_id(0)

        @pl.when(i == 0)
        def _():
            carry[...] = jnp.zeros_like(carry)

        first = (i % (S // tm)) == 0
        prev_row = jnp.where(first, prev_ref[...], carry[...])
        row0 = lax.broadcasted_iota(jnp.int32, (tm, 1), 0) == 0
        rkv_prev = jnp.where(row0, prev_row[:, :N_RKV], pltpu.roll(rkv, 1, 0))
        lp_prev = jnp.where(row0, prev_row[:, c_lp:], pltpu.roll(lp, 1, 0))
        carry[...] = cur_ref[tm - 1:tm, :]
    rkv = rkv + (rkv_prev - rkv) * mu_ref[...]
    r = rkv[:, :RWKV_WIDTH]
    kr = rkv[:, RWKV_WIDTH:2 * RWKV_WIDTH]
    vr = rkv[:, 2 * RWKV_WIDTH:]
    l1 = cur_ref[:, N_RKV:c_lp] + lp_prev
    t0 = l1[:, :LANES]
    lane = lax.broadcasted_iota(jnp.int32, (tm, LANES), 1)
    act = jnp.concatenate([jnp.where(lane < LORA_WIDTHS[0], jnp.tanh(t0), t0),
                           jax.nn.sigmoid(l1[:, LANES:2 * LANES]),
                           l1[:, 2 * LANES:]], axis=1)
    l2 = _dot(act.astype(BF16), w2_ref[...])
    w0, a0, v0, k_k, k_a = (vec_ref[j:j + 1, :] for j in range(5))
    z = w0 + l2[:, :RWKV_WIDTH]
    softplus_neg = jnp.maximum(-z, 0.0) + jnp.log(1.0 + jnp.exp(-jnp.abs(z)))
    lw_ref[...] = -jnp.exp(-softplus_neg - 0.5)
    a = jax.nn.sigmoid(a0 + l2[:, RWKV_WIDTH:2 * RWKV_WIDTH])
    g_ref[...] = l2[:, 2 * RWKV_WIDTH:3 * RWKV_WIDTH]
    if has_vfirst:
        vr = vr + (vf_ref[...] - vr) * jax.nn.sigmoid(v0 + l2[:, 3 * RWKV_WIDTH:])
    kk = kr * k_k
    head_sum = _head_sum_matrix()
    ss = jnp.concatenate([_dot(kk[:, p * LANES:(p + 1) * LANES] ** 2, head_sum, HIGHEST)
                          for p in range(RWKV_WIDTH // LANES)], axis=1)
    kk = kk / jnp.maximum(jnp.sqrt(ss), 1e-12)
    r_ref[...] = r
    k_ref[...] = kr * (1.0 + (a - 1.0) * k_a)
    v_ref[...] = vr
    a_ref[...] = -kk
    b_ref[...] = kk * a


def _rwkv_prep(rest, rest_prev, mu_rkv, vecs, w2, v_first, S, *, tm=256):
    T = rest.shape[0]
    tm = _row_tile(T, S, tm)
    prev_op, prev_spec = _mod_operand(rest_prev, S, tm)
    tile = pl.BlockSpec((tm, RWKV_WIDTH), lambda i: (i, 0))
    const = lambda a: pl.BlockSpec(a.shape, lambda i: (0,) * a.ndim)
    ops = [rest, prev_op, mu_rkv, vecs, w2]
    specs = [pl.BlockSpec((tm, N_REST), lambda i: (i, 0)), prev_spec, const(mu_rkv), const(vecs), const(w2)]
    if v_first is not None:
        ops.append(v_first)
        specs.append(tile)
    return pl.pallas_call(
        functools.partial(_rwkv_prep_body, S=S, tm=tm, has_vfirst=v_first is not None),
        grid=(T // tm,),
        in_specs=specs,
        out_specs=[tile] * 7,
        out_shape=[jax.ShapeDtypeStruct((T, RWKV_WIDTH), F32)] * 7,
        scratch_shapes=[pltpu.VMEM((1, N_REST), F32)],
        compiler_params=_params(1),
        name="rwkv_prep",
    )(*ops)


def _wkv_body(r_ref, lw_ref, k_ref, v_ref, a_ref, b_ref, s0_ref, y_ref, sn_ref, s_scr, *, n_chunks):
    C = WKV_CHUNK
    C2 = 2 * C
    n_pairs = RWKV_WIDTH // LANES
    t = pl.program_id(1)

    @pl.when(t == 0)
    def _():
        s_scr[...] = s0_ref[...]

    row = lax.broadcasted_iota(jnp.int32, (C2, C2), 0)
    col = lax.broadcasted_iota(jnp.int32, (C2, C2), 1)
    same_head = (row // C) == (col // C)
    lower_strict = same_head & (col < row)
    lower_incl = same_head & (col <= row)
    eye = (row == col).astype(F32)
    tri = (lax.broadcasted_iota(jnp.int32, (C, C), 1) <= lax.broadcasted_iota(jnp.int32, (C, C), 0)).astype(BF16)
    first_head = lax.broadcasted_iota(jnp.int32, (C, LANES), 1) < HEAD_DIM
    block_diag = _head_sum_matrix() > 0.5
    ones = jnp.ones((C, LANES), BF16)
    bf = lambda x: x.astype(BF16)

    def split(x):
        return jnp.concatenate([jnp.where(first_head, x, 0.0), jnp.where(first_head, 0.0, x)], axis=0)

    items = [(c, p) for c in range(n_chunks) for p in range(n_pairs)]
    ld = lambda ref, c, p: ref[c * C:(c + 1) * C, p * LANES:(p + 1) * LANES]
    each = lambda f, *ls: [f(*xs) for xs in zip(*ls)]
    r, lw, k, v, a, b = ([ld(ref, c, p) for c, p in items] for ref in (r_ref, lw_ref, k_ref, v_ref, a_ref, b_ref))

    lw3 = each(_split3, lw)
    cum = each(lambda t3: sum(_dot(tri, x) for x in t3), lw3)
    x_all = each(lambda a_, r_, lw_, c_: bf(jnp.concatenate([split(a_ * jnp.exp(c_ - lw_)), split(r_ * jnp.exp(c_))],
                                                            axis=0)), a, r, lw, cum)

    def keys(b_, k_, c_):
        g_inv = jnp.exp(-c_)
        bt, kt = b_ * g_inv, k_ * g_inv
        return bf(jnp.concatenate([bt, bt, kt, kt], axis=0))

    y_all = each(keys, b, k, cum)
    gram = each(_dot_nt, x_all, y_all)
    l_pow = each(lambda g_: jnp.where(lower_strict, g_[0:C2, 0:C2], 0.0), gram)
    m_ak = each(lambda g_: bf(jnp.where(lower_strict, g_[0:C2, C2:], 0.0)), gram)
    m_rb = each(lambda g_: bf(jnp.where(lower_incl, g_[C2:, 0:C2], 0.0)), gram)
    m_rk = each(lambda g_: bf(jnp.where(lower_incl, g_[C2:, C2:], 0.0)), gram)
    v_st = each(lambda v_: bf(split(v_)), v)
    t_inv = each(lambda l_: eye + l_, l_pow)
    for _ in range(int(math.log2(C)) - 1):
        l_pow = each(lambda l_: _dot(bf(l_), bf(l_)), l_pow)
        t_inv = each(lambda t_, l_: t_ + _dot(bf(t_), bf(l_)), t_inv, l_pow)
    t_inv = each(bf, t_inv)
    akv = each(_dot, m_ak, v_st)
    rkv = each(_dot, m_rk, v_st)
    tail = each(lambda c_: jnp.exp(c_[C - 1:C, :] - c_), cum)
    b_end = each(lambda b_, t_: bf(b_ * t_), b, tail)
    kv_end = each(lambda k_, t_, v_: _dot_tn(bf(k_ * t_), bf(v_)), k, tail, v)
    g_col = each(lambda t3: jnp.exp(sum(_dot_tn(x, ones) for x in t3)), lw3)

    state = [s_scr[p] for p in range(n_pairs)]
    for c in range(n_chunks):
        sel = lambda ls: ls[c * n_pairs:(c + 1) * n_pairs]
        xs = each(lambda x_, s_: _dot(x_, bf(s_)), sel(x_all), state)
        u_st = each(lambda t_, xs_, akv_: _dot(t_, bf(xs_[0:C2] + akv_)), sel(t_inv), xs, sel(akv))
        y_st = each(lambda xs_, m_, u_, rkv_: xs_[C2:] + _dot(m_, bf(u_)) + rkv_, xs, sel(m_rb), u_st, sel(rkv))
        upd = each(lambda b_, u_: _dot_tn(b_, bf(u_[0:C] + u_[C:])), sel(b_end), u_st)
        for p in range(n_pairs):
            y_ref[c * C:(c + 1) * C, p * LANES:(p + 1) * LANES] = y_st[p][0:C] + y_st[p][C:]
        state = each(lambda g_, s_, u_, kv_: g_ * s_ + jnp.where(block_diag, u_ + kv_, 0.0),
                     sel(g_col), state, upd, sel(kv_end))
    for p in range(n_pairs):
        s_scr[p] = state[p]

    @pl.when(t == pl.num_programs(1) - 1)
    def _():
        sn_ref[...] = s_scr[...]


def _wkv(r, lw, k, v, a, b, s0, *, chunks_per_step=4):
    B, T, W = r.shape
    H = N_RWKV_HEADS
    n_pairs = W // LANES
    tt = min(T, WKV_CHUNK * chunks_per_step)
    assert T % tt == 0 and tt % WKV_CHUNK == 0
    s0t = jnp.swapaxes(s0, -1, -2).reshape(B, n_pairs, 2, HEAD_DIM, HEAD_DIM)
    z = jnp.zeros_like(s0t[:, :, 0])
    s0bd = jnp.concatenate([jnp.concatenate([s0t[:, :, 0], z], axis=-1),
                            jnp.concatenate([z, s0t[:, :, 1]], axis=-1)], axis=-2)
    seq = pl.BlockSpec((None, tt, W), lambda bi, ti: (bi, ti, 0))
    st = pl.BlockSpec((None, n_pairs, LANES, LANES), lambda bi, ti: (bi, 0, 0, 0))
    y, sbd = pl.pallas_call(
        functools.partial(_wkv_body, n_chunks=tt // WKV_CHUNK),
        grid=(B, T // tt),
        in_specs=[seq] * 6 + [st],
        out_specs=[seq, st],
        out_shape=[jax.ShapeDtypeStruct((B, T, W), F32), jax.ShapeDtypeStruct((B, n_pairs, LANES, LANES), F32)],
        scratch_shapes=[pltpu.VMEM((n_pairs, LANES, LANES), F32)],
        compiler_params=_params(2),
        name="wkv",
    )(r, lw, k, v, a, b, s0bd)
    sn = jnp.stack([sbd[:, :, :HEAD_DIM, :HEAD_DIM], sbd[:, :, HEAD_DIM:, HEAD_DIM:]], axis=2)
    return y, jnp.swapaxes(sn.reshape(B, H, HEAD_DIM, HEAD_DIM), -1, -2)


def _wkv_step_body(r_ref, lw_ref, k_ref, v_ref, a_ref, b_ref, s_ref, y_ref, sn_ref):
    row0 = lax.broadcasted_iota(jnp.int32, (8, HEAD_DIM), 0) == 0
    ys = []
    for h in range(N_RWKV_HEADS):
        cols = slice(h * HEAD_DIM, (h + 1) * HEAD_DIM)
        r8, k8, v8, a8, b8 = (jnp.where(row0, ref[:, cols], 0.0) for ref in (r_ref, k_ref, v_ref, a_ref, b_ref))
        s = s_ref[h]
        bf = lambda t: t.astype(BF16)
        s_new = (s * jnp.exp(lw_ref[:, cols]) + _dot(bf(s), bf(_dot_tn(bf(a8), bf(b8))))
                 + _dot_tn(bf(v8), bf(k8)))
        sn_ref[h] = s_new
        ys.append(_dot_nt(bf(r8), bf(s_new))[0:1])
    y_ref[...] = jnp.concatenate(ys, axis=1)


def _wkv_step(r, lw, k, v, a, b, states, layer):
    B, W = r.shape
    H = N_RWKV_HEADS
    row = pl.BlockSpec((None, 1, W), lambda i: (i, 0, 0))
    vec = lambda t: t.reshape(B, 1, W)
    y, sn = pl.pallas_call(
        _wkv_step_body,
        grid=(B,),
        in_specs=[row] * 6 + [pl.BlockSpec((None, None, H, HEAD_DIM, HEAD_DIM), lambda i: (layer, i, 0, 0, 0))],
        out_specs=[row, pl.BlockSpec((None, H, HEAD_DIM, HEAD_DIM), lambda i: (i, 0, 0, 0))],
        out_shape=[jax.ShapeDtypeStruct((B, 1, W), F32), jax.ShapeDtypeStruct((B, H, HEAD_DIM, HEAD_DIM), F32)],
        compiler_params=_params(1),
        name="wkv_step",
    )(vec(r), vec(lw), vec(k), vec(v), vec(a), vec(b), states)
    return y.reshape(B, W), sn


def _out_proj_body(att_ref, y_ref, r_ref, k_ref, v_ref, g_ref, x_ref, gt_ref, sc_ref, sh_ref,
                   vec_ref, wo_ref, rw_ref, rb_ref, x1_ref, h2_ref, logit_ref):
    head_sum = _head_sum_matrix().astype(BF16)
    hsum = lambda t: sum(_dot(part, head_sum) for part in _split3(t))
    n_pairs = RWKV_WIDTH // LANES
    parts = [att_ref[p].astype(BF16) for p in range(ATT_WIDTH // LANES)]
    for p in range(n_pairs):
        cols = slice(p * LANES, (p + 1) * LANES)
        y = y_ref[:, cols]
        d = y - hsum(y) * (1.0 / HEAD_DIM)
        var = hsum(d * d) * (1.0 / HEAD_DIM)
        yn = d * lax.rsqrt(var + LNX_EPS) * vec_ref[0:1, cols] + vec_ref[1:2, cols]
        rk = r_ref[:, cols] * k_ref[:, cols] * vec_ref[2:3, cols]
        bonus = hsum(rk) * v_ref[:, cols]
        parts.append(((yn + bonus) * g_ref[:, cols]).astype(BF16))
    mix = _dot(jnp.concatenate(parts, axis=1), wo_ref[...])
    x1 = x_ref[...] + gt_ref[...] * mix
    x1_ref[...] = x1
    h2 = _rms_mod(x1, sc_ref[...], sh_ref[...])
    tm = h2.shape[0]
    for j in range(ROW_TILE_SUBLANES):
        h2_ref[pl.ds(j, tm, stride=ROW_TILE_SUBLANES), :] = h2[:, j * LANES:(j + 1) * LANES]
    logit_ref[...] = _dot(h2, rw_ref[...], HIGHEST) + rb_ref[...]


def _out_proj(att, y, r, k, v, g, x, gt, sc, sh, vecs, w_out, rw, rb, S, *, tm=256):
    T = x.shape[0]
    tm = _row_tile(T, S, tm)
    n_pairs = att.shape[0]
    tile = lambda n: pl.BlockSpec((tm, n), lambda i: (i, 0))
    const = lambda a: pl.BlockSpec(a.shape, lambda i: (0,) * a.ndim)
    mods = [_mod_operand(m, S, tm) for m in (gt, sc, sh)]
    return pl.pallas_call(
        _out_proj_body,
        grid=(T // tm,),
        in_specs=[pl.BlockSpec((n_pairs, tm, LANES), lambda i: (0, i, 0))] + [tile(RWKV_WIDTH)] * 5 + [tile(D_MODEL)]
                 + [m[1] for m in mods] + [const(vecs), const(w_out), const(rw), const(rb)],
        out_specs=[tile(D_MODEL), pl.BlockSpec((tm * ROW_TILE_SUBLANES, LANES), lambda i: (i, 0)), tile(LANES)],
        out_shape=[jax.ShapeDtypeStruct((T, D_MODEL), F32), jax.ShapeDtypeStruct((T * ROW_TILE_SUBLANES, LANES), F32),
                   jax.ShapeDtypeStruct((T, LANES), F32)],
        compiler_params=_params(1),
        name="out_proj",
    )(att, y, r, k, v, g, x, *[m[0] for m in mods], vecs, w_out, rw, rb)


def _moe_body(ge_ref, gidx_ref, sidx_ref, x_hbm, w1_ref, b1_ref, w2_ref, b2_ref, out_hbm,
              xb0, xb1, ob0, ob1, w1b, w2b, gsem, ssem, *, G, ng):
    s = pl.program_id(0)

    def step(cur, x_cur, x_oth, o_cur, o_oth):
        oth = 1 - cur

        tile_rows = lambda i: pl.ds(i * ROW_TILE_SUBLANES, ROW_TILE_SUBLANES)

        hbm_rows = lambda first: pl.ds(pl.multiple_of(first, ROW_TILE_SUBLANES), ROW_TILE_SUBLANES)

        def start_gather():
            for i in range(G):
                pltpu.make_async_copy(x_hbm.at[hbm_rows(gidx_ref[0, 0, i])], x_cur.at[tile_rows(i)],
                                      gsem.at[cur]).start(priority=i % 2)

        def start_scatter():
            for i in range(G):
                pltpu.make_async_copy(o_cur.at[tile_rows(i)], out_hbm.at[hbm_rows(sidx_ref[0, 0, i])],
                                      ssem.at[cur]).start(priority=i % 2)

        def wait_rows(buf, sem):
            pltpu.make_async_copy(buf, buf, sem).wait()

        def evaluate():
            col = lambda j: pl.ds(j, G, stride=ROW_TILE_SUBLANES)
            x = jnp.concatenate([x_oth[col(j), :] for j in range(ROW_TILE_SUBLANES)], axis=1).astype(BF16)
            u = _dot(x, w1b[...]) + b1_ref[...]
            glu = jnp.minimum(u[:, :D_EXPERT], SWIGLU_LIMIT)
            lin = jnp.clip(u[:, D_EXPERT:], -SWIGLU_LIMIT, SWIGLU_LIMIT)
            act = glu * jax.nn.sigmoid(SWIGLU_ALPHA * glu) * (lin + 1.0)
            y = _dot(act.astype(BF16), w2b[...]) + b2_ref[...]
            for j in range(ROW_TILE_SUBLANES):
                o_oth[col(j), :] = y[:, j * LANES:(j + 1) * LANES]

        @pl.when((s >= 1) & (s <= ng))
        def _():
            wait_rows(x_oth, gsem.at[oth])
            e = jnp.clip(s - 1, 0, ng - 1)

            @pl.when((s == 1) | (ge_ref[e] != ge_ref[jnp.maximum(e - 1, 0)]))
            def _():
                w1b[...] = w1_ref[...].astype(BF16)
                w2b[...] = w2_ref[...].astype(BF16)

        @pl.when(s >= 3)
        def _():
            wait_rows(o_oth, ssem.at[oth])

        @pl.when(s == 0)
        def _():
            start_gather()

        @pl.when(s == 1)
        def _():
            start_gather()
            evaluate()

        @pl.when((s >= 2) & (s < ng))
        def _():
            start_gather()
            start_scatter()
            evaluate()

        @pl.when(s == ng)
        def _():
            start_scatter()
            evaluate()

        @pl.when(s == ng + 1)
        def _():
            start_scatter()
            wait_rows(o_cur, ssem.at[cur])

    @pl.when(s % 2 == 0)
    def _():
        step(0, xb0, xb1, ob0, ob1)

    @pl.when(s % 2 == 1)
    def _():
        step(1, xb1, xb0, ob1, ob0)


def _moe_experts(x, gather_idx, scatter_idx, group_e, w1, b1, w2, b2, layer, *, G, n_rows):
    ng = group_e.shape[0]
    assert ng >= 2
    depth = w1.shape[0]
    gidx = (gather_idx * ROW_TILE_SUBLANES).reshape(ng, 1, G)
    sidx = (scatter_idx * ROW_TILE_SUBLANES).reshape(ng, 1, G)
    smem_blk = lambda f: pl.BlockSpec((1, 1, G), f, memory_space=pltpu.SMEM)
    expert = lambda s, ge: (layer, ge[jnp.clip(s - 1, 0, ng - 1)], 0, 0)
    row_tile = (G * ROW_TILE_SUBLANES, LANES)
    grid_spec = pltpu.PrefetchScalarGridSpec(
        num_scalar_prefetch=1,
        grid=(ng + 2,),
        in_specs=[
            smem_blk(lambda s, ge: (jnp.minimum(s, ng - 1), 0, 0)),
            smem_blk(lambda s, ge: (jnp.clip(s - 2, 0, ng - 1), 0, 0)),
            pl.BlockSpec(memory_space=pl.ANY),
            pl.BlockSpec((None, None, D_MODEL, 2 * D_EXPERT), expert),
            pl.BlockSpec((None, None, 1, 2 * D_EXPERT), expert),
            pl.BlockSpec((None, None, D_EXPERT, D_MODEL), expert),
            pl.BlockSpec((None, None, 1, D_MODEL), expert),
        ],
        out_specs=pl.BlockSpec(memory_space=pl.ANY),
        scratch_shapes=[
            pltpu.VMEM(row_tile, F32),
            pltpu.VMEM(row_tile, F32),
            pltpu.VMEM(row_tile, F32),
            pltpu.VMEM(row_tile, F32),
            pltpu.VMEM((D_MODEL, 2 * D_EXPERT), BF16),
            pltpu.VMEM((D_EXPERT, D_MODEL), BF16),
            pltpu.SemaphoreType.DMA((2,)),
            pltpu.SemaphoreType.DMA((2,)),
        ],
    )
    return pl.pallas_call(
        functools.partial(_moe_body, G=G, ng=ng),
        grid_spec=grid_spec,
        out_shape=jax.ShapeDtypeStruct(((n_rows + G) * ROW_TILE_SUBLANES, LANES), F32),
        compiler_params=_params(1),
        name="moe_experts",
    )(group_e, gidx, sidx, x, w1, b1.reshape(depth, N_EXPERTS, 1, -1), w2, b2.reshape(depth, N_EXPERTS, 1, -1))


def _combine_body(*refs, final):
    outs = refs[:TOP_K]
    w_ref, x_ref, gt_ref = refs[TOP_K:TOP_K + 3]
    w = w_ref[...]
    tm = w.shape[0]
    rows = lambda ref: jnp.concatenate([ref[pl.ds(j, tm, stride=ROW_TILE_SUBLANES), :]
                                        for j in range(ROW_TILE_SUBLANES)], axis=1)
    acc = rows(outs[0]) * w[:, 0:1]
    for kk in range(1, TOP_K):
        acc = acc + rows(outs[kk]) * w[:, kk:kk + 1]
    x2 = x_ref[...] + gt_ref[...] * acc
    if final:
        fg_ref, o_ref = refs[TOP_K + 3:]
        o_ref[...] = x2 * lax.rsqrt(jnp.mean(x2 * x2, axis=-1, keepdims=True) + NORM_EPS) * fg_ref[...]
    else:
        refs[TOP_K + 3][...] = x2


def _combine(moe_out, weights, x, gt, final_g, S, *, stride, offset, tm=512):
    T = x.shape[0]
    tm = _row_tile(T, S, tm)
    assert stride % tm == 0 and offset % tm == 0
    gt_op, gt_spec = _mod_operand(gt, S, tm)
    tile = pl.BlockSpec((tm, D_MODEL), lambda i: (i, 0))
    ops = [moe_out] * TOP_K + [weights, x, gt_op]
    specs = [pl.BlockSpec((tm * ROW_TILE_SUBLANES, LANES), lambda i, kk=kk: ((kk * stride + offset) // tm + i, 0))
             for kk in range(TOP_K)]
    specs += [pl.BlockSpec((tm, TOP_K), lambda i: (i, 0)), tile, gt_spec]
    if final_g is not None:
        ops.append(final_g.reshape(1, D_MODEL))
        specs.append(pl.BlockSpec((1, D_MODEL), lambda i: (0, 0)))
    return pl.pallas_call(
        functools.partial(_combine_body, final=final_g is not None),
        grid=(T // tm,),
        in_specs=specs,
        out_specs=tile,
        out_shape=jax.ShapeDtypeStruct((T, D_MODEL), F32),
        compiler_params=_params(1),
        name="combine",
    )(*ops)


def _route(logits, T, stride):
    top_val, top_idx = lax.top_k(logits, TOP_K)
    weights = jax.nn.softmax(top_val, axis=-1)
    A = T * TOP_K
    G = max(8, min(256, A // N_EXPERTS))
    ng = -(-A // G) + N_EXPERTS
    flat_e = top_idx.reshape(A).astype(jnp.int32)
    order = jnp.argsort(flat_e).astype(jnp.int32)
    experts = jnp.arange(N_EXPERTS, dtype=jnp.int32)
    counts = jnp.sum(flat_e[:, None] == experts[None, :], axis=0, dtype=jnp.int32)
    padded = (counts + G - 1) // G * G
    pad_end = jnp.cumsum(padded)
    pad_start = pad_end - padded
    start = jnp.cumsum(counts) - counts
    g0 = jnp.arange(ng, dtype=jnp.int32) * G
    group_e = jnp.minimum(jnp.sum(pad_end[None, :] <= g0[:, None], axis=1, dtype=jnp.int32), N_EXPERTS - 1)
    in_group = jnp.arange(G, dtype=jnp.int32)[None, :]
    q = (g0 - pad_start[group_e])[:, None] + in_group
    valid = q < counts[group_e][:, None]
    asg = order[jnp.clip(start[group_e][:, None] + q, 0, A - 1)]
    gather_idx = jnp.where(valid, asg // TOP_K, 0).reshape(ng * G)
    scatter_idx = jnp.where(valid, (asg % TOP_K) * stride + asg // TOP_K, TOP_K * stride + in_group).reshape(ng * G)
    return weights, gather_idx, scatter_idx, group_e, G


def _layer_weights(l, P):
    mu_w, mu_a, mu_g = P["mu_wag"][l]
    firsts = [(mu_w, P["decay_w1"][l]), (mu_a, P["iclr_a1"][l]), (mu_g, P["gate_g1"][l])]
    seconds = [P["decay_w2"][l], P["iclr_a2"][l], P["gate_g2"][l]]
    if l > 0:
        firsts.append((P["vres_mu"][l - 1], P["vres_v1"][l - 1]))
        seconds.append(P["vres_v2"][l - 1])
    n_used = sum(w.shape[1] for _, w in firsts)
    zpad = jnp.zeros((D_MODEL, LORA_COLS - n_used), F32)
    cur = jnp.concatenate([(1.0 - mu)[:, None] * w for mu, w in firsts] + [zpad], axis=1)
    prev = jnp.concatenate([mu[:, None] * w for mu, w in firsts] + [zpad], axis=1)
    w_big = jnp.concatenate([P["w_in"][l], cur, prev], axis=1).astype(BF16)
    w2 = jnp.zeros((LORA_COLS, 4 * RWKV_WIDTH), F32)
    r0 = 0
    for i, s in enumerate(seconds):
        w2 = w2.at[r0:r0 + s.shape[0], i * RWKV_WIDTH:(i + 1) * RWKV_WIDTH].set(s)
        r0 += s.shape[0]
    return w_big, w2.astype(BF16)


def _trunk(x, c, P, wkv0, h_prev, k_buf, v_buf):
    B, S, _ = x.shape
    T = B * S
    depth = P["w_ada"].shape[0]
    n_pairs = ATT_WIDTH // LANES
    unslab = lambda t: jnp.transpose(t.reshape(n_pairs, B, -1, LANES), (1, 2, 0, 3)).reshape(B, -1, ATT_WIDTH)
    cs = jax.nn.silu(c)
    x = x.reshape(T, D_MODEL)
    new_k, new_v, new_wkv, new_shift = [], [], [], []
    v_first = None
    for l in range(depth):
        mod = _mm_rows(cs, P["w_ada"][l].astype(BF16), tn=1024) + P["b_ada"][l]
        sh1, sc1, gt1, sh2, sc2, gt2 = jnp.split(mod, 6, axis=-1)
        w_big, w_lora2 = _layer_weights(l, P)
        qkv, kv_rows, rest = _pre_proj(x, sc1, sh1, w_big, S)
        rest_prev = _mm_rows(h_prev[l], w_big[:, 3 * ATT_WIDTH:])
        x_last = x.reshape(B, S, D_MODEL)[:, -1]
        new_shift.append(_rms_mod(x_last, sc1, sh1))
        keep = min(MAX_WINDOW, S)
        kv_rows = kv_rows.reshape(B, S, N_ATT_HEADS, 2 * HEAD_DIM)[:, S - keep:]
        k_rows, v_rows = kv_rows[..., :HEAD_DIM], kv_rows[..., HEAD_DIM:]

        if k_buf is None:
            att = _att_prompt(qkv, B, P["att_out_g"][l])
        else:
            assert S == 1
            q = unslab(qkv[:n_pairs]).reshape(B, N_ATT_HEADS, HEAD_DIM)
            att = _att_decode(q, k_rows.reshape(B, N_ATT_HEADS, HEAD_DIM), v_rows.reshape(B, N_ATT_HEADS, HEAD_DIM),
                              k_buf, v_buf, l, P["att_out_g"][l])
            att = jnp.transpose(att.reshape(B, n_pairs, LANES), (1, 0, 2))

        zero = jnp.zeros((RWKV_WIDTH,), F32)
        vecs = jnp.stack([P["decay_w0"][l], P["iclr_a0"][l], P["vres_v0"][l - 1] if l > 0 else zero,
                          P["k_k"][l], P["k_a"][l], zero, zero, zero])
        r, lw, k, v, a, b, g = _rwkv_prep(rest, rest_prev, P["mu_rkv"][l].reshape(1, N_RKV), vecs, w_lora2,
                                          v_first, S)
        if l == 0:
            v_first = v
        if S == 1:
            y, wkv = _wkv_step(r, lw, k, v, a, b, wkv0, l)
        else:
            seq = lambda t: t.reshape(B, S, RWKV_WIDTH)
            y, wkv = _wkv(seq(r), seq(lw), seq(k), seq(v), seq(a), seq(b), wkv0[l])
            y = y.reshape(T, RWKV_WIDTH)

        vecs_out = jnp.stack([P["lnx_w"][l], P["lnx_b"][l], P["r_k"][l].reshape(RWKV_WIDTH)] + [zero] * 5)
        rw = jnp.pad(P["router_w"][l], ((0, 0), (0, LANES - N_EXPERTS)))
        rb = jnp.pad(P["router_b"][l], (0, LANES - N_EXPERTS)).reshape(1, LANES)
        x1, h2, logits = _out_proj(att, y, r, k, v, g, x, gt1, sc2, sh2, vecs_out, P["w_out"][l].astype(BF16),
                                   rw, rb, S)

        moe_out, weights, stride, offset = yield h2, logits[:, :N_EXPERTS]
        x = _combine(moe_out, weights, x1, gt2, P["final_g"] if l == depth - 1 else None, S,
                     stride=stride, offset=offset)

        new_k.append(k_rows)
        new_v.append(v_rows)
        new_wkv.append(wkv)
    return x.reshape(B, S, D_MODEL), jnp.stack(new_k), jnp.stack(new_v), jnp.stack(new_wkv), jnp.stack(new_shift)


def kernel(x_prompt, x_sample, c_prompt, c_sample, state_attn_k, state_attn_v, state_wkv, state_shift, w_ada, b_ada, w_in, att_out_g, mu_rkv, mu_wag, decay_w0, decay_w1, decay_w2, iclr_a0, iclr_a1, iclr_a2, gate_g1, gate_g2, vres_mu, vres_v0, vres_v1, vres_v2, k_k, k_a, r_k, lnx_w, lnx_b, w_out, router_w, router_b, moe_w1, moe_b1, moe_w2, moe_b2, final_g):
    P = dict(w_ada=w_ada, b_ada=b_ada, w_in=w_in, att_out_g=att_out_g, mu_rkv=mu_rkv, mu_wag=mu_wag,
             decay_w0=decay_w0, decay_w1=decay_w1, decay_w2=decay_w2, iclr_a0=iclr_a0, iclr_a1=iclr_a1,
             iclr_a2=iclr_a2, gate_g1=gate_g1, gate_g2=gate_g2, vres_mu=vres_mu, vres_v0=vres_v0,
             vres_v1=vres_v1, vres_v2=vres_v2, k_k=k_k, k_a=k_a, r_k=r_k, lnx_w=lnx_w, lnx_b=lnx_b,
             w_out=w_out, router_w=router_w, router_b=router_b, moe_w1=moe_w1, moe_b1=moe_b1,
             moe_w2=moe_w2, moe_b2=moe_b2, final_g=final_g)
    depth = w_ada.shape[0]
    B = x_prompt.shape[0]
    wkv0_prompt = jnp.zeros((depth, B, N_RWKV_HEADS, HEAD_DIM, HEAD_DIM), F32)
    shift0_prompt = jnp.zeros((depth, B, D_MODEL), x_prompt.dtype)
    streams = [_trunk(x_prompt, c_prompt, P, wkv0_prompt, shift0_prompt, None, None),
               _trunk(x_sample, c_sample, P, state_wkv, state_shift, state_attn_k, state_attn_v)]
    pending = [next(s) for s in streams]
    results = [None] * len(streams)
    combine_tile = 512
    for l in range(depth):
        counts = [lg.shape[0] for _, lg in pending]
        T = sum(counts)
        stride = -(-T // combine_tile) * combine_tile
        h2 = jnp.concatenate([h for h, _ in pending] + [jnp.zeros(((stride - T) * ROW_TILE_SUBLANES, LANES), F32)], axis=0)
        logits = jnp.concatenate([lg for _, lg in pending] + [jnp.zeros((stride - T, N_EXPERTS), F32)], axis=0)
        weights, gather_idx, scatter_idx, group_e, G = _route(logits, stride, stride)
        moe_out = _moe_experts(h2, gather_idx, scatter_idx, group_e, moe_w1, moe_b1, moe_w2, moe_b2, l,
                               G=G, n_rows=TOP_K * stride)
        offset, nxt = 0, []
        for i, (s, n) in enumerate(zip(streams, counts)):
            try:
                nxt.append(s.send((moe_out, weights[offset:offset + n], stride, offset)))
            except StopIteration as done:
                results[i] = done.value
            offset += n
        pending = nxt
    (y_p, k_p, v_p, wkv_p, shift_p), (y_s, k_s, v_s, wkv_s, shift_s) = results
    return (y_p, y_s, k_p, v_p, wkv_p, shift_p, k_s, v_s, wkv_s, shift_s)
```
